```python
import math
import jax, jax.numpy as jnp
from jax import lax
import numpy as np

D_MODEL = 1024
BATCH = 8
SEQ = 2048
DEPTH = 2

CHUNK = 64
N_MIXERS = 2
N_RGLRU_LAYERS = (DEPTH + 1) // 2
N_RWKV_LAYERS = DEPTH // 2

RG_WIDTH = ((4 * D_MODEL // 3 + 127) // 128) * 128
RG_HEADS = 16
RG_BLOCK = RG_WIDTH // RG_HEADS
RG_CONV = 4
RG_C = 8.0

RWKV_HEAD = 64
RWKV_HEADS = D_MODEL // RWKV_HEAD
RWKV_DECAY_LORA = 64
RWKV_AAA_LORA = 64
RWKV_GATE_LORA = 128
RWKV_GN_EPS = 64e-5
RWKV_L2_EPS = 1e-12

PEER_HEADS = 8
PEER_NKEYS = 128
PEER_EXPERTS = PEER_NKEYS * PEER_NKEYS
PEER_DKEY = 256
PEER_DHALF = PEER_DKEY // 2
PEER_TOPK = 16
PEER_BLOCK = 128

DEEPNORM_ALPHA = (2 * DEPTH) ** 0.25
DEEPNORM_BETA = (8 * DEPTH) ** -0.25
LN_EPS = 1e-5

kernel_name = "hybrid_rglru_rwkv7_peer_deepnorm"


def layer_norm(x, g, b):
    xf = x.astype(jnp.float32)
    mu = jnp.mean(xf, axis=-1, keepdims=True)
    var = jnp.mean(jnp.square(xf - mu), axis=-1, keepdims=True)
    return ((xf - mu) * lax.rsqrt(var + LN_EPS) * g + b).astype(x.dtype)


def causal_depthwise_conv(x, w, b):
    y = lax.conv_general_dilated(
        x, w[:, None, :], window_strides=(1,), padding=[(RG_CONV - 1, 0)],
        dimension_numbers=("NWC", "WIO", "NWC"), feature_group_count=x.shape[-1])
    return y + b


def _linear_recurrence_combine(left, right):
    a_l, b_l = left
    a_r, b_r = right
    return a_l * a_r, a_r * b_l + b_r


def rglru_mixer(x, w_in, conv_w, conv_b, w_a, b_a, w_x, b_x, lam, w_out):
    bsz, s, _ = x.shape
    h = x @ w_in
    gate_branch = jax.nn.gelu(h[..., :RG_WIDTH])
    xc = causal_depthwise_conv(h[..., RG_WIDTH:], conv_w, conv_b)
    xh = xc.reshape(bsz, s, RG_HEADS, RG_BLOCK).astype(jnp.float32)
    r = jax.nn.sigmoid(jnp.einsum("bshi,hij->bshj", xh, w_a.astype(jnp.float32)) + b_a)
    i = jax.nn.sigmoid(jnp.einsum("bshi,hij->bshj", xh, w_x.astype(jnp.float32)) + b_x)
    log_a = -RG_C * jax.nn.softplus(-lam.astype(jnp.float32)) * r
    a = jnp.exp(log_a)
    u = jnp.sqrt(-jnp.expm1(2.0 * log_a)) * (i * xh)
    _, hs = lax.associative_scan(_linear_recurrence_combine, (a, u), axis=1)
    y = (hs.reshape(bsz, s, RG_WIDTH) * gate_branch).astype(x.dtype)
    return y @ w_out


def _rwkv7_step(state, inp):
    r_t, w_t, k_t, v_t, a_t, b_t = inp
    sa = jnp.einsum("bhvk,bhk->bhv", state, a_t)
    state = (state * w_t[:, :, None, :] + sa[..., None] * b_t[:, :, None, :]
             + v_t[..., None] * k_t[:, :, None, :])
    return state, jnp.einsum("bhvk,bhk->bhv", state, r_t)


def rwkv7_time_mix(x, mix, w_r, w_k, w_v, w0, w1, w2, a0, a1, a2, g1, g2,
                   k_k, k_a, r_k, lnx_g, lnx_b, w_o):
    bsz, s, d = x.shape
    xx = jnp.pad(x, ((0, 0), (1, 0), (0, 0)))[:, :-1] - x
    xr, xw, xk, xv, xa, xg = (x + xx * mix[m] for m in range(6))
    r = xr @ w_r
    k = xk @ w_k
    v = xv @ w_v
    w_log = -jax.nn.softplus(-(w0 + jnp.tanh(xw @ w1) @ w2)) - 0.5
    a = jax.nn.sigmoid(a0 + (xa @ a1) @ a2)
    g = jax.nn.sigmoid(xg @ g1) @ g2

    def heads(t):
        return t.reshape(bsz, s, RWKV_HEADS, RWKV_HEAD).astype(jnp.float32)

    r, w_log, k, v, a = heads(r), heads(w_log), heads(k), heads(v), heads(a)
    kk = k * k_k.reshape(RWKV_HEADS, RWKV_HEAD).astype(jnp.float32)
    kk = kk / jnp.maximum(jnp.sqrt(jnp.sum(kk * kk, axis=-1, keepdims=True)), RWKV_L2_EPS)
    k = k * (1.0 + (a - 1.0) * k_a.reshape(RWKV_HEADS, RWKV_HEAD).astype(jnp.float32))
    decay = jnp.exp(-jnp.exp(w_log))
    seq = tuple(jnp.moveaxis(t, 1, 0) for t in (r, decay, k, v, -kk, kk * a))
    state0 = jnp.zeros((bsz, RWKV_HEADS, RWKV_HEAD, RWKV_HEAD), jnp.float32)
    _, o = lax.scan(_rwkv7_step, state0, seq)
    o = jnp.moveaxis(o, 0, 1)
    mu = jnp.mean(o, axis=-1, keepdims=True)
    var = jnp.mean(jnp.square(o - mu), axis=-1, keepdims=True)
    o = ((o - mu) * lax.rsqrt(var + RWKV_GN_EPS)).reshape(bsz, s, d) * lnx_g + lnx_b
    bonus = jnp.sum(r * k * r_k.astype(jnp.float32), axis=-1, keepdims=True) * v
    o = o + bonus.reshape(bsz, s, d)
    return (o * g).astype(x.dtype) @ w_o


def peer_channel_mix(x, w_q, sub_keys, u, v):
    bsz, s, d = x.shape
    n_tok = bsz * s
    n_blk = n_tok // PEER_BLOCK
    xt = x.reshape(n_tok, d)
    q = (xt @ w_q).reshape(n_blk, PEER_BLOCK, PEER_HEADS, 2, PEER_DHALF)
    keys = sub_keys.astype(jnp.float32)

    def block(args):
        xb, qb = args
        sc = jnp.einsum("thpd,hpnd->thpn", qb.astype(jnp.float32), keys)
        sv, si = lax.top_k(sc, PEER_TOPK)
        comb = (sv[:, :, 0, :, None] + sv[:, :, 1, None, :]).reshape(
            PEER_BLOCK, PEER_HEADS, PEER_TOPK * PEER_TOPK)
        cv, ci = lax.top_k(comb, PEER_TOPK)
        i0 = jnp.take_along_axis(si[:, :, 0], ci // PEER_TOPK, axis=-1)
        i1 = jnp.take_along_axis(si[:, :, 1], ci % PEER_TOPK, axis=-1)
        eid = i0 * PEER_NKEYS + i1
        gate = jax.nn.softmax(cv, axis=-1)
        u_sel = u[eid]
        v_sel = v[eid]
        act = jax.nn.gelu(jnp.einsum("td,thkd->thk", xb, u_sel).astype(jnp.float32))
        coef = (gate * act).astype(xb.dtype)
        return jnp.einsum("thk,thkd->td", coef, v_sel)

    y = lax.map(block, (xt.reshape(n_blk, PEER_BLOCK, d), q))
    return y.reshape(bsz, s, d)


def setup_inputs(seed: int = 0) -> dict:
    key = jax.random.key(seed)
    ks = jax.random.split(key, 40)
    f32 = jnp.float32
    D, W, H, Bk = D_MODEL, RG_WIDTH, RG_HEADS, RG_BLOCK
    nA, nB = N_RGLRU_LAYERS, N_RWKV_LAYERS
    nrm = lambda k, shape, scale: jax.random.normal(k, shape, f32) * scale

    x = jax.random.normal(ks[0], (BATCH, SEQ, D), f32)
    rg_w_in = nrm(ks[1], (nA, D, 2 * W), D ** -0.5)
    rg_conv_w = nrm(ks[2], (nA, RG_CONV, W), RG_CONV ** -0.5)
    rg_conv_b = nrm(ks[3], (nA, W), 0.01)
    rg_w_a = nrm(ks[4], (nA, H, Bk, Bk), Bk ** -0.5)
    rg_b_a = nrm(ks[5], (nA, H, Bk), 0.01)
    rg_w_x = nrm(ks[6], (nA, H, Bk, Bk), Bk ** -0.5)
    rg_b_x = nrm(ks[7], (nA, H, Bk), 0.01)
    a_pow = jax.random.uniform(ks[8], (nA, H, Bk), f32, 0.9, 0.999) ** (1.0 / RG_C)
    rg_lambda = jnp.log(a_pow) - jnp.log1p(-a_pow)
    rg_w_out = nrm(ks[9], (nA, W, D), DEEPNORM_BETA * W ** -0.5)
    rw_mix = jax.random.uniform(ks[10], (nB, 6, D), f32)
    rw_w_r = nrm(ks[11], (nB, D, D), D ** -0.5)
    rw_w_k = nrm(ks[12], (nB, D, D), D ** -0.5)
    rw_w_v = nrm(ks[13], (nB, D, D), D ** -0.5)
    rw_w0 = jax.random.uniform(ks[14], (nB, D), f32, -5.0, -0.5)
    rw_w1 = nrm(ks[15], (nB, D, RWKV_DECAY_LORA), D ** -0.5)
    rw_w2 = nrm(ks[16], (nB, RWKV_DECAY_LORA, D), 0.1 * RWKV_DECAY_LORA ** -0.5)
    rw_a0 = nrm(ks[17], (nB, D), 0.1)
    rw_a1 = nrm(ks[18], (nB, D, RWKV_AAA_LORA), D ** -0.5)
    rw_a2 = nrm(ks[19], (nB, RWKV_AAA_LORA, D), 0.5 * RWKV_AAA_LORA ** -0.5)
    rw_g1 = nrm(ks[20], (nB, D, RWKV_GATE_LORA), D ** -0.5)
    rw_g2 = nrm(ks[21], (nB, RWKV_GATE_LORA, D), RWKV_GATE_LORA ** -0.5)
    rw_k_k = 0.85 + nrm(ks[22], (nB, D), 0.05)
    rw_k_a = 1.0 + nrm(ks[23], (nB, D), 0.05)
    rw_r_k = nrm(ks[24], (nB, RWKV_HEADS, RWKV_HEAD), 0.1)
    rw_lnx_g = 1.0 + nrm(ks[25], (nB, D), 0.01)
    rw_lnx_b = nrm(ks[26], (nB, D), 0.01)
    rw_w_o = nrm(ks[27], (nB, D, D), DEEPNORM_BETA * D ** -0.5)
    peer_w_q = nrm(ks[28], (DEPTH, D, PEER_HEADS * PEER_DKEY), D ** -0.5)
    peer_sub_keys = nrm(ks[29], (DEPTH, PEER_HEADS, 2, PEER_NKEYS, PEER_DHALF), PEER_DHALF ** -0.5)
    peer_u = nrm(ks[30], (DEPTH, PEER_EXPERTS, D), D ** -0.5)
    peer_v = nrm(ks[31], (DEPTH, PEER_EXPERTS, D), DEEPNORM_BETA * PEER_HEADS ** -0.5)
    ln_g = 1.0 + nrm(ks[32], (DEPTH, 2, D), 0.01)
    ln_b = nrm(ks[33], (DEPTH, 2, D), 0.01)
    return {
        "x": x,
        "rg_w_in": rg_w_in, "rg_conv_w": rg_conv_w, "rg_conv_b": rg_conv_b,
        "rg_w_a": rg_w_a, "rg_b_a": rg_b_a, "rg_w_x": rg_w_x, "rg_b_x": rg_b_x,
        "rg_lambda": rg_lambda, "rg_w_out": rg_w_out,
        "rw_mix": rw_mix, "rw_w_r": rw_w_r, "rw_w_k": rw_w_k, "rw_w_v": rw_w_v,
        "rw_w0": rw_w0, "rw_w1": rw_w1, "rw_w2": rw_w2,
        "rw_a0": rw_a0, "rw_a1": rw_a1, "rw_a2": rw_a2,
        "rw_g1": rw_g1, "rw_g2": rw_g2, "rw_k_k": rw_k_k, "rw_k_a": rw_k_a,
        "rw_r_k": rw_r_k, "rw_lnx_g": rw_lnx_g, "rw_lnx_b": rw_lnx_b, "rw_w_o": rw_w_o,
        "peer_w_q": peer_w_q, "peer_sub_keys": peer_sub_keys, "peer_u": peer_u, "peer_v": peer_v,
        "ln_g": ln_g, "ln_b": ln_b,
    }


def reference(x, rg_w_in, rg_conv_w, rg_conv_b, rg_w_a, rg_b_a, rg_w_x, rg_b_x, rg_lambda, rg_w_out,
              rw_mix, rw_w_r, rw_w_k, rw_w_v, rw_w0, rw_w1, rw_w2, rw_a0, rw_a1, rw_a2,
              rw_g1, rw_g2, rw_k_k, rw_k_a, rw_r_k, rw_lnx_g, rw_lnx_b, rw_w_o,
              peer_w_q, peer_sub_keys, peer_u, peer_v, ln_g, ln_b):
    for i in range(DEPTH):
        j = i // N_MIXERS
        if i % N_MIXERS == 0:
            m = rglru_mixer(x, rg_w_in[j], rg_conv_w[j], rg_conv_b[j], rg_w_a[j], rg_b_a[j],
                            rg_w_x[j], rg_b_x[j], rg_lambda[j], rg_w_out[j])
        else:
            m = rwkv7_time_mix(x, rw_mix[j], rw_w_r[j], rw_w_k[j], rw_w_v[j], rw_w0[j], rw_w1[j],
                               rw_w2[j], rw_a0[j], rw_a1[j], rw_a2[j], rw_g1[j], rw_g2[j],
                               rw_k_k[j], rw_k_a[j], rw_r_k[j], rw_lnx_g[j], rw_lnx_b[j], rw_w_o[j])
        x = layer_norm(DEEPNORM_ALPHA * x + m, ln_g[i, 0], ln_b[i, 0])
        c = peer_channel_mix(x, peer_w_q[i], peer_sub_keys[i], peer_u[i], peer_v[i])
        x = layer_norm(DEEPNORM_ALPHA * x + c, ln_g[i, 1], ln_b[i, 1])
    return x
```

```python
import functools
import math
import jax, jax.numpy as jnp
from jax import lax
from jax.experimental import pallas as pl
from jax.experimental.pallas import tpu as pltpu

D_MODEL = 1024
DEPTH = 2
RG_WIDTH = 1408
RG_HEADS = 16
RG_BLOCK = 88
RG_CONV = 4
RG_C = 8.0
RWKV_HEAD = 64
RWKV_HEADS = 16
RWKV_GN_EPS = 64e-5
RWKV_L2_EPS = 1e-12
PEER_HEADS = 8
PEER_NKEYS = 128
PEER_DHALF = 128
PEER_TOPK = 16
PEER_BLOCK = 128
ALPHA = (2 * DEPTH) ** 0.25
LN_EPS = 1e-5

LANES = 128
SUBLANES = 8
VMEM_LIMIT = 56 << 20

f32 = jnp.float32
bf16 = jnp.bfloat16


def _gelu_tanh(x):
    return 0.5 * x * (1.0 + jnp.tanh(math.sqrt(2.0 / math.pi) * (x + 0.044715 * (x * x * x))))


def _layer_norm_rows(z, g, b):
    mu = jnp.mean(z, axis=-1, keepdims=True)
    zc = z - mu
    var = jnp.mean(zc * zc, axis=-1, keepdims=True)
    return zc * lax.rsqrt(var + LN_EPS) * g + b


def _top_rows(s, k):
    rows = []
    for _ in range(k):
        m = jnp.max(s, axis=0, keepdims=True)
        rows.append(m)
        s = jnp.where(s == m, -jnp.inf, s)
    return rows


def _peer_select_kernel(x_ref, wq_ref, keys_ref, s0_ref, s1_ref, p0_ref, p1_ref, tau_ref, q_ref):
    h = pl.program_id(1)

    @pl.when(h == 0)
    def _():
        q_ref[...] = jnp.dot(x_ref[...].astype(bf16), wq_ref[...], preferred_element_type=f32)

    def scores(p):
        off = pl.multiple_of((2 * h + p) * PEER_DHALF, PEER_DHALF)
        qhp = q_ref[:, pl.ds(off, PEER_DHALF)].astype(bf16)
        return lax.dot_general(keys_ref[0, p], qhp, (((1,), (1,)), ((), ())),
                               preferred_element_type=f32)

    s0 = scores(0)
    s1 = scores(1)
    a = _top_rows(s0, PEER_TOPK)
    b = _top_rows(s1, PEER_TOPK)
    row = lax.broadcasted_iota(jnp.int32, (PEER_TOPK, s0.shape[1]), 0)
    bmat = jnp.zeros((PEER_TOPK, s0.shape[1]), f32)
    for j in range(PEER_TOPK):
        bmat = jnp.where(row == j, b[j], bmat)
    cand = jnp.concatenate([a[0] + bmat] + [a[i] + bmat[:8] for i in range(1, PEER_TOPK)], axis=0)
    tau = _top_rows(cand, PEER_TOPK)[-1]
    cmax = a[0] + b[0]
    z = jnp.sum(jnp.where(cand >= tau, jnp.exp(cand - cmax), 0.0), axis=0, keepdims=True)
    s0_ref[0] = s0
    s1_ref[0] = s1
    p0_ref[0] = jnp.exp(s0 - a[0]) / z
    p1_ref[0] = jnp.exp(s1 - b[0])
    tau_ref[0] = tau


def peer_select(x, wq_b, keys_b, *, tq):
    n, d = x.shape
    nh = PEER_HEADS
    stat = jax.ShapeDtypeStruct((nh, PEER_NKEYS, n), f32)
    stat_spec = pl.BlockSpec((1, PEER_NKEYS, tq), lambda i, h: (h, 0, i))
    return pl.pallas_call(
        _peer_select_kernel,
        grid=(n // tq, nh),
        in_specs=[pl.BlockSpec((tq, d), lambda i, h: (i, 0)),
                  pl.BlockSpec((d, 2 * nh * PEER_DHALF), lambda i, h: (0, 0)),
                  pl.BlockSpec((1, 2, PEER_NKEYS, PEER_DHALF), lambda i, h: (h, 0, 0, 0))],
        out_specs=[stat_spec, stat_spec, stat_spec, stat_spec,
                   pl.BlockSpec((1, 1, tq), lambda i, h: (h, 0, i))],
        out_shape=[stat, stat, stat, stat, jax.ShapeDtypeStruct((nh, 1, n), f32)],
        scratch_shapes=[pltpu.VMEM((tq, 2 * nh * PEER_DHALF), f32)],
        compiler_params=pltpu.CompilerParams(
            dimension_semantics=("arbitrary", "arbitrary"), vmem_limit_bytes=VMEM_LIMIT),
        name="peer_select",
    )(x, wq_b, keys_b)


def _peer_main_kernel(xb_ref, x_ref, s0_ref, s1_ref, p0_ref, p1_ref, tau_ref, u_ref, vt_ref,
                      g_ref, b_ref, o_ref, acc_ref, at_ref, ga_ref, *, te, tt):
    j = pl.program_id(1)
    n_i0 = te // PEER_NKEYS
    n_c = tt // LANES

    @pl.when(j == 0)
    def _():
        acc_ref[...] = jnp.zeros_like(acc_ref)

    at_ref[...] = lax.dot_general(u_ref[...], xb_ref[...], (((1,), (1,)), ((), ())),
                                  preferred_element_type=f32)

    def body(idx, carry):
        grp = idx // n_c
        c = idx % n_c
        lanes = pl.ds(pl.multiple_of(c * LANES, LANES), LANES)
        i0s = pl.ds(pl.multiple_of(j * n_i0 + grp * SUBLANES, SUBLANES), SUBLANES)
        for r in range(SUBLANES):
            g = jnp.zeros((PEER_NKEYS, LANES), f32)
            for h in range(PEER_HEADS):
                s0row = s0_ref[h, i0s, lanes][r:r + 1]
                p0row = p0_ref[h, i0s, lanes][r:r + 1]
                sel = (s1_ref[h, :, lanes] + s0row) >= tau_ref[h, :, lanes]
                g = g + jnp.where(sel, p1_ref[h, :, lanes] * p0row, 0.0)
            rows = pl.ds(pl.multiple_of((grp * SUBLANES + r) * PEER_NKEYS, PEER_NKEYS), PEER_NKEYS)
            ga_ref[rows, lanes] = (_gelu_tanh(at_ref[rows, lanes]) * g).astype(bf16)
        return carry

    lax.fori_loop(0, (n_i0 // SUBLANES) * n_c, body, 0)
    acc_ref[...] += jnp.dot(vt_ref[...], ga_ref[...], preferred_element_type=f32)

    @pl.when(j == pl.num_programs(1) - 1)
    def _():
        z = ALPHA * x_ref[...] + acc_ref[...].T
        o_ref[...] = _layer_norm_rows(z, g_ref[...], b_ref[...])


def peer_main(x, xb, stats, u_b, vt_b, ln_g, ln_b, *, tt, te):
    n, d = x.shape
    e = u_b.shape[0]
    nh = PEER_HEADS
    s0, s1, p0, p1, tau = stats
    stat_spec = pl.BlockSpec((nh, PEER_NKEYS, tt), lambda i, j: (0, 0, i))
    return pl.pallas_call(
        functools.partial(_peer_main_kernel, te=te, tt=tt),
        grid=(n // tt, e // te),
        in_specs=[pl.BlockSpec((tt, d), lambda i, j: (i, 0)),
                  pl.BlockSpec((tt, d), lambda i, j: (i, 0)),
                  stat_spec, stat_spec, stat_spec, stat_spec,
                  pl.BlockSpec((nh, 1, tt), lambda i, j: (0, 0, i)),
                  pl.BlockSpec((te, d), lambda i, j: (j, 0)),
                  pl.BlockSpec((d, te), lambda i, j: (0, j)),
                  pl.BlockSpec((1, d), lambda i, j: (0, 0)),
                  pl.BlockSpec((1, d), lambda i, j: (0, 0))],
        out_specs=pl.BlockSpec((tt, d), lambda i, j: (i, 0)),
        out_shape=jax.ShapeDtypeStruct((n, d), f32),
        scratch_shapes=[pltpu.VMEM((d, tt), f32), pltpu.VMEM((te, tt), f32), pltpu.VMEM((te, tt), bf16)],
        compiler_params=pltpu.CompilerParams(
            dimension_semantics=("arbitrary", "arbitrary"), vmem_limit_bytes=VMEM_LIMIT),
        name="peer_main",
    )(xb, x, s0, s1, p0, p1, tau, u_b, vt_b, ln_g.reshape(1, d), ln_b.reshape(1, d))


def peer_layer(x, w_q, sub_keys, u, v, ln_g, ln_b, *, tq=256, tt=512, te=1024):
    stats = peer_select(x, w_q.astype(bf16), sub_keys.astype(bf16), tq=tq)
    return peer_main(x, x.astype(bf16), stats, u.astype(bf16), v.T.astype(bf16), ln_g, ln_b, tt=tt, te=te)


def _ln_kernel(x_ref, m_ref, g_ref, b_ref, o_ref):
    o_ref[...] = _layer_norm_rows(ALPHA * x_ref[...] + m_ref[...], g_ref[...], b_ref[...])


def residual_ln(x, m, g, b):
    n, d = x.shape
    tb = 512
    return pl.pallas_call(
        _ln_kernel,
        grid=(n // tb,),
        in_specs=[pl.BlockSpec((tb, d), lambda i: (i, 0)),
                  pl.BlockSpec((tb, d), lambda i: (i, 0)),
                  pl.BlockSpec((1, d), lambda i: (0, 0)),
                  pl.BlockSpec((1, d), lambda i: (0, 0))],
        out_specs=pl.BlockSpec((tb, d), lambda i: (i, 0)),
        out_shape=jax.ShapeDtypeStruct((n, d), f32),
        name="residual_ln",
    )(x, m, g.reshape(1, d), b.reshape(1, d))


def causal_depthwise_conv(x, w, b):
    y = lax.conv_general_dilated(
        x, w[:, None, :], window_strides=(1,), padding=[(RG_CONV - 1, 0)],
        dimension_numbers=("NWC", "WIO", "NWC"), feature_group_count=x.shape[-1])
    return y + b


def _comb(left, right):
    a_l, b_l = left
    a_r, b_r = right
    return a_l * a_r, a_r * b_l + b_r


def rglru_mixer(x, w_in, conv_w, conv_b, w_a, b_a, w_x, b_x, lam, w_out):
    bsz, s, _ = x.shape
    h = x @ w_in
    gate_branch = jax.nn.gelu(h[..., :RG_WIDTH])
    xc = causal_depthwise_conv(h[..., RG_WIDTH:], conv_w, conv_b)
    xh = xc.reshape(bsz, s, RG_HEADS, RG_BLOCK)
    r = jax.nn.sigmoid(jnp.einsum("bshi,hij->bshj", xh, w_a) + b_a)
    i = jax.nn.sigmoid(jnp.einsum("bshi,hij->bshj", xh, w_x) + b_x)
    log_a = -RG_C * jax.nn.softplus(-lam) * r
    a = jnp.exp(log_a)
    u = jnp.sqrt(-jnp.expm1(2.0 * log_a)) * (i * xh)
    _, hs = lax.associative_scan(_comb, (a, u), axis=1)
    y = hs.reshape(bsz, s, RG_WIDTH) * gate_branch
    return y @ w_out


def _rwkv7_step(state, inp):
    r_t, w_t, k_t, v_t, a_t, b_t = inp
    sa = jnp.einsum("bhvk,bhk->bhv", state, a_t)
    state = (state * w_t[:, :, None, :] + sa[..., None] * b_t[:, :, None, :]
             + v_t[..., None] * k_t[:, :, None, :])
    return state, jnp.einsum("bhvk,bhk->bhv", state, r_t)


def rwkv7_time_mix(x, mix, w_r, w_k, w_v, w0, w1, w2, a0, a1, a2, g1, g2,
                   k_k, k_a, r_k, lnx_g, lnx_b, w_o):
    bsz, s, d = x.shape
    xx = jnp.pad(x, ((0, 0), (1, 0), (0, 0)))[:, :-1] - x
    xr, xw, xk, xv, xa, xg = (x + xx * mix[m] for m in range(6))
    r = xr @ w_r
    k = xk @ w_k
    v = xv @ w_v
    w_log = -jax.nn.softplus(-(w0 + jnp.tanh(xw @ w1) @ w2)) - 0.5
    a = jax.nn.sigmoid(a0 + (xa @ a1) @ a2)
    g = jax.nn.sigmoid(xg @ g1) @ g2

    def heads(t):
        return t.reshape(bsz, s, RWKV_HEADS, RWKV_HEAD)

    r, w_log, k, v, a = heads(r), heads(w_log), heads(k), heads(v), heads(a)
    kk = k * k_k.reshape(RWKV_HEADS, RWKV_HEAD)
    kk = kk / jnp.maximum(jnp.sqrt(jnp.sum(kk * kk, axis=-1, keepdims=True)), RWKV_L2_EPS)
    k = k * (1.0 + (a - 1.0) * k_a.reshape(RWKV_HEADS, RWKV_HEAD))
    decay = jnp.exp(-jnp.exp(w_log))
    seq = tuple(jnp.moveaxis(t, 1, 0) for t in (r, decay, k, v, -kk, kk * a))
    state0 = jnp.zeros((bsz, RWKV_HEADS, RWKV_HEAD, RWKV_HEAD), jnp.float32)
    _, o = lax.scan(_rwkv7_step, state0, seq)
    o = jnp.moveaxis(o, 0, 1)
    mu = jnp.mean(o, axis=-1, keepdims=True)
    var = jnp.mean(jnp.square(o - mu), axis=-1, keepdims=True)
    o = ((o - mu) * lax.rsqrt(var + RWKV_GN_EPS)).reshape(bsz, s, d) * lnx_g + lnx_b
    bonus = jnp.sum(r * k * r_k, axis=-1, keepdims=True) * v
    o = o + bonus.reshape(bsz, s, d)
    return (o * g) @ w_o


def kernel(x, rg_w_in, rg_conv_w, rg_conv_b, rg_w_a, rg_b_a, rg_w_x, rg_b_x, rg_lambda, rg_w_out, rw_mix, rw_w_r, rw_w_k, rw_w_v, rw_w0, rw_w1, rw_w2, rw_a0, rw_a1, rw_a2, rw_g1, rw_g2, rw_k_k, rw_k_a, rw_r_k, rw_lnx_g, rw_lnx_b, rw_w_o, peer_w_q, peer_sub_keys, peer_u, peer_v, ln_g, ln_b):
    bsz, s, d = x.shape
    for i in range(DEPTH):
        j = i // 2
        if i % 2 == 0:
            m = rglru_mixer(x, rg_w_in[j], rg_conv_w[j], rg_conv_b[j], rg_w_a[j], rg_b_a[j],
                            rg_w_x[j], rg_b_x[j], rg_lambda[j], rg_w_out[j])
        else:
            m = rwkv7_time_mix(x, rw_mix[j], rw_w_r[j], rw_w_k[j], rw_w_v[j], rw_w0[j], rw_w1[j],
                               rw_w2[j], rw_a0[j], rw_a1[j], rw_a2[j], rw_g1[j], rw_g2[j],
                               rw_k_k[j], rw_k_a[j], rw_r_k[j], rw_lnx_g[j], rw_lnx_b[j], rw_w_o[j])
        x1 = residual_ln(x.reshape(-1, d), m.reshape(-1, d), ln_g[i, 0], ln_b[i, 0])
        x2 = peer_layer(x1, peer_w_q[i], peer_sub_keys[i], peer_u[i], peer_v[i], ln_g[i, 1], ln_b[i, 1])
        x = x2.reshape(bsz, s, d)
    return x
```

```python
import functools
import math
import jax, jax.numpy as jnp
from jax import lax
from jax.experimental import pallas as pl
from jax.experimental.pallas import tpu as pltpu

D_MODEL = 1024
DEPTH = 2
RG_WIDTH = 1408
RG_HEADS = 16
RG_BLOCK = 88
RG_CONV = 4
RG_C = 8.0
RWKV_HEAD = 64
RWKV_HEADS = 16
RWKV_GN_EPS = 64e-5
RWKV_L2_EPS = 1e-12
PEER_HEADS = 8
PEER_NKEYS = 128
PEER_DHALF = 128
PEER_TOPK = 16
PEER_BLOCK = 128
ALPHA = (2 * DEPTH) ** 0.25
LN_EPS = 1e-5

LANES = 128
SUBLANES = 8
VMEM_LIMIT = 56 << 20

f32 = jnp.float32
bf16 = jnp.bfloat16


def _gelu_tanh(x):
    return 0.5 * x * (1.0 + jnp.tanh(math.sqrt(2.0 / math.pi) * (x + 0.044715 * (x * x * x))))


def _layer_norm_rows(z, g, b):
    mu = jnp.mean(z, axis=-1, keepdims=True)
    zc = z - mu
    var = jnp.mean(zc * zc, axis=-1, keepdims=True)
    return zc * lax.rsqrt(var + LN_EPS) * g + b


def _top_rows(s, k):
    rows = []
    for _ in range(k):
        m = jnp.max(s, axis=0, keepdims=True)
        rows.append(m)
        s = jnp.where(s == m, -jnp.inf, s)
    return rows


def _peer_select_kernel(x_ref, wq_ref, keys_ref, s0_ref, s1_ref, p0_ref, p1_ref, tau_ref, q_ref):
    h = pl.program_id(1)

    @pl.when(h == 0)
    def _():
        q_ref[...] = jnp.dot(x_ref[...].astype(bf16), wq_ref[...], preferred_element_type=f32)

    def scores(p):
        off = pl.multiple_of((2 * h + p) * PEER_DHALF, PEER_DHALF)
        qhp = q_ref[:, pl.ds(off, PEER_DHALF)].astype(bf16)
        return lax.dot_general(keys_ref[0, p], qhp, (((1,), (1,)), ((), ())),
                               preferred_element_type=f32)

    s0 = scores(0)
    s1 = scores(1)
    a = _top_rows(s0, PEER_TOPK)
    b = _top_rows(s1, PEER_TOPK)
    row = lax.broadcasted_iota(jnp.int32, (PEER_TOPK, s0.shape[1]), 0)
    bmat = jnp.zeros((PEER_TOPK, s0.shape[1]), f32)
    for j in range(PEER_TOPK):
        bmat = jnp.where(row == j, b[j], bmat)
    cand = jnp.concatenate([a[0] + bmat] + [a[i] + bmat[:8] for i in range(1, PEER_TOPK)], axis=0)
    tau = _top_rows(cand, PEER_TOPK)[-1]
    cmax = a[0] + b[0]
    z = jnp.sum(jnp.where(cand >= tau, jnp.exp(cand - cmax), 0.0), axis=0, keepdims=True)
    s0_ref[0] = s0
    s1_ref[0] = s1
    p0_ref[0] = jnp.exp(s0 - a[0]) / z
    p1_ref[0] = jnp.exp(s1 - b[0])
    tau_ref[0] = tau


def peer_select(x, wq_b, keys_b, *, tq):
    n, d = x.shape
    nh = PEER_HEADS
    stat = jax.ShapeDtypeStruct((nh, PEER_NKEYS, n), f32)
    stat_spec = pl.BlockSpec((1, PEER_NKEYS, tq), lambda i, h: (h, 0, i))
    return pl.pallas_call(
        _peer_select_kernel,
        grid=(n // tq, nh),
        in_specs=[pl.BlockSpec((tq, d), lambda i, h: (i, 0)),
                  pl.BlockSpec((d, 2 * nh * PEER_DHALF), lambda i, h: (0, 0)),
                  pl.BlockSpec((1, 2, PEER_NKEYS, PEER_DHALF), lambda i, h: (h, 0, 0, 0))],
        out_specs=[stat_spec, stat_spec, stat_spec, stat_spec,
                   pl.BlockSpec((1, 1, tq), lambda i, h: (h, 0, i))],
        out_shape=[stat, stat, stat, stat, jax.ShapeDtypeStruct((nh, 1, n), f32)],
        scratch_shapes=[pltpu.VMEM((tq, 2 * nh * PEER_DHALF), f32)],
        compiler_params=pltpu.CompilerParams(
            dimension_semantics=("arbitrary", "arbitrary"), vmem_limit_bytes=VMEM_LIMIT),
        name="peer_select",
    )(x, wq_b, keys_b)


def _peer_main_kernel(xb_ref, x_ref, s0_ref, s1_ref, p0_ref, p1_ref, tau_ref, u_ref, vt_ref,
                      g_ref, b_ref, o_ref, acc_ref, at_ref, ga_ref, *, te, tt):
    j = pl.program_id(1)
    n_i0 = te // PEER_NKEYS
    n_c = tt // LANES

    @pl.when(j == 0)
    def _():
        acc_ref[...] = jnp.zeros_like(acc_ref)

    at_ref[...] = lax.dot_general(u_ref[...], xb_ref[...], (((1,), (1,)), ((), ())),
                                  preferred_element_type=f32)

    def body(idx, carry):
        grp = idx // n_c
        c = idx % n_c
        lanes = pl.ds(pl.multiple_of(c * LANES, LANES), LANES)
        i0s = pl.ds(pl.multiple_of(j * n_i0 + grp * SUBLANES, SUBLANES), SUBLANES)
        for r in range(SUBLANES):
            g = jnp.zeros((PEER_NKEYS, LANES), f32)
            for h in range(PEER_HEADS):
                s0row = s0_ref[h, i0s, lanes][r:r + 1]
                p0row = p0_ref[h, i0s, lanes][r:r + 1]
                sel = (s1_ref[h, :, lanes] + s0row) >= tau_ref[h, :, lanes]
                g = g + jnp.where(sel, p1_ref[h, :, lanes] * p0row, 0.0)
            rows = pl.ds(pl.multiple_of((grp * SUBLANES + r) * PEER_NKEYS, PEER_NKEYS), PEER_NKEYS)
            ga_ref[rows, lanes] = (_gelu_tanh(at_ref[rows, lanes]) * g).astype(bf16)
        return carry

    lax.fori_loop(0, (n_i0 // SUBLANES) * n_c, body, 0)
    acc_ref[...] += jnp.dot(vt_ref[...], ga_ref[...], preferred_element_type=f32)

    @pl.when(j == pl.num_programs(1) - 1)
    def _():
        z = ALPHA * x_ref[...] + acc_ref[...].T
        o_ref[...] = _layer_norm_rows(z, g_ref[...], b_ref[...])


def peer_main(x, xb, stats, u_b, vt_b, ln_g, ln_b, *, tt, te):
    n, d = x.shape
    e = u_b.shape[0]
    nh = PEER_HEADS
    s0, s1, p0, p1, tau = stats
    stat_spec = pl.BlockSpec((nh, PEER_NKEYS, tt), lambda i, j: (0, 0, i))
    return pl.pallas_call(
        functools.partial(_peer_main_kernel, te=te, tt=tt),
        grid=(n // tt, e // te),
        in_specs=[pl.BlockSpec((tt, d), lambda i, j: (i, 0)),
                  pl.BlockSpec((tt, d), lambda i, j: (i, 0)),
                  stat_spec, stat_spec, stat_spec, stat_spec,
                  pl.BlockSpec((nh, 1, tt), lambda i, j: (0, 0, i)),
                  pl.BlockSpec((te, d), lambda i, j: (j, 0)),
                  pl.BlockSpec((d, te), lambda i, j: (0, j)),
                  pl.BlockSpec((1, d), lambda i, j: (0, 0)),
                  pl.BlockSpec((1, d), lambda i, j: (0, 0))],
        out_specs=pl.BlockSpec((tt, d), lambda i, j: (i, 0)),
        out_shape=jax.ShapeDtypeStruct((n, d), f32),
        scratch_shapes=[pltpu.VMEM((d, tt), f32), pltpu.VMEM((te, tt), f32), pltpu.VMEM((te, tt), bf16)],
        compiler_params=pltpu.CompilerParams(
            dimension_semantics=("arbitrary", "arbitrary"), vmem_limit_bytes=VMEM_LIMIT),
        name="peer_main",
    )(xb, x, s0, s1, p0, p1, tau, u_b, vt_b, ln_g.reshape(1, d), ln_b.reshape(1, d))


def peer_layer(x, w_q, sub_keys, u, v, ln_g, ln_b, *, tq=256, tt=512, te=1024):
    stats = peer_select(x, w_q.astype(bf16), sub_keys.astype(bf16), tq=tq)
    return peer_main(x, x.astype(bf16), stats, u.astype(bf16), v.T.astype(bf16), ln_g, ln_b, tt=tt, te=te)


def _rglru_kernel(x_ref, win_ref, cw_ref, vec_ref, wg_ref, wout_ref, lng_ref, lnb_ref, o_ref,
                  tail_ref, h_ref, *, tiles_per_seq):
    i = pl.program_id(0)
    w = RG_WIDTH
    x = x_ref[...]
    ts = x.shape[0]

    @pl.when(i % tiles_per_seq == 0)
    def _():
        tail_ref[...] = jnp.zeros_like(tail_ref)
        h_ref[...] = jnp.zeros_like(h_ref)

    conv_b, b_a, b_x, lam = (vec_ref[n:n + 1, :] for n in range(4))
    hin = jnp.dot(x.astype(bf16), win_ref[...], preferred_element_type=f32)
    gate_branch = _gelu_tanh(hin[:, :w])
    hx = hin[:, w:]

    row = lax.broadcasted_iota(jnp.int32, (ts, w), 0)
    row8 = lax.broadcasted_iota(jnp.int32, (SUBLANES, w), 0)
    tail = tail_ref[...]
    xc = hx * cw_ref[RG_CONV - 1:RG_CONV, :] + conv_b
    for s in range(1, RG_CONV):
        rolled = pltpu.roll(hx, s, axis=0)
        head = jnp.where(row8 < s, pltpu.roll(tail, s, axis=0), rolled[:SUBLANES])
        shifted = jnp.concatenate([head, rolled[SUBLANES:]], axis=0)
        xc = xc + shifted * cw_ref[RG_CONV - 1 - s:RG_CONV - s, :]
    tail_ref[...] = hx[ts - SUBLANES:]

    gates = jnp.dot(xc.astype(bf16), wg_ref[...], preferred_element_type=f32)
    r = _sigmoid(gates[:, :w] + b_a)
    ig = _sigmoid(gates[:, w:] + b_x)
    log_a = (-RG_C * _softplus(-lam)) * r
    a = jnp.exp(log_a)
    u = jnp.sqrt(-jnp.tanh(log_a) * (a * a + 1.0)) * (ig * xc)

    s = 1
    while s < ts:
        keep = row >= s
        a_sh = jnp.where(keep, pltpu.roll(a, s, axis=0), 1.0)
        u_sh = jnp.where(keep, pltpu.roll(u, s, axis=0), 0.0)
        u = a * u_sh + u
        a = a * a_sh
        s *= 2
    hs = u + a * h_ref[SUBLANES - 1:SUBLANES, :]
    h_ref[...] = hs[ts - SUBLANES:]

    m = jnp.dot((hs * gate_branch).astype(bf16), wout_ref[...], preferred_element_type=f32)
    o_ref[...] = _layer_norm_rows(ALPHA * x + m, lng_ref[...], lnb_ref[...])


def rglru_layer(x, seq, w_in, conv_w, conv_b, w_a, b_a, w_x, b_x, lam, w_out, ln_g, ln_b, *, ts=256):
    n, d = x.shape
    w = RG_WIDTH
    blockdiag = lambda t: jax.scipy.linalg.block_diag(*t)
    w_gates = jnp.concatenate([blockdiag(w_a), blockdiag(w_x)], axis=1).astype(bf16)
    vecs = jnp.stack([conv_b, b_a.reshape(w), b_x.reshape(w), lam.reshape(w)])
    consts = (w_in.astype(bf16), conv_w, vecs, w_gates, w_out.astype(bf16), ln_g.reshape(1, d), ln_b.reshape(1, d))
    full = lambda arr: pl.BlockSpec(arr.shape, lambda i: (0,) * arr.ndim, pipeline_mode=pl.Buffered(1))
    tok = pl.BlockSpec((ts, d), lambda i: (i, 0))
    return pl.pallas_call(
        functools.partial(_rglru_kernel, tiles_per_seq=seq // ts),
        grid=(n // ts,),
        in_specs=[tok] + [full(c) for c in consts],
        out_specs=tok,
        out_shape=jax.ShapeDtypeStruct((n, d), f32),
        scratch_shapes=[pltpu.VMEM((SUBLANES, w), f32), pltpu.VMEM((SUBLANES, w), f32)],
        compiler_params=pltpu.CompilerParams(
            dimension_semantics=("arbitrary",), vmem_limit_bytes=VMEM_LIMIT),
        name="rglru",
    )(x, *consts)


def _sigmoid(x):
    return 1.0 / (1.0 + jnp.exp(-x))


def _softplus(x):
    return jnp.maximum(x, 0.0) + jnp.log1p(jnp.exp(-jnp.abs(x)))


def _head_sums(t, ones_ref):
    hi = t.astype(bf16)
    lo = (t - hi.astype(f32)).astype(bf16)
    return (jnp.dot(hi, ones_ref[...], preferred_element_type=f32)
            + jnp.dot(lo, ones_ref[...], preferred_element_type=f32))


def _rwkv_proj_kernel(x_ref, xp_ref, mix_ref, wr_ref, wk_ref, wv_ref, w1_ref, w2_ref, a1_ref, a2_ref,
                      g1_ref, g2_ref, vec_ref, ones_ref,
                      r_ref, w_ref, k_ref, v_ref, a_ref, b_ref, g_ref, *, tiles_per_seq):
    i = pl.program_id(0)
    x = x_ref[...]
    ts = x.shape[0]
    prev_last = jnp.where(i % tiles_per_seq == 0, 0.0, xp_ref[SUBLANES - 1:SUBLANES, :])
    row = lax.broadcasted_iota(jnp.int32, x.shape, 0)
    xprev = jnp.where(row == 0, prev_last, pltpu.roll(x, 1, axis=0))
    xx = xprev - x

    def mixed(m):
        return (x + xx * mix_ref[m:m + 1, :]).astype(bf16)

    def mm(a, w_ref_):
        return jnp.dot(a, w_ref_[...], preferred_element_type=f32)

    w0, a0, k_k, k_a = (vec_ref[n:n + 1, :] for n in range(4))
    r = mm(mixed(0), wr_ref)
    k = mm(mixed(2), wk_ref)
    v = mm(mixed(3), wv_ref)
    lw = mm(jnp.tanh(mm(mixed(1), w1_ref)).astype(bf16), w2_ref)
    w_log = -_softplus(-(w0 + lw)) - 0.5
    gate_a = _sigmoid(a0 + mm(mm(mixed(4), a1_ref).astype(bf16), a2_ref))
    g = mm(_sigmoid(mm(mixed(5), g1_ref)).astype(bf16), g2_ref)
    kk = k * k_k
    kk = kk / jnp.maximum(jnp.sqrt(_head_sums(kk * kk, ones_ref)), RWKV_L2_EPS)
    r_ref[...] = r
    w_ref[...] = jnp.exp(-jnp.exp(w_log))
    k_ref[...] = k * (1.0 + (gate_a - 1.0) * k_a)
    v_ref[...] = v
    a_ref[...] = -kk
    b_ref[...] = kk * gate_a
    g_ref[...] = g


def rwkv_proj(x, seq, mix, w_r, w_k, w_v, w1, w2, a1, a2, g1, g2, vecs, ones, *, ts):
    n, d = x.shape
    full = lambda arr: pl.BlockSpec(arr.shape, lambda i: (0,) * arr.ndim)
    tok = pl.BlockSpec((ts, d), lambda i: (i, 0))
    prev = pl.BlockSpec((SUBLANES, d), lambda i: (jnp.maximum(i * (ts // SUBLANES) - 1, 0), 0))
    weights = (mix, w_r, w_k, w_v, w1, w2, a1, a2, g1, g2, vecs, ones)
    out = jax.ShapeDtypeStruct((n, d), f32)
    return pl.pallas_call(
        functools.partial(_rwkv_proj_kernel, tiles_per_seq=seq // ts),
        grid=(n // ts,),
        in_specs=[tok, prev] + [full(w) for w in weights],
        out_specs=[tok] * 7,
        out_shape=[out] * 7,
        compiler_params=pltpu.CompilerParams(
            dimension_semantics=("arbitrary",), vmem_limit_bytes=VMEM_LIMIT),
        name="rwkv_proj",
    )(x, x, *weights)


def _rwkv_scan_kernel(r_ref, w_ref, k_ref, v_ref, a_ref, b_ref, o_ref, s_ref):
    nk = s_ref.shape[0]

    @pl.when(pl.program_id(0) == 0)
    def _():
        s_ref[...] = jnp.zeros_like(s_ref)

    def step(t, carry):
        sa0 = jnp.zeros(s_ref.shape[1:], f32)
        sa1 = jnp.zeros(s_ref.shape[1:], f32)
        for kk in range(0, nk, 2):
            sa0 = sa0 + s_ref[kk] * a_ref[t, kk:kk + 1, :]
            sa1 = sa1 + s_ref[kk + 1] * a_ref[t, kk + 1:kk + 2, :]
        sa = sa0 + sa1
        vt = v_ref[t]
        o0 = jnp.zeros(s_ref.shape[1:], f32)
        o1 = jnp.zeros(s_ref.shape[1:], f32)
        for kk in range(nk):
            new = (s_ref[kk] * w_ref[t, kk:kk + 1, :]
                   + (sa * b_ref[t, kk:kk + 1, :] + vt * k_ref[t, kk:kk + 1, :]))
            s_ref[kk] = new
            if kk % 2 == 0:
                o0 = o0 + new * r_ref[t, kk:kk + 1, :]
            else:
                o1 = o1 + new * r_ref[t, kk:kk + 1, :]
        o = o0 + o1
        mu = jnp.mean(o, axis=0, keepdims=True)
        oc = o - mu
        var = jnp.mean(oc * oc, axis=0, keepdims=True)
        o_ref[t] = oc * lax.rsqrt(var + RWKV_GN_EPS)
        return carry

    lax.fori_loop(0, r_ref.shape[0], step, 0)


def rwkv_scan(r, w, k, v, a, b, *, tc):
    s, hd, chains = r.shape
    blk = pl.BlockSpec((tc, hd, chains), lambda i: (i, 0, 0))
    return pl.pallas_call(
        _rwkv_scan_kernel,
        grid=(s // tc,),
        in_specs=[blk] * 6,
        out_specs=blk,
        out_shape=jax.ShapeDtypeStruct((s, hd, chains), f32),
        scratch_shapes=[pltpu.VMEM((hd, hd, chains), f32)],
        compiler_params=pltpu.CompilerParams(
            dimension_semantics=("arbitrary",), vmem_limit_bytes=VMEM_LIMIT),
        name="rwkv_scan",
    )(r, w, k, v, a, b)


def _rwkv_out_kernel(x_ref, o_ref, r_ref, k_ref, v_ref, g_ref, vec_ref, ones_ref, wo_ref, lng_ref, lnb_ref,
                     out_ref):
    lnx_g, lnx_b, r_k = (vec_ref[n:n + 1, :] for n in range(3))
    bonus = _head_sums(r_ref[...] * k_ref[...] * r_k, ones_ref) * v_ref[...]
    y = (o_ref[...] * lnx_g + lnx_b + bonus) * g_ref[...]
    m = jnp.dot(y.astype(bf16), wo_ref[...], preferred_element_type=f32)
    out_ref[...] = _layer_norm_rows(ALPHA * x_ref[...] + m, lng_ref[...], lnb_ref[...])


def rwkv_out(x, o, r, k, v, g, vecs, ones, w_o, ln_g, ln_b, *, ts):
    n, d = x.shape
    full = lambda arr: pl.BlockSpec(arr.shape, lambda i: (0,) * arr.ndim)
    tok = pl.BlockSpec((ts, d), lambda i: (i, 0))
    consts = (vecs, ones, w_o, ln_g.reshape(1, d), ln_b.reshape(1, d))
    return pl.pallas_call(
        _rwkv_out_kernel,
        grid=(n // ts,),
        in_specs=[tok] * 6 + [full(c) for c in consts],
        out_specs=tok,
        out_shape=jax.ShapeDtypeStruct((n, d), f32),
        compiler_params=pltpu.CompilerParams(
            dimension_semantics=("arbitrary",), vmem_limit_bytes=VMEM_LIMIT),
        name="rwkv_out",
    )(x, o, r, k, v, g, *consts)


def _head_ones(d, head):
    seg = jnp.arange(d) // head
    return (seg[:, None] == seg[None, :]).astype(bf16)


def rwkv_layer(x, seq, mix, w_r, w_k, w_v, w0, w1, w2, a0, a1, a2, g1, g2, k_k, k_a, r_k, lnx_g, lnx_b, w_o,
               ln_g, ln_b, *, ts=256, tc=16):
    n, d = x.shape
    bsz = n // seq
    nh, hd = RWKV_HEADS, RWKV_HEAD
    ones = _head_ones(d, hd)
    c = lambda w: w.astype(bf16)
    vecs_in = jnp.stack([w0, a0, k_k, k_a])
    r, w, k, v, a, b, g = rwkv_proj(x, seq, mix, c(w_r), c(w_k), c(w_v), c(w1), c(w2), c(a1), c(a2),
                                    c(g1), c(g2), vecs_in, ones, ts=ts)
    to_chains = lambda t: t.reshape(bsz, seq, nh, hd).transpose(1, 3, 0, 2).reshape(seq, hd, bsz * nh)
    o = rwkv_scan(*(to_chains(t) for t in (r, w, k, v, a, b)), tc=tc)
    o = o.reshape(seq, hd, bsz, nh).transpose(2, 0, 3, 1).reshape(n, d)
    vecs_out = jnp.stack([lnx_g, lnx_b, r_k.reshape(d)])
    return rwkv_out(x, o, r, k, v, g, vecs_out, ones, c(w_o), ln_g, ln_b, ts=ts)


def kernel(x, rg_w_in, rg_conv_w, rg_conv_b, rg_w_a, rg_b_a, rg_w_x, rg_b_x, rg_lambda, rg_w_out, rw_mix, rw_w_r, rw_w_k, rw_w_v, rw_w0, rw_w1, rw_w2, rw_a0, rw_a1, rw_a2, rw_g1, rw_g2, rw_k_k, rw_k_a, rw_r_k, rw_lnx_g, rw_lnx_b, rw_w_o, peer_w_q, peer_sub_keys, peer_u, peer_v, ln_g, ln_b):
    bsz, s, d = x.shape
    x = x.reshape(bsz * s, d)
    for i in range(DEPTH):
        j = i // 2
        if i % 2 == 0:
            x1 = rglru_layer(x, s, rg_w_in[j], rg_conv_w[j], rg_conv_b[j], rg_w_a[j], rg_b_a[j],
                             rg_w_x[j], rg_b_x[j], rg_lambda[j], rg_w_out[j], ln_g[i, 0], ln_b[i, 0])
        else:
            x1 = rwkv_layer(x, s, rw_mix[j], rw_w_r[j], rw_w_k[j], rw_w_v[j], rw_w0[j], rw_w1[j],
                            rw_w2[j], rw_a0[j], rw_a1[j], rw_a2[j], rw_g1[j], rw_g2[j],
                            rw_k_k[j], rw_k_a[j], rw_r_k[j], rw_lnx_g[j], rw_lnx_b[j], rw_w_o[j],
                            ln_g[i, 0], ln_b[i, 0])
        x = peer_layer(x1, peer_w_q[i], peer_sub_keys[i], peer_u[i], peer_v[i], ln_g[i, 1], ln_b[i, 1])
    return x.reshape(bsz, s, d)
```

```python
import functools
import math
import jax, jax.numpy as jnp
from jax import lax
from jax.experimental import pallas as pl
from jax.experimental.pallas import tpu as pltpu

D_MODEL = 1024
DEPTH = 2
RG_WIDTH = 1408
RG_HEADS = 16
RG_BLOCK = 88
RG_CONV = 4
RG_C = 8.0
RWKV_HEAD = 64
RWKV_HEADS = 16
RWKV_GN_EPS = 64e-5
RWKV_L2_EPS = 1e-12
PEER_HEADS = 8
PEER_NKEYS = 128
PEER_DHALF = 128
PEER_TOPK = 16
PEER_BLOCK = 128
ALPHA = (2 * DEPTH) ** 0.25
LN_EPS = 1e-5

LANES = 128
SUBLANES = 8
VMEM_LIMIT = 56 << 20

f32 = jnp.float32
bf16 = jnp.bfloat16
u32 = jnp.uint32


def _gelu_tanh(x):
    return 0.5 * x * (1.0 + jnp.tanh(math.sqrt(2.0 / math.pi) * (x + 0.044715 * (x * x * x))))


def _layer_norm_rows(z, g, b):
    mu = jnp.mean(z, axis=-1, keepdims=True)
    zc = z - mu
    var = jnp.mean(zc * zc, axis=-1, keepdims=True)
    return zc * lax.rsqrt(var + LN_EPS) * g + b


def _top_rows(s, k, with_rank=False):
    rows = []
    rank = jnp.full(s.shape, float(k), f32)
    for j in range(k):
        m = jnp.max(s, axis=0, keepdims=True)
        rows.append(m)
        hit = s == m
        if with_rank:
            rank = jnp.where(hit, float(j), rank)
        s = jnp.where(hit, -jnp.inf, s)
    return (rows, rank) if with_rank else rows


def _peer_select_kernel(x_ref, wq_ref, keys_ref, cnt_ref, rk1_ref, p0_ref, p1_ref, q_ref):
    h = pl.program_id(1)

    @pl.when(h == 0)
    def _():
        q_ref[...] = jnp.dot(x_ref[...].astype(bf16), wq_ref[...], preferred_element_type=f32)

    def scores(p):
        off = pl.multiple_of((2 * h + p) * PEER_DHALF, PEER_DHALF)
        qhp = q_ref[:, pl.ds(off, PEER_DHALF)].astype(bf16)
        return lax.dot_general(keys_ref[0, p], qhp, (((1,), (1,)), ((), ())),
                               preferred_element_type=f32)

    s0 = scores(0)
    s1 = scores(1)
    a = _top_rows(s0, PEER_TOPK)
    b, rank1 = _top_rows(s1, PEER_TOPK, with_rank=True)
    row = lax.broadcasted_iota(jnp.int32, (PEER_TOPK, s0.shape[1]), 0)
    bmat = jnp.zeros((PEER_TOPK, s0.shape[1]), f32)
    for j in range(PEER_TOPK):
        bmat = jnp.where(row == j, b[j], bmat)
    cand = jnp.concatenate([a[0] + bmat] + [a[i] + bmat[:8] for i in range(1, PEER_TOPK)], axis=0)
    tau = _top_rows(cand, PEER_TOPK)[-1]
    cmax = a[0] + b[0]
    z = jnp.sum(jnp.where(cand >= tau, jnp.exp(cand - cmax), 0.0), axis=0, keepdims=True)
    cnt = jnp.zeros_like(s0)
    for j in range(PEER_TOPK):
        cnt = cnt + jnp.where(s0 + b[j] >= tau, 1.0, 0.0)
    cnt = jnp.where(s0 >= a[-1], cnt, 0.0)
    p0 = jnp.exp(s0 - a[0]) / z
    p1 = jnp.exp(s1 - b[0])
    for ref, val in ((cnt_ref, _bf16_twice(cnt)), (p0_ref, _bf16_twice(p0)),
                     (rk1_ref, pltpu.bitcast(rank1.astype(bf16), u32)),
                     (p1_ref, pltpu.bitcast(p1.astype(bf16), u32))):
        for c in range(val.shape[1] // LANES):
            ref[0, c] = val[:, c * LANES:(c + 1) * LANES]


def _bf16_twice(t):
    bits = pltpu.bitcast(t.astype(bf16).astype(f32), u32) >> 16
    return bits | (bits << 16)


def peer_select(x, wq_b, keys_b, *, tq):
    n, d = x.shape
    nh = PEER_HEADS
    full_spec = pl.BlockSpec((1, tq // LANES, PEER_NKEYS, LANES), lambda i, h: (h, i, 0, 0))
    half_spec = pl.BlockSpec((1, tq // LANES, PEER_NKEYS // 2, LANES), lambda i, h: (h, i, 0, 0))
    full = jax.ShapeDtypeStruct((nh, n // LANES, PEER_NKEYS, LANES), u32)
    half = jax.ShapeDtypeStruct((nh, n // LANES, PEER_NKEYS // 2, LANES), u32)
    return pl.pallas_call(
        _peer_select_kernel,
        grid=(n // tq, nh),
        in_specs=[pl.BlockSpec((tq, d), lambda i, h: (i, 0)),
                  pl.BlockSpec((d, 2 * nh * PEER_DHALF), lambda i, h: (0, 0)),
                  pl.BlockSpec((1, 2, PEER_NKEYS, PEER_DHALF), lambda i, h: (h, 0, 0, 0))],
        out_specs=[full_spec, half_spec, full_spec, half_spec],
        out_shape=[full, half, full, half],
        scratch_shapes=[pltpu.VMEM((tq, 2 * nh * PEER_DHALF), f32)],
        compiler_params=pltpu.CompilerParams(
            dimension_semantics=("arbitrary", "arbitrary"), vmem_limit_bytes=VMEM_LIMIT),
        name="peer_select",
    )(x, wq_b, keys_b)


def _peer_main_kernel(xb_ref, x_ref, cnt_ref, rk1_ref, p0_ref, p1_ref, u_ref, vt_ref,
                      g_ref, b_ref, o_ref, acc_ref, at_ref, ga_ref, *, te, tt):
    j = pl.program_id(1)
    n_i0 = te // PEER_NKEYS
    n_c = tt // LANES

    @pl.when(j == 0)
    def _():
        acc_ref[...] = jnp.zeros_like(acc_ref)

    at = lax.dot_general(u_ref[...], xb_ref[...], (((1,), (1,)), ((), ())),
                         preferred_element_type=f32)
    for c in range(n_c):
        at_ref[c] = at[:, c * LANES:(c + 1) * LANES]

    def body(idx, carry):
        grp = idx // n_c
        c = idx % n_c
        i0s = pl.ds(pl.multiple_of(j * n_i0 + grp * SUBLANES, SUBLANES), SUBLANES)
        half = PEER_NKEYS // 2
        unpack = lambda words: pltpu.bitcast(words, bf16)
        for r in range(SUBLANES):
            g = jnp.zeros((PEER_NKEYS, LANES), bf16)
            for h in range(PEER_HEADS):
                cntrow = jnp.broadcast_to(cnt_ref[h, c, i0s][r:r + 1], (half, LANES))
                p0row = jnp.broadcast_to(p0_ref[h, c, i0s][r:r + 1], (half, LANES))
                sel = unpack(rk1_ref[h, c]) < unpack(cntrow)
                g = g + jnp.where(sel, unpack(p1_ref[h, c]) * unpack(p0row), jnp.zeros((), bf16))
            il = grp * SUBLANES + r
            rows = pl.ds(pl.multiple_of(il * PEER_NKEYS, PEER_NKEYS), PEER_NKEYS)
            ga = _gelu_tanh(at_ref[c, rows]).astype(bf16) * g
            ga_ref[c, pl.ds(pl.multiple_of(il * half, half), half)] = pltpu.bitcast(ga, u32)
        return carry

    lax.fori_loop(0, (n_i0 // SUBLANES) * n_c, body, 0)
    ga_all = jnp.concatenate([pltpu.bitcast(ga_ref[c], bf16) for c in range(n_c)], axis=1)
    acc_ref[...] += jnp.dot(vt_ref[...], ga_all, preferred_element_type=f32)

    @pl.when(j == pl.num_programs(1) - 1)
    def _():
        z = ALPHA * x_ref[...] + acc_ref[...].T
        o_ref[...] = _layer_norm_rows(z, g_ref[...], b_ref[...])


def peer_main(x, xb, stats, u_b, vt_b, ln_g, ln_b, *, tt, te):
    n, d = x.shape
    e = u_b.shape[0]
    nh = PEER_HEADS
    cnt, rk1, p0, p1 = stats
    full_spec = pl.BlockSpec((nh, tt // LANES, PEER_NKEYS, LANES), lambda i, j: (0, i, 0, 0))
    half_spec = pl.BlockSpec((nh, tt // LANES, PEER_NKEYS // 2, LANES), lambda i, j: (0, i, 0, 0))
    return pl.pallas_call(
        functools.partial(_peer_main_kernel, te=te, tt=tt),
        grid=(n // tt, e // te),
        in_specs=[pl.BlockSpec((tt, d), lambda i, j: (i, 0)),
                  pl.BlockSpec((tt, d), lambda i, j: (i, 0)),
                  full_spec, half_spec, full_spec, half_spec,
                  pl.BlockSpec((te, d), lambda i, j: (j, 0)),
                  pl.BlockSpec((d, te), lambda i, j: (0, j)),
                  pl.BlockSpec((1, d), lambda i, j: (0, 0)),
                  pl.BlockSpec((1, d), lambda i, j: (0, 0))],
        out_specs=pl.BlockSpec((tt, d), lambda i, j: (i, 0)),
        out_shape=jax.ShapeDtypeStruct((n, d), f32),
        scratch_shapes=[pltpu.VMEM((d, tt), f32), pltpu.VMEM((tt // LANES, te, LANES), f32),
                        pltpu.VMEM((tt // LANES, te // 2, LANES), u32)],
        compiler_params=pltpu.CompilerParams(
            dimension_semantics=("arbitrary", "arbitrary"), vmem_limit_bytes=VMEM_LIMIT),
        name="peer_main",
    )(xb, x, cnt, rk1, p0, p1, u_b, vt_b, ln_g.reshape(1, d), ln_b.reshape(1, d))


def peer_layer(x, w_q, sub_keys, u, v, ln_g, ln_b, *, tq=256, tt=512, te=1024):
    stats = peer_select(x, w_q.astype(bf16), sub_keys.astype(bf16), tq=tq)
    return peer_main(x, x.astype(bf16), stats, u.astype(bf16), v.T.astype(bf16), ln_g, ln_b, tt=tt, te=te)


def _rglru_kernel(x_ref, win_ref, cw_ref, vec_ref, wg_ref, wout_ref, lng_ref, lnb_ref, o_ref,
                  tail_ref, h_ref, *, tiles_per_seq):
    i = pl.program_id(0)
    w = RG_WIDTH
    x = x_ref[...]
    ts = x.shape[0]

    @pl.when(i % tiles_per_seq == 0)
    def _():
        tail_ref[...] = jnp.zeros_like(tail_ref)
        h_ref[...] = jnp.zeros_like(h_ref)

    conv_b, b_a, b_x, lam = (vec_ref[n:n + 1, :] for n in range(4))
    hin = jnp.dot(x.astype(bf16), win_ref[...], preferred_element_type=f32)
    gate_branch = _gelu_tanh(hin[:, :w])
    hx = hin[:, w:]

    row = lax.broadcasted_iota(jnp.int32, (ts, w), 0)
    row8 = lax.broadcasted_iota(jnp.int32, (SUBLANES, w), 0)
    tail = tail_ref[...]
    xc = hx * cw_ref[RG_CONV - 1:RG_CONV, :] + conv_b
    for s in range(1, RG_CONV):
        rolled = pltpu.roll(hx, s, axis=0)
        head = jnp.where(row8 < s, pltpu.roll(tail, s, axis=0), rolled[:SUBLANES])
        shifted = jnp.concatenate([head, rolled[SUBLANES:]], axis=0)
        xc = xc + shifted * cw_ref[RG_CONV - 1 - s:RG_CONV - s, :]
    tail_ref[...] = hx[ts - SUBLANES:]

    gates = jnp.dot(xc.astype(bf16), wg_ref[...], preferred_element_type=f32)
    r = _sigmoid(gates[:, :w] + b_a)
    ig = _sigmoid(gates[:, w:] + b_x)
    log_a = (-RG_C * _softplus(-lam)) * r
    a = jnp.exp(log_a)
    u = jnp.sqrt(-jnp.tanh(log_a) * (a * a + 1.0)) * (ig * xc)

    s = 1
    while s < ts:
        keep = row >= s
        a_sh = jnp.where(keep, pltpu.roll(a, s, axis=0), 1.0)
        u_sh = jnp.where(keep, pltpu.roll(u, s, axis=0), 0.0)
        u = a * u_sh + u
        a = a * a_sh
        s *= 2
    hs = u + a * h_ref[SUBLANES - 1:SUBLANES, :]
    h_ref[...] = hs[ts - SUBLANES:]

    m = jnp.dot((hs * gate_branch).astype(bf16), wout_ref[...], preferred_element_type=f32)
    o_ref[...] = _layer_norm_rows(ALPHA * x + m, lng_ref[...], lnb_ref[...])


def rglru_layer(x, seq, w_in, conv_w, conv_b, w_a, b_a, w_x, b_x, lam, w_out, ln_g, ln_b, *, ts=256):
    n, d = x.shape
    w = RG_WIDTH
    blockdiag = lambda t: jax.scipy.linalg.block_diag(*t)
    w_gates = jnp.concatenate([blockdiag(w_a), blockdiag(w_x)], axis=1).astype(bf16)
    vecs = jnp.stack([conv_b, b_a.reshape(w), b_x.reshape(w), lam.reshape(w)])
    consts = (w_in.astype(bf16), conv_w, vecs, w_gates, w_out.astype(bf16), ln_g.reshape(1, d), ln_b.reshape(1, d))
    full = lambda arr: pl.BlockSpec(arr.shape, lambda i: (0,) * arr.ndim, pipeline_mode=pl.Buffered(1))
    tok = pl.BlockSpec((ts, d), lambda i: (i, 0))
    return pl.pallas_call(
        functools.partial(_rglru_kernel, tiles_per_seq=seq // ts),
        grid=(n // ts,),
        in_specs=[tok] + [full(c) for c in consts],
        out_specs=tok,
        out_shape=jax.ShapeDtypeStruct((n, d), f32),
        scratch_shapes=[pltpu.VMEM((SUBLANES, w), f32), pltpu.VMEM((SUBLANES, w), f32)],
        compiler_params=pltpu.CompilerParams(
            dimension_semantics=("arbitrary",), vmem_limit_bytes=VMEM_LIMIT),
        name="rglru",
    )(x, *consts)


def _sigmoid(x):
    return 1.0 / (1.0 + jnp.exp(-x))


def _softplus(x):
    return jnp.maximum(x, 0.0) + jnp.log1p(jnp.exp(-jnp.abs(x)))


def _head_sums(t, ones_ref):
    hi = t.astype(bf16)
    lo = (t - hi.astype(f32)).astype(bf16)
    return (jnp.dot(hi, ones_ref[...], preferred_element_type=f32)
            + jnp.dot(lo, ones_ref[...], preferred_element_type=f32))


def _rwkv_proj_kernel(x_ref, xp_ref, mix_ref, wr_ref, wk_ref, wv_ref, w1_ref, w2_ref, a1_ref, a2_ref,
                      g1_ref, g2_ref, vec_ref, ones_ref,
                      r_ref, w_ref, k_ref, v_ref, a_ref, b_ref, g_ref, *, tiles_per_seq):
    i = pl.program_id(0)
    x = x_ref[...]
    ts = x.shape[0]
    prev_last = jnp.where(i % tiles_per_seq == 0, 0.0, xp_ref[SUBLANES - 1:SUBLANES, :])
    row = lax.broadcasted_iota(jnp.int32, x.shape, 0)
    xprev = jnp.where(row == 0, prev_last, pltpu.roll(x, 1, axis=0))
    xx = xprev - x

    def mixed(m):
        return (x + xx * mix_ref[m:m + 1, :]).astype(bf16)

    def mm(a, w_ref_):
        return jnp.dot(a, w_ref_[...], preferred_element_type=f32)

    w0, a0, k_k, k_a = (vec_ref[n:n + 1, :] for n in range(4))
    r = mm(mixed(0), wr_ref)
    k = mm(mixed(2), wk_ref)
    v = mm(mixed(3), wv_ref)
    lw = mm(jnp.tanh(mm(mixed(1), w1_ref)).astype(bf16), w2_ref)
    w_log = -_softplus(-(w0 + lw)) - 0.5
    gate_a = _sigmoid(a0 + mm(mm(mixed(4), a1_ref).astype(bf16), a2_ref))
    g = mm(_sigmoid(mm(mixed(5), g1_ref)).astype(bf16), g2_ref)
    kk = k * k_k
    kk = kk / jnp.maximum(jnp.sqrt(_head_sums(kk * kk, ones_ref)), RWKV_L2_EPS)
    r_ref[...] = r
    w_ref[...] = jnp.exp(-jnp.exp(w_log))
    k_ref[...] = k * (1.0 + (gate_a - 1.0) * k_a)
    v_ref[...] = v
    a_ref[...] = -kk
    b_ref[...] = kk * gate_a
    g_ref[...] = g


def rwkv_proj(x, seq, mix, w_r, w_k, w_v, w1, w2, a1, a2, g1, g2, vecs, ones, *, ts):
    n, d = x.shape
    full = lambda arr: pl.BlockSpec(arr.shape, lambda i: (0,) * arr.ndim)
    tok = pl.BlockSpec((ts, d), lambda i: (i, 0))
    prev = pl.BlockSpec((SUBLANES, d), lambda i: (jnp.maximum(i * (ts // SUBLANES) - 1, 0), 0))
    weights = (mix, w_r, w_k, w_v, w1, w2, a1, a2, g1, g2, vecs, ones)
    out = jax.ShapeDtypeStruct((n, d), f32)
    return pl.pallas_call(
        functools.partial(_rwkv_proj_kernel, tiles_per_seq=seq // ts),
        grid=(n // ts,),
        in_specs=[tok, prev] + [full(w) for w in weights],
        out_specs=[tok] * 7,
        out_shape=[out] * 7,
        compiler_params=pltpu.CompilerParams(
            dimension_semantics=("arbitrary",), vmem_limit_bytes=VMEM_LIMIT),
        name="rwkv_proj",
    )(x, x, *weights)


def _rwkv_scan_kernel(r_ref, w_ref, k_ref, v_ref, a_ref, b_ref, o_ref, s_ref):
    nk = s_ref.shape[0]

    @pl.when(pl.program_id(0) == 0)
    def _():
        s_ref[...] = jnp.zeros_like(s_ref)

    def step(t, carry):
        sa0 = jnp.zeros(s_ref.shape[1:], f32)
        sa1 = jnp.zeros(s_ref.shape[1:], f32)
        for kk in range(0, nk, 2):
            sa0 = sa0 + s_ref[kk] * a_ref[t, kk:kk + 1, :]
            sa1 = sa1 + s_ref[kk + 1] * a_ref[t, kk + 1:kk + 2, :]
        sa = sa0 + sa1
        vt = v_ref[t]
        o0 = jnp.zeros(s_ref.shape[1:], f32)
        o1 = jnp.zeros(s_ref.shape[1:], f32)
        for kk in range(nk):
            new = (s_ref[kk] * w_ref[t, kk:kk + 1, :]
                   + (sa * b_ref[t, kk:kk + 1, :] + vt * k_ref[t, kk:kk + 1, :]))
            s_ref[kk] = new
            if kk % 2 == 0:
                o0 = o0 + new * r_ref[t, kk:kk + 1, :]
            else:
                o1 = o1 + new * r_ref[t, kk:kk + 1, :]
        o = o0 + o1
        mu = jnp.mean(o, axis=0, keepdims=True)
        oc = o - mu
        var = jnp.mean(oc * oc, axis=0, keepdims=True)
        o_ref[t] = oc * lax.rsqrt(var + RWKV_GN_EPS)
        return carry

    lax.fori_loop(0, r_ref.shape[0], step, 0)


def rwkv_scan(r, w, k, v, a, b, *, tc):
    s, hd, chains = r.shape
    blk = pl.BlockSpec((tc, hd, chains), lambda i: (i, 0, 0))
    return pl.pallas_call(
        _rwkv_scan_kernel,
        grid=(s // tc,),
        in_specs=[blk] * 6,
        out_specs=blk,
        out_shape=jax.ShapeDtypeStruct((s, hd, chains), f32),
        scratch_shapes=[pltpu.VMEM((hd, hd, chains), f32)],
        compiler_params=pltpu.CompilerParams(
            dimension_semantics=("arbitrary",), vmem_limit_bytes=VMEM_LIMIT),
        name="rwkv_scan",
    )(r, w, k, v, a, b)


def _rwkv_out_kernel(x_ref, o_ref, r_ref, k_ref, v_ref, g_ref, vec_ref, ones_ref, wo_ref, lng_ref, lnb_ref,
                     out_ref):
    lnx_g, lnx_b, r_k = (vec_ref[n:n + 1, :] for n in range(3))
    bonus = _head_sums(r_ref[...] * k_ref[...] * r_k, ones_ref) * v_ref[...]
    y = (o_ref[...] * lnx_g + lnx_b + bonus) * g_ref[...]
    m = jnp.dot(y.astype(bf16), wo_ref[...], preferred_element_type=f32)
    out_ref[...] = _layer_norm_rows(ALPHA * x_ref[...] + m, lng_ref[...], lnb_ref[...])


def rwkv_out(x, o, r, k, v, g, vecs, ones, w_o, ln_g, ln_b, *, ts):
    n, d = x.shape
    full = lambda arr: pl.BlockSpec(arr.shape, lambda i: (0,) * arr.ndim)
    tok = pl.BlockSpec((ts, d), lambda i: (i, 0))
    consts = (vecs, ones, w_o, ln_g.reshape(1, d), ln_b.reshape(1, d))
    return pl.pallas_call(
        _rwkv_out_kernel,
        grid=(n // ts,),
        in_specs=[tok] * 6 + [full(c) for c in consts],
        out_specs=tok,
        out_shape=jax.ShapeDtypeStruct((n, d), f32),
        compiler_params=pltpu.CompilerParams(
            dimension_semantics=("arbitrary",), vmem_limit_bytes=VMEM_LIMIT),
        name="rwkv_out",
    )(x, o, r, k, v, g, *consts)


def _head_ones(d, head):
    seg = jnp.arange(d) // head
    return (seg[:, None] == seg[None, :]).astype(bf16)


def rwkv_layer(x, seq, mix, w_r, w_k, w_v, w0, w1, w2, a0, a1, a2, g1, g2, k_k, k_a, r_k, lnx_g, lnx_b, w_o,
               ln_g, ln_b, *, ts=256, tc=16):
    n, d = x.shape
    bsz = n // seq
    nh, hd = RWKV_HEADS, RWKV_HEAD
    ones = _head_ones(d, hd)
    c = lambda w: w.astype(bf16)
    vecs_in = jnp.stack([w0, a0, k_k, k_a])
    r, w, k, v, a, b, g = rwkv_proj(x, seq, mix, c(w_r), c(w_k), c(w_v), c(w1), c(w2), c(a1), c(a2),
                                    c(g1), c(g2), vecs_in, ones, ts=ts)
    to_chains = lambda t: t.reshape(bsz, seq, nh, hd).transpose(1, 3, 0, 2).reshape(seq, hd, bsz * nh)
    o = rwkv_scan(*(to_chains(t) for t in (r, w, k, v, a, b)), tc=tc)
    o = o.reshape(seq, hd, bsz, nh).transpose(2, 0, 3, 1).reshape(n, d)
    vecs_out = jnp.stack([lnx_g, lnx_b, r_k.reshape(d)])
    return rwkv_out(x, o, r, k, v, g, vecs_out, ones, c(w_o), ln_g, ln_b, ts=ts)


def kernel(x, rg_w_in, rg_conv_w, rg_conv_b, rg_w_a, rg_b_a, rg_w_x, rg_b_x, rg_lambda, rg_w_out, rw_mix, rw_w_r, rw_w_k, rw_w_v, rw_w0, rw_w1, rw_w2, rw_a0, rw_a1, rw_a2, rw_g1, rw_g2, rw_k_k, rw_k_a, rw_r_k, rw_lnx_g, rw_lnx_b, rw_w_o, peer_w_q, peer_sub_keys, peer_u, peer_v, ln_g, ln_b):
    bsz, s, d = x.shape
    x = x.reshape(bsz * s, d)
    for i in range(DEPTH):
        j = i // 2
        if i % 2 == 0:
            x1 = rglru_layer(x, s, rg_w_in[j], rg_conv_w[j], rg_conv_b[j], rg_w_a[j], rg_b_a[j],
                             rg_w_x[j], rg_b_x[j], rg_lambda[j], rg_w_out[j], ln_g[i, 0], ln_b[i, 0])
        else:
            x1 = rwkv_layer(x, s, rw_mix[j], rw_w_r[j], rw_w_k[j], rw_w_v[j], rw_w0[j], rw_w1[j],
                            rw_w2[j], rw_a0[j], rw_a1[j], rw_a2[j], rw_g1[j], rw_g2[j],
                            rw_k_k[j], rw_k_a[j], rw_r_k[j], rw_lnx_g[j], rw_lnx_b[j], rw_w_o[j],
                            ln_g[i, 0], ln_b[i, 0])
        x = peer_layer(x1, peer_w_q[i], peer_sub_keys[i], peer_u[i], peer_v[i], ln_g[i, 1], ln_b[i, 1])
    return x.reshape(bsz, s, d)
```

```python
import functools
import math
import jax, jax.numpy as jnp
from jax import lax
from jax.experimental import pallas as pl
from jax.experimental.pallas import tpu as pltpu

D_MODEL = 1024
DEPTH = 2
RG_WIDTH = 1408
RG_HEADS = 16
RG_BLOCK = 88
RG_CONV = 4
RG_C = 8.0
RWKV_HEAD = 64
RWKV_HEADS = 16
RWKV_GN_EPS = 64e-5
RWKV_L2_EPS = 1e-12
PEER_HEADS = 8
PEER_NKEYS = 128
PEER_DHALF = 128
PEER_TOPK = 16
PEER_BLOCK = 128
ALPHA = (2 * DEPTH) ** 0.25
LN_EPS = 1e-5

LANES = 128
SUBLANES = 8
MXU_COLS = 256
VMEM_LIMIT = 56 << 20

f32 = jnp.float32
bf16 = jnp.bfloat16
u32 = jnp.uint32


def _gelu_tanh(x):
    return 0.5 * x * (1.0 + jnp.tanh(math.sqrt(2.0 / math.pi) * (x + 0.044715 * (x * x * x))))


def _layer_norm_rows(z, g, b):
    mu = jnp.mean(z, axis=-1, keepdims=True)
    zc = z - mu
    var = jnp.mean(zc * zc, axis=-1, keepdims=True)
    return zc * lax.rsqrt(var + LN_EPS) * g + b


def _top_rows(s, k, with_rank=False):
    rows = []
    rank = jnp.full(s.shape, float(k), f32)
    for j in range(k):
        m = jnp.max(s, axis=0, keepdims=True)
        rows.append(m)
        hit = s == m
        if with_rank:
            rank = jnp.where(hit, float(j), rank)
        s = jnp.where(hit, -jnp.inf, s)
    return (rows, rank) if with_rank else rows


def _peer_select_kernel(x_ref, wq_ref, keys_ref, cnt_ref, rk1_ref, p0_ref, p1_ref, q_ref):
    h = pl.program_id(1)

    @pl.when(h == 0)
    def _():
        q_ref[...] = jnp.dot(x_ref[...].astype(bf16), wq_ref[...], preferred_element_type=f32)

    def scores(p):
        off = pl.multiple_of((2 * h + p) * PEER_DHALF, PEER_DHALF)
        qhp = q_ref[:, pl.ds(off, PEER_DHALF)].astype(bf16)
        return lax.dot_general(keys_ref[0, p], qhp, (((1,), (1,)), ((), ())),
                               preferred_element_type=f32)

    s0 = scores(0)
    s1 = scores(1)
    a = _top_rows(s0, PEER_TOPK)
    b, rank1 = _top_rows(s1, PEER_TOPK, with_rank=True)
    row = lax.broadcasted_iota(jnp.int32, (PEER_TOPK, s0.shape[1]), 0)
    bmat = jnp.zeros((PEER_TOPK, s0.shape[1]), f32)
    for j in range(PEER_TOPK):
        bmat = jnp.where(row == j, b[j], bmat)
    cand = jnp.concatenate([a[0] + bmat] + [a[i] + bmat[:8] for i in range(1, PEER_TOPK)], axis=0)
    tau = _top_rows(cand, PEER_TOPK)[-1]
    cmax = a[0] + b[0]
    z = jnp.sum(jnp.where(cand >= tau, jnp.exp(cand - cmax), 0.0), axis=0, keepdims=True)
    cnt = jnp.zeros_like(s0)
    for j in range(PEER_TOPK):
        cnt = cnt + jnp.where(s0 + b[j] >= tau, 1.0, 0.0)
    cnt = jnp.where(s0 >= a[-1], cnt, 0.0)
    p0 = jnp.exp(s0 - a[0]) / z
    p1 = jnp.exp(s1 - b[0])
    for ref, val in ((cnt_ref, _bf16_twice(cnt)), (p0_ref, _bf16_twice(p0)),
                     (rk1_ref, pltpu.bitcast(rank1.astype(bf16), u32)),
                     (p1_ref, pltpu.bitcast(p1.astype(bf16), u32))):
        for c in range(val.shape[1] // LANES):
            ref[0, c] = val[:, c * LANES:(c + 1) * LANES]


def _bf16_twice(t):
    bits = pltpu.bitcast(t.astype(bf16).astype(f32), u32) >> 16
    return bits | (bits << 16)


def peer_select(x, wq_b, keys_b, *, tq):
    n, d = x.shape
    nh = PEER_HEADS
    full_spec = pl.BlockSpec((1, tq // LANES, PEER_NKEYS, LANES), lambda i, h: (h, i, 0, 0))
    half_spec = pl.BlockSpec((1, tq // LANES, PEER_NKEYS // 2, LANES), lambda i, h: (h, i, 0, 0))
    full = jax.ShapeDtypeStruct((nh, n // LANES, PEER_NKEYS, LANES), u32)
    half = jax.ShapeDtypeStruct((nh, n // LANES, PEER_NKEYS // 2, LANES), u32)
    return pl.pallas_call(
        _peer_select_kernel,
        grid=(n // tq, nh),
        in_specs=[pl.BlockSpec((tq, d), lambda i, h: (i, 0)),
                  pl.BlockSpec((d, 2 * nh * PEER_DHALF), lambda i, h: (0, 0)),
                  pl.BlockSpec((1, 2, PEER_NKEYS, PEER_DHALF), lambda i, h: (h, 0, 0, 0))],
        out_specs=[full_spec, half_spec, full_spec, half_spec],
        out_shape=[full, half, full, half],
        scratch_shapes=[pltpu.VMEM((tq, 2 * nh * PEER_DHALF), f32)],
        compiler_params=pltpu.CompilerParams(
            dimension_semantics=("arbitrary", "arbitrary"), vmem_limit_bytes=VMEM_LIMIT),
        name="peer_select",
    )(x, wq_b, keys_b)


def _peer_main_kernel(xb_ref, x_ref, cnt_ref, rk1_ref, p0_ref, p1_ref, u_ref, vt_ref,
                      g_ref, b_ref, o_ref, acc_ref, *slab_refs, te, tt):
    j = pl.program_id(1)
    n_i0 = te // PEER_NKEYS
    n_slab = tt // LANES
    per_chunk = MXU_COLS // LANES
    at_refs, ga_refs = slab_refs[:n_slab], slab_refs[n_slab:]
    half = PEER_NKEYS // 2
    unpack = lambda words: pltpu.bitcast(words, bf16)

    @pl.when(j == 0)
    def _():
        acc_ref[...] = jnp.zeros_like(acc_ref)

    def act_chunk(c):
        at = lax.dot_general(u_ref[...], xb_ref[c * MXU_COLS:(c + 1) * MXU_COLS, :],
                             (((1,), (1,)), ((), ())), preferred_element_type=f32)
        for k in range(per_chunk):
            at_refs[c * per_chunk + k][...] = at[:, k * LANES:(k + 1) * LANES]

    def gate_slab(s):
        for il in range(n_i0):
            g = jnp.zeros((PEER_NKEYS, LANES), bf16)
            for h in range(PEER_HEADS):
                cntrow = jnp.broadcast_to(cnt_ref[h, s, il:il + 1, :], (half, LANES))
                p0row = jnp.broadcast_to(p0_ref[h, s, il:il + 1, :], (half, LANES))
                sel = unpack(rk1_ref[h, s]) < unpack(cntrow)
                g = g + jnp.where(sel, unpack(p1_ref[h, s]) * unpack(p0row), jnp.zeros((), bf16))
            ga = _gelu_tanh(at_refs[s][il * PEER_NKEYS:(il + 1) * PEER_NKEYS, :]).astype(bf16) * g
            ga_refs[s][il * half:(il + 1) * half, :] = pltpu.bitcast(ga, u32)

    def out_chunk(c):
        ga = jnp.concatenate([unpack(ga_refs[c * per_chunk + k][...]) for k in range(per_chunk)], axis=1)
        lanes = slice(c * MXU_COLS, (c + 1) * MXU_COLS)
        acc_ref[:, lanes] += jnp.dot(vt_ref[...], ga, preferred_element_type=f32)

    n_chunk = tt // MXU_COLS
    act_chunk(0)
    for c in range(n_chunk):
        gate_slab(c * per_chunk)
        if c + 1 < n_chunk:
            act_chunk(c + 1)
        for k in range(1, per_chunk):
            gate_slab(c * per_chunk + k)
        out_chunk(c)

    @pl.when(j == pl.num_programs(1) - 1)
    def _():
        z = ALPHA * x_ref[...] + acc_ref[...].T
        o_ref[...] = _layer_norm_rows(z, g_ref[...], b_ref[...])


def peer_main(x, xb, stats, u_b, vt_b, ln_g, ln_b, *, tt, te):
    n, d = x.shape
    e = u_b.shape[0]
    nh = PEER_HEADS
    cnt, rk1, p0, p1 = stats
    full_spec = pl.BlockSpec((nh, tt // LANES, te // PEER_NKEYS, LANES), lambda i, j: (0, i, j, 0))
    half_spec = pl.BlockSpec((nh, tt // LANES, PEER_NKEYS // 2, LANES), lambda i, j: (0, i, 0, 0))
    return pl.pallas_call(
        functools.partial(_peer_main_kernel, te=te, tt=tt),
        grid=(n // tt, e // te),
        in_specs=[pl.BlockSpec((tt, d), lambda i, j: (i, 0)),
                  pl.BlockSpec((tt, d), lambda i, j: (i, 0)),
                  full_spec, half_spec, full_spec, half_spec,
                  pl.BlockSpec((te, d), lambda i, j: (j, 0)),
                  pl.BlockSpec((d, te), lambda i, j: (0, j)),
                  pl.BlockSpec((1, d), lambda i, j: (0, 0)),
                  pl.BlockSpec((1, d), lambda i, j: (0, 0))],
        out_specs=pl.BlockSpec((tt, d), lambda i, j: (i, 0)),
        out_shape=jax.ShapeDtypeStruct((n, d), f32),
        scratch_shapes=([pltpu.VMEM((d, tt), f32)]
                        + [pltpu.VMEM((te, LANES), f32)] * (tt // LANES)
                        + [pltpu.VMEM((te // 2, LANES), u32)] * (tt // LANES)),
        compiler_params=pltpu.CompilerParams(
            dimension_semantics=("arbitrary", "arbitrary"), vmem_limit_bytes=VMEM_LIMIT),
        name="peer_main",
    )(xb, x, cnt, rk1, p0, p1, u_b, vt_b, ln_g.reshape(1, d), ln_b.reshape(1, d))


def peer_layer(x, w_q, sub_keys, u, v, ln_g, ln_b, *, tq=256, tt=1024, te=1024):
    stats = peer_select(x, w_q.astype(bf16), sub_keys.astype(bf16), tq=tq)
    return peer_main(x, x.astype(bf16), stats, u.astype(bf16), v.T.astype(bf16), ln_g, ln_b, tt=tt, te=te)


def _rglru_kernel(x_ref, win_ref, cw_ref, vec_ref, wg_ref, wout_ref, lng_ref, lnb_ref, o_ref,
                  tail_ref, h_ref, *, tiles_per_seq):
    i = pl.program_id(0)
    w = RG_WIDTH
    x = x_ref[...]
    ts = x.shape[0]

    @pl.when(i % tiles_per_seq == 0)
    def _():
        tail_ref[...] = jnp.zeros_like(tail_ref)
        h_ref[...] = jnp.zeros_like(h_ref)

    conv_b, b_a, b_x, lam = (vec_ref[n:n + 1, :] for n in range(4))
    hin = jnp.dot(x.astype(bf16), win_ref[...], preferred_element_type=f32)
    gate_branch = _gelu_tanh(hin[:, :w])
    hx = hin[:, w:]

    row = lax.broadcasted_iota(jnp.int32, (ts, w), 0)
    row8 = lax.broadcasted_iota(jnp.int32, (SUBLANES, w), 0)
    tail = tail_ref[...]
    xc = hx * cw_ref[RG_CONV - 1:RG_CONV, :] + conv_b
    for s in range(1, RG_CONV):
        rolled = pltpu.roll(hx, s, axis=0)
        head = jnp.where(row8 < s, pltpu.roll(tail, s, axis=0), rolled[:SUBLANES])
        shifted = jnp.concatenate([head, rolled[SUBLANES:]], axis=0)
        xc = xc + shifted * cw_ref[RG_CONV - 1 - s:RG_CONV - s, :]
    tail_ref[...] = hx[ts - SUBLANES:]

    gates = jnp.dot(xc.astype(bf16), wg_ref[...], preferred_element_type=f32)
    r = _sigmoid(gates[:, :w] + b_a)
    ig = _sigmoid(gates[:, w:] + b_x)
    log_a = (-RG_C * _softplus(-lam)) * r
    a = jnp.exp(log_a)
    u = jnp.sqrt(-jnp.tanh(log_a) * (a * a + 1.0)) * (ig * xc)

    s = 1
    while s < ts:
        keep = row >= s
        a_sh = jnp.where(keep, pltpu.roll(a, s, axis=0), 1.0)
        u_sh = jnp.where(keep, pltpu.roll(u, s, axis=0), 0.0)
        u = a * u_sh + u
        a = a * a_sh
        s *= 2
    hs = u + a * h_ref[SUBLANES - 1:SUBLANES, :]
    h_ref[...] = hs[ts - SUBLANES:]

    m = jnp.dot((hs * gate_branch).astype(bf16), wout_ref[...], preferred_element_type=f32)
    o_ref[...] = _layer_norm_rows(ALPHA * x + m, lng_ref[...], lnb_ref[...])


def rglru_layer(x, seq, w_in, conv_w, conv_b, w_a, b_a, w_x, b_x, lam, w_out, ln_g, ln_b, *, ts=256):
    n, d = x.shape
    w = RG_WIDTH
    blockdiag = lambda t: jax.scipy.linalg.block_diag(*t)
    w_gates = jnp.concatenate([blockdiag(w_a), blockdiag(w_x)], axis=1).astype(bf16)
    vecs = jnp.stack([conv_b, b_a.reshape(w), b_x.reshape(w), lam.reshape(w)])
    consts = (w_in.astype(bf16), conv_w, vecs, w_gates, w_out.astype(bf16), ln_g.reshape(1, d), ln_b.reshape(1, d))
    full = lambda arr: pl.BlockSpec(arr.shape, lambda i: (0,) * arr.ndim, pipeline_mode=pl.Buffered(1))
    tok = pl.BlockSpec((ts, d), lambda i: (i, 0))
    return pl.pallas_call(
        functools.partial(_rglru_kernel, tiles_per_seq=seq // ts),
        grid=(n // ts,),
        in_specs=[tok] + [full(c) for c in consts],
        out_specs=tok,
        out_shape=jax.ShapeDtypeStruct((n, d), f32),
        scratch_shapes=[pltpu.VMEM((SUBLANES, w), f32), pltpu.VMEM((SUBLANES, w), f32)],
        compiler_params=pltpu.CompilerParams(
            dimension_semantics=("arbitrary",), vmem_limit_bytes=VMEM_LIMIT),
        name="rglru",
    )(x, *consts)


def _sigmoid(x):
    return 1.0 / (1.0 + jnp.exp(-x))


def _softplus(x):
    return jnp.maximum(x, 0.0) + jnp.log1p(jnp.exp(-jnp.abs(x)))


def _head_sums(t, ones_ref):
    hi = t.astype(bf16)
    lo = (t - hi.astype(f32)).astype(bf16)
    return (jnp.dot(hi, ones_ref[...], preferred_element_type=f32)
            + jnp.dot(lo, ones_ref[...], preferred_element_type=f32))


def _rwkv_proj_kernel(x_ref, xp_ref, mix_ref, wr_ref, wk_ref, wv_ref, w1_ref, w2_ref, a1_ref, a2_ref,
                      g1_ref, g2_ref, vec_ref, ones_ref,
                      r_ref, w_ref, k_ref, v_ref, a_ref, b_ref, g_ref, *, tiles_per_seq):
    i = pl.program_id(0)
    x = x_ref[...]
    ts = x.shape[0]
    prev_last = jnp.where(i % tiles_per_seq == 0, 0.0, xp_ref[SUBLANES - 1:SUBLANES, :])
    row = lax.broadcasted_iota(jnp.int32, x.shape, 0)
    xprev = jnp.where(row == 0, prev_last, pltpu.roll(x, 1, axis=0))
    xx = xprev - x

    def mixed(m):
        return (x + xx * mix_ref[m:m + 1, :]).astype(bf16)

    def mm(a, w_ref_):
        return jnp.dot(a, w_ref_[...], preferred_element_type=f32)

    w0, a0, k_k, k_a = (vec_ref[n:n + 1, :] for n in range(4))
    r = mm(mixed(0), wr_ref)
    k = mm(mixed(2), wk_ref)
    v = mm(mixed(3), wv_ref)
    lw = mm(jnp.tanh(mm(mixed(1), w1_ref)).astype(bf16), w2_ref)
    w_log = -_softplus(-(w0 + lw)) - 0.5
    gate_a = _sigmoid(a0 + mm(mm(mixed(4), a1_ref).astype(bf16), a2_ref))
    g = mm(_sigmoid(mm(mixed(5), g1_ref)).astype(bf16), g2_ref)
    kk = k * k_k
    kk = kk / jnp.maximum(jnp.sqrt(_head_sums(kk * kk, ones_ref)), RWKV_L2_EPS)
    r_ref[...] = r
    w_ref[...] = jnp.exp(-jnp.exp(w_log))
    k_ref[...] = k * (1.0 + (gate_a - 1.0) * k_a)
    v_ref[...] = v
    a_ref[...] = -kk
    b_ref[...] = kk * gate_a
    g_ref[...] = g


def rwkv_proj(x, seq, mix, w_r, w_k, w_v, w1, w2, a1, a2, g1, g2, vecs, ones, *, ts):
    n, d = x.shape
    full = lambda arr: pl.BlockSpec(arr.shape, lambda i: (0,) * arr.ndim)
    tok = pl.BlockSpec((ts, d), lambda i: (i, 0))
    prev = pl.BlockSpec((SUBLANES, d), lambda i: (jnp.maximum(i * (ts // SUBLANES) - 1, 0), 0))
    weights = (mix, w_r, w_k, w_v, w1, w2, a1, a2, g1, g2, vecs, ones)
    out = jax.ShapeDtypeStruct((n, d), f32)
    return pl.pallas_call(
        functools.partial(_rwkv_proj_kernel, tiles_per_seq=seq // ts),
        grid=(n // ts,),
        in_specs=[tok, prev] + [full(w) for w in weights],
        out_specs=[tok] * 7,
        out_shape=[out] * 7,
        compiler_params=pltpu.CompilerParams(
            dimension_semantics=("arbitrary",), vmem_limit_bytes=VMEM_LIMIT),
        name="rwkv_proj",
    )(x, x, *weights)


def _rwkv_scan_kernel(r_ref, w_ref, k_ref, v_ref, a_ref, b_ref, o_ref, s_ref):
    nk = s_ref.shape[0]

    @pl.when(pl.program_id(0) == 0)
    def _():
        s_ref[...] = jnp.zeros_like(s_ref)

    def step(t, carry):
        sa0 = jnp.zeros(s_ref.shape[1:], f32)
        sa1 = jnp.zeros(s_ref.shape[1:], f32)
        for kk in range(0, nk, 2):
            sa0 = sa0 + s_ref[kk] * a_ref[t, kk:kk + 1, :]
            sa1 = sa1 + s_ref[kk + 1] * a_ref[t, kk + 1:kk + 2, :]
        sa = sa0 + sa1
        vt = v_ref[t]
        o0 = jnp.zeros(s_ref.shape[1:], f32)
        o1 = jnp.zeros(s_ref.shape[1:], f32)
        for kk in range(nk):
            new = (s_ref[kk] * w_ref[t, kk:kk + 1, :]
                   + (sa * b_ref[t, kk:kk + 1, :] + vt * k_ref[t, kk:kk + 1, :]))
            s_ref[kk] = new
            if kk % 2 == 0:
                o0 = o0 + new * r_ref[t, kk:kk + 1, :]
            else:
                o1 = o1 + new * r_ref[t, kk:kk + 1, :]
        o = o0 + o1
        mu = jnp.mean(o, axis=0, keepdims=True)
        oc = o - mu
        var = jnp.mean(oc * oc, axis=0, keepdims=True)
        o_ref[t] = oc * lax.rsqrt(var + RWKV_GN_EPS)
        return carry

    lax.fori_loop(0, r_ref.shape[0], step, 0)


def rwkv_scan(r, w, k, v, a, b, *, tc):
    s, hd, chains = r.shape
    blk = pl.BlockSpec((tc, hd, chains), lambda i: (i, 0, 0))
    return pl.pallas_call(
        _rwkv_scan_kernel,
        grid=(s // tc,),
        in_specs=[blk] * 6,
        out_specs=blk,
        out_shape=jax.ShapeDtypeStruct((s, hd, chains), f32),
        scratch_shapes=[pltpu.VMEM((hd, hd, chains), f32)],
        compiler_params=pltpu.CompilerParams(
            dimension_semantics=("arbitrary",), vmem_limit_bytes=VMEM_LIMIT),
        name="rwkv_scan",
    )(r, w, k, v, a, b)


def _rwkv_out_kernel(x_ref, o_ref, r_ref, k_ref, v_ref, g_ref, vec_ref, ones_ref, wo_ref, lng_ref, lnb_ref,
                     out_ref):
    lnx_g, lnx_b, r_k = (vec_ref[n:n + 1, :] for n in range(3))
    bonus = _head_sums(r_ref[...] * k_ref[...] * r_k, ones_ref) * v_ref[...]
    y = (o_ref[...] * lnx_g + lnx_b + bonus) * g_ref[...]
    m = jnp.dot(y.astype(bf16), wo_ref[...], preferred_element_type=f32)
    out_ref[...] = _layer_norm_rows(ALPHA * x_ref[...] + m, lng_ref[...], lnb_ref[...])


def rwkv_out(x, o, r, k, v, g, vecs, ones, w_o, ln_g, ln_b, *, ts):
    n, d = x.shape
    full = lambda arr: pl.BlockSpec(arr.shape, lambda i: (0,) * arr.ndim)
    tok = pl.BlockSpec((ts, d), lambda i: (i, 0))
    consts = (vecs, ones, w_o, ln_g.reshape(1, d), ln_b.reshape(1, d))
    return pl.pallas_call(
        _rwkv_out_kernel,
        grid=(n // ts,),
        in_specs=[tok] * 6 + [full(c) for c in consts],
        out_specs=tok,
        out_shape=jax.ShapeDtypeStruct((n, d), f32),
        compiler_params=pltpu.CompilerParams(
            dimension_semantics=("arbitrary",), vmem_limit_bytes=VMEM_LIMIT),
        name="rwkv_out",
    )(x, o, r, k, v, g, *consts)


def _head_ones(d, head):
    seg = jnp.arange(d) // head
    return (seg[:, None] == seg[None, :]).astype(bf16)


def rwkv_layer(x, seq, mix, w_r, w_k, w_v, w0, w1, w2, a0, a1, a2, g1, g2, k_k, k_a, r_k, lnx_g, lnx_b, w_o,
               ln_g, ln_b, *, ts=256, tc=16):
    n, d = x.shape
    bsz = n // seq
    nh, hd = RWKV_HEADS, RWKV_HEAD
    ones = _head_ones(d, hd)
    c = lambda w: w.astype(bf16)
    vecs_in = jnp.stack([w0, a0, k_k, k_a])
    r, w, k, v, a, b, g = rwkv_proj(x, seq, mix, c(w_r), c(w_k), c(w_v), c(w1), c(w2), c(a1), c(a2),
                                    c(g1), c(g2), vecs_in, ones, ts=ts)
    to_chains = lambda t: t.reshape(bsz, seq, nh, hd).transpose(1, 3, 0, 2).reshape(seq, hd, bsz * nh)
    o = rwkv_scan(*(to_chains(t) for t in (r, w, k, v, a, b)), tc=tc)
    o = o.reshape(seq, hd, bsz, nh).transpose(2, 0, 3, 1).reshape(n, d)
    vecs_out = jnp.stack([lnx_g, lnx_b, r_k.reshape(d)])
    return rwkv_out(x, o, r, k, v, g, vecs_out, ones, c(w_o), ln_g, ln_b, ts=ts)


def kernel(x, rg_w_in, rg_conv_w, rg_conv_b, rg_w_a, rg_b_a, rg_w_x, rg_b_x, rg_lambda, rg_w_out, rw_mix, rw_w_r, rw_w_k, rw_w_v, rw_w0, rw_w1, rw_w2, rw_a0, rw_a1, rw_a2, rw_g1, rw_g2, rw_k_k, rw_k_a, rw_r_k, rw_lnx_g, rw_lnx_b, rw_w_o, peer_w_q, peer_sub_keys, peer_u, peer_v, ln_g, ln_b):
    bsz, s, d = x.shape
    x = x.reshape(bsz * s, d)
    for i in range(DEPTH):
        j = i // 2
        if i % 2 == 0:
            x1 = rglru_layer(x, s, rg_w_in[j], rg_conv_w[j], rg_conv_b[j], rg_w_a[j], rg_b_a[j],
                             rg_w_x[j], rg_b_x[j], rg_lambda[j], rg_w_out[j], ln_g[i, 0], ln_b[i, 0])
        else:
            x1 = rwkv_layer(x, s, rw_mix[j], rw_w_r[j], rw_w_k[j], rw_w_v[j], rw_w0[j], rw_w1[j],
                            rw_w2[j], rw_a0[j], rw_a1[j], rw_a2[j], rw_g1[j], rw_g2[j],
                            rw_k_k[j], rw_k_a[j], rw_r_k[j], rw_lnx_g[j], rw_lnx_b[j], rw_w_o[j],
                            ln_g[i, 0], ln_b[i, 0])
        x = peer_layer(x1, peer_w_q[i], peer_sub_keys[i], peer_u[i], peer_v[i], ln_g[i, 1], ln_b[i, 1])
    return x.reshape(bsz, s, d)
```

```python
import functools
import math
import jax, jax.numpy as jnp
from jax import lax
from jax.experimental import pallas as pl
from jax.experimental.pallas import tpu as pltpu

D_MODEL = 1024
DEPTH = 2
RG_WIDTH = 1408
RG_HEADS = 16
RG_BLOCK = 88
RG_CONV = 4
RG_C = 8.0
RWKV_HEAD = 64
RWKV_HEADS = 16
RWKV_GN_EPS = 64e-5
RWKV_L2_EPS = 1e-12
PEER_HEADS = 8
PEER_NKEYS = 128
PEER_DHALF = 128
PEER_TOPK = 16
PEER_BLOCK = 128
ALPHA = (2 * DEPTH) ** 0.25
LN_EPS = 1e-5

LANES = 128
SUBLANES = 8
MXU_COLS = 256
VMEM_LIMIT = 56 << 20

f32 = jnp.float32
bf16 = jnp.bfloat16
u32 = jnp.uint32


def _gelu_tanh(x):
    return 0.5 * x * (1.0 + jnp.tanh(math.sqrt(2.0 / math.pi) * (x + 0.044715 * (x * x * x))))


def _layer_norm_rows(z, g, b):
    mu = jnp.mean(z, axis=-1, keepdims=True)
    zc = z - mu
    var = jnp.mean(zc * zc, axis=-1, keepdims=True)
    return zc * lax.rsqrt(var + LN_EPS) * g + b


def _top_rows(s, k, with_rank=False):
    rows = []
    rank = jnp.full(s.shape, float(k), f32)
    for j in range(k):
        m = jnp.max(s, axis=0, keepdims=True)
        rows.append(m)
        hit = s == m
        if with_rank:
            rank = jnp.where(hit, float(j), rank)
        s = jnp.where(hit, -jnp.inf, s)
    return (rows, rank) if with_rank else rows


def _peer_select_kernel(x_ref, wq_ref, keys_ref, cnt_ref, rk1_ref, p0_ref, p1_ref, q_ref):
    h = pl.program_id(1)

    @pl.when(h == 0)
    def _():
        q_ref[...] = jnp.dot(x_ref[...].astype(bf16), wq_ref[...], preferred_element_type=f32)

    def scores(p):
        off = pl.multiple_of((2 * h + p) * PEER_DHALF, PEER_DHALF)
        qhp = q_ref[:, pl.ds(off, PEER_DHALF)].astype(bf16)
        return lax.dot_general(keys_ref[0, p], qhp, (((1,), (1,)), ((), ())),
                               preferred_element_type=f32)

    s0_all = scores(0)
    s1_all = scores(1)
    for c in range(s0_all.shape[1] // LANES):
        s0 = s0_all[:, c * LANES:(c + 1) * LANES]
        s1 = s1_all[:, c * LANES:(c + 1) * LANES]
        cnt, rank1, p0, p1 = _select_slab(s0, s1)
        cnt_ref[0, c] = _bf16_twice(cnt)
        p0_ref[0, c] = _bf16_twice(p0)
        rk1_ref[0, c] = pltpu.bitcast(rank1.astype(bf16), u32)
        p1_ref[0, c] = pltpu.bitcast(p1.astype(bf16), u32)


def _select_slab(s0, s1):
    a = _top_rows(s0, PEER_TOPK)
    b, rank1 = _top_rows(s1, PEER_TOPK, with_rank=True)
    row = lax.broadcasted_iota(jnp.int32, (PEER_TOPK, s0.shape[1]), 0)
    bmat = jnp.zeros((PEER_TOPK, s0.shape[1]), f32)
    for j in range(PEER_TOPK):
        bmat = jnp.where(row == j, b[j], bmat)
    cands = [a[0] + bmat] + [a[i] + bmat[:SUBLANES] for i in range(1, PEER_TOPK)]
    cand = jnp.concatenate(cands, axis=0)
    tau = _top_rows(cand, PEER_TOPK)[-1]
    cmax = a[0] + b[0]
    z = jnp.sum(jnp.where(cand >= tau, jnp.exp(cand - cmax), 0.0), axis=0, keepdims=True)
    cnt = jnp.zeros_like(s0)
    for i in range(PEER_TOPK):
        cnt_i = jnp.sum(jnp.where(cands[i] >= tau, 1.0, 0.0), axis=0, keepdims=True)
        cnt = jnp.where(s0 == a[i], cnt_i, cnt)
    p0 = jnp.exp(s0 - a[0]) / z
    p1 = jnp.exp(s1 - b[0])
    return cnt, rank1, p0, p1


def _bf16_twice(t):
    bits = pltpu.bitcast(t.astype(bf16).astype(f32), u32) >> 16
    return bits | (bits << 16)


def peer_select(x, wq_b, keys_b, *, tq):
    n, d = x.shape
    nh = PEER_HEADS
    full_spec = pl.BlockSpec((1, tq // LANES, PEER_NKEYS, LANES), lambda i, h: (h, i, 0, 0))
    half_spec = pl.BlockSpec((1, tq // LANES, PEER_NKEYS // 2, LANES), lambda i, h: (h, i, 0, 0))
    full = jax.ShapeDtypeStruct((nh, n // LANES, PEER_NKEYS, LANES), u32)
    half = jax.ShapeDtypeStruct((nh, n // LANES, PEER_NKEYS // 2, LANES), u32)
    return pl.pallas_call(
        _peer_select_kernel,
        grid=(n // tq, nh),
        in_specs=[pl.BlockSpec((tq, d), lambda i, h: (i, 0)),
                  pl.BlockSpec((d, 2 * nh * PEER_DHALF), lambda i, h: (0, 0)),
                  pl.BlockSpec((1, 2, PEER_NKEYS, PEER_DHALF), lambda i, h: (h, 0, 0, 0))],
        out_specs=[full_spec, half_spec, full_spec, half_spec],
        out_shape=[full, half, full, half],
        scratch_shapes=[pltpu.VMEM((tq, 2 * nh * PEER_DHALF), f32)],
        compiler_params=pltpu.CompilerParams(
            dimension_semantics=("arbitrary", "arbitrary"), vmem_limit_bytes=VMEM_LIMIT),
        name="peer_select",
    )(x, wq_b, keys_b)


def _peer_main_kernel(xb_ref, x_ref, cnt_ref, rk1_ref, p0_ref, p1_ref, u_ref, vt_ref,
                      g_ref, b_ref, o_ref, acc_ref, *slab_refs, te, tt):
    j = pl.program_id(1)
    n_i0 = te // PEER_NKEYS
    n_slab = tt // LANES
    per_chunk = MXU_COLS // LANES
    at_refs, ga_refs = slab_refs[:n_slab], slab_refs[n_slab:]
    half = PEER_NKEYS // 2
    unpack = lambda words: pltpu.bitcast(words, bf16)

    @pl.when(j == 0)
    def _():
        acc_ref[...] = jnp.zeros_like(acc_ref)

    def act_chunk(c):
        at = lax.dot_general(u_ref[...], xb_ref[c * MXU_COLS:(c + 1) * MXU_COLS, :],
                             (((1,), (1,)), ((), ())), preferred_element_type=f32)
        for k in range(per_chunk):
            at_refs[c * per_chunk + k][...] = at[:, k * LANES:(k + 1) * LANES]

    def gate_slab(s):
        for il in range(n_i0):
            g = jnp.zeros((PEER_NKEYS, LANES), bf16)
            for h in range(PEER_HEADS):
                cntrow = jnp.broadcast_to(cnt_ref[h, s, il:il + 1, :], (half, LANES))
                p0row = jnp.broadcast_to(p0_ref[h, s, il:il + 1, :], (half, LANES))
                sel = unpack(rk1_ref[h, s]) < unpack(cntrow)
                g = g + jnp.where(sel, unpack(p1_ref[h, s]) * unpack(p0row), jnp.zeros((), bf16))
            ga = _gelu_tanh(at_refs[s][il * PEER_NKEYS:(il + 1) * PEER_NKEYS, :]).astype(bf16) * g
            ga_refs[s][il * half:(il + 1) * half, :] = pltpu.bitcast(ga, u32)

    def out_chunk(c):
        ga = jnp.concatenate([unpack(ga_refs[c * per_chunk + k][...]) for k in range(per_chunk)], axis=1)
        lanes = slice(c * MXU_COLS, (c + 1) * MXU_COLS)
        acc_ref[:, lanes] += jnp.dot(vt_ref[...], ga, preferred_element_type=f32)

    n_chunk = tt // MXU_COLS
    act_chunk(0)
    for c in range(n_chunk):
        gate_slab(c * per_chunk)
        if c + 1 < n_chunk:
            act_chunk(c + 1)
        for k in range(1, per_chunk):
            gate_slab(c * per_chunk + k)
        out_chunk(c)

    @pl.when(j == pl.num_programs(1) - 1)
    def _():
        z = ALPHA * x_ref[...] + acc_ref[...].T
        o_ref[...] = _layer_norm_rows(z, g_ref[...], b_ref[...])


def peer_main(x, xb, stats, u_b, vt_b, ln_g, ln_b, *, tt, te):
    n, d = x.shape
    e = u_b.shape[0]
    nh = PEER_HEADS
    cnt, rk1, p0, p1 = stats
    full_spec = pl.BlockSpec((nh, tt // LANES, te // PEER_NKEYS, LANES), lambda i, j: (0, i, j, 0))
    half_spec = pl.BlockSpec((nh, tt // LANES, PEER_NKEYS // 2, LANES), lambda i, j: (0, i, 0, 0))
    return pl.pallas_call(
        functools.partial(_peer_main_kernel, te=te, tt=tt),
        grid=(n // tt, e // te),
        in_specs=[pl.BlockSpec((tt, d), lambda i, j: (i, 0)),
                  pl.BlockSpec((tt, d), lambda i, j: (i, 0)),
                  full_spec, half_spec, full_spec, half_spec,
                  pl.BlockSpec((te, d), lambda i, j: (j, 0)),
                  pl.BlockSpec((d, te), lambda i, j: (0, j)),
                  pl.BlockSpec((1, d), lambda i, j: (0, 0)),
                  pl.BlockSpec((1, d), lambda i, j: (0, 0))],
        out_specs=pl.BlockSpec((tt, d), lambda i, j: (i, 0)),
        out_shape=jax.ShapeDtypeStruct((n, d), f32),
        scratch_shapes=([pltpu.VMEM((d, tt), f32)]
                        + [pltpu.VMEM((te, LANES), f32)] * (tt // LANES)
                        + [pltpu.VMEM((te // 2, LANES), u32)] * (tt // LANES)),
        compiler_params=pltpu.CompilerParams(
            dimension_semantics=("arbitrary", "arbitrary"), vmem_limit_bytes=VMEM_LIMIT),
        name="peer_main",
    )(xb, x, cnt, rk1, p0, p1, u_b, vt_b, ln_g.reshape(1, d), ln_b.reshape(1, d))


def peer_layer(x, w_q, sub_keys, u, v, ln_g, ln_b, *, tq=256, tt=1024, te=1024):
    stats = peer_select(x, w_q.astype(bf16), sub_keys.astype(bf16), tq=tq)
    return peer_main(x, x.astype(bf16), stats, u.astype(bf16), v.T.astype(bf16), ln_g, ln_b, tt=tt, te=te)


def _rglru_kernel(x_ref, win_ref, cw_ref, vec_ref, wg_ref, wout_ref, lng_ref, lnb_ref, o_ref,
                  tail_ref, h_ref, *, tiles_per_seq):
    i = pl.program_id(0)
    w = RG_WIDTH
    x = x_ref[...]
    ts = x.shape[0]

    @pl.when(i % tiles_per_seq == 0)
    def _():
        tail_ref[...] = jnp.zeros_like(tail_ref)
        h_ref[...] = jnp.zeros_like(h_ref)

    conv_b, b_a, b_x, lam = (vec_ref[n:n + 1, :] for n in range(4))
    hin = jnp.dot(x.astype(bf16), win_ref[...], preferred_element_type=f32)
    gate_branch = _gelu_tanh(hin[:, :w])
    hx = hin[:, w:]

    row = lax.broadcasted_iota(jnp.int32, (ts, w), 0)
    row8 = lax.broadcasted_iota(jnp.int32, (SUBLANES, w), 0)
    tail = tail_ref[...]
    xc = hx * cw_ref[RG_CONV - 1:RG_CONV, :] + conv_b
    for s in range(1, RG_CONV):
        rolled = pltpu.roll(hx, s, axis=0)
        head = jnp.where(row8 < s, pltpu.roll(tail, s, axis=0), rolled[:SUBLANES])
        shifted = jnp.concatenate([head, rolled[SUBLANES:]], axis=0)
        xc = xc + shifted * cw_ref[RG_CONV - 1 - s:RG_CONV - s, :]
    tail_ref[...] = hx[ts - SUBLANES:]

    gates = jnp.dot(xc.astype(bf16), wg_ref[...], preferred_element_type=f32)
    r = _sigmoid(gates[:, :w] + b_a)
    ig = _sigmoid(gates[:, w:] + b_x)
    log_a = (-RG_C * _softplus(-lam)) * r
    a = jnp.exp(log_a)
    u = jnp.sqrt(-jnp.tanh(log_a) * (a * a + 1.0)) * (ig * xc)

    s = 1
    while s < ts:
        keep = row >= s
        a_sh = jnp.where(keep, pltpu.roll(a, s, axis=0), 1.0)
        u_sh = jnp.where(keep, pltpu.roll(u, s, axis=0), 0.0)
        u = a * u_sh + u
        a = a * a_sh
        s *= 2
    hs = u + a * h_ref[SUBLANES - 1:SUBLANES, :]
    h_ref[...] = hs[ts - SUBLANES:]

    m = jnp.dot((hs * gate_branch).astype(bf16), wout_ref[...], preferred_element_type=f32)
    o_ref[...] = _layer_norm_rows(ALPHA * x + m, lng_ref[...], lnb_ref[...])


def rglru_layer(x, seq, w_in, conv_w, conv_b, w_a, b_a, w_x, b_x, lam, w_out, ln_g, ln_b, *, ts=256):
    n, d = x.shape
    w = RG_WIDTH
    blockdiag = lambda t: jax.scipy.linalg.block_diag(*t)
    w_gates = jnp.concatenate([blockdiag(w_a), blockdiag(w_x)], axis=1).astype(bf16)
    vecs = jnp.stack([conv_b, b_a.reshape(w), b_x.reshape(w), lam.reshape(w)])
    consts = (w_in.astype(bf16), conv_w, vecs, w_gates, w_out.astype(bf16), ln_g.reshape(1, d), ln_b.reshape(1, d))
    full = lambda arr: pl.BlockSpec(arr.shape, lambda i: (0,) * arr.ndim, pipeline_mode=pl.Buffered(1))
    tok = pl.BlockSpec((ts, d), lambda i: (i, 0))
    return pl.pallas_call(
        functools.partial(_rglru_kernel, tiles_per_seq=seq // ts),
        grid=(n // ts,),
        in_specs=[tok] + [full(c) for c in consts],
        out_specs=tok,
        out_shape=jax.ShapeDtypeStruct((n, d), f32),
        scratch_shapes=[pltpu.VMEM((SUBLANES, w), f32), pltpu.VMEM((SUBLANES, w), f32)],
        compiler_params=pltpu.CompilerParams(
            dimension_semantics=("arbitrary",), vmem_limit_bytes=VMEM_LIMIT),
        name="rglru",
    )(x, *consts)


def _sigmoid(x):
    return 1.0 / (1.0 + jnp.exp(-x))


def _softplus(x):
    return jnp.maximum(x, 0.0) + jnp.log1p(jnp.exp(-jnp.abs(x)))


def _head_sums(t, ones_ref):
    hi = t.astype(bf16)
    lo = (t - hi.astype(f32)).astype(bf16)
    return (jnp.dot(hi, ones_ref[...], preferred_element_type=f32)
            + jnp.dot(lo, ones_ref[...], preferred_element_type=f32))


def _rwkv_proj_kernel(x_ref, xp_ref, mix_ref, wr_ref, wk_ref, wv_ref, w1_ref, w2_ref, a1_ref, a2_ref,
                      g1_ref, g2_ref, vec_ref, ones_ref,
                      r_ref, w_ref, k_ref, v_ref, a_ref, b_ref, g_ref, *, tiles_per_seq):
    i = pl.program_id(0)
    x = x_ref[...]
    ts = x.shape[0]
    prev_last = jnp.where(i % tiles_per_seq == 0, 0.0, xp_ref[SUBLANES - 1:SUBLANES, :])
    row = lax.broadcasted_iota(jnp.int32, x.shape, 0)
    xprev = jnp.where(row == 0, prev_last, pltpu.roll(x, 1, axis=0))
    xx = xprev - x

    def mixed(m):
        return (x + xx * mix_ref[m:m + 1, :]).astype(bf16)

    def mm(a, w_ref_):
        return jnp.dot(a, w_ref_[...], preferred_element_type=f32)

    w0, a0, k_k, k_a = (vec_ref[n:n + 1, :] for n in range(4))
    r = mm(mixed(0), wr_ref)
    k = mm(mixed(2), wk_ref)
    v = mm(mixed(3), wv_ref)
    lw = mm(jnp.tanh(mm(mixed(1), w1_ref)).astype(bf16), w2_ref)
    w_log = -_softplus(-(w0 + lw)) - 0.5
    gate_a = _sigmoid(a0 + mm(mm(mixed(4), a1_ref).astype(bf16), a2_ref))
    g = mm(_sigmoid(mm(mixed(5), g1_ref)).astype(bf16), g2_ref)
    kk = k * k_k
    kk = kk / jnp.maximum(jnp.sqrt(_head_sums(kk * kk, ones_ref)), RWKV_L2_EPS)
    r_ref[...] = r
    w_ref[...] = jnp.exp(-jnp.exp(w_log))
    k_ref[...] = k * (1.0 + (gate_a - 1.0) * k_a)
    v_ref[...] = v
    a_ref[...] = -kk
    b_ref[...] = kk * gate_a
    g_ref[...] = g


def rwkv_proj(x, seq, mix, w_r, w_k, w_v, w1, w2, a1, a2, g1, g2, vecs, ones, *, ts):
    n, d = x.shape
    full = lambda arr: pl.BlockSpec(arr.shape, lambda i: (0,) * arr.ndim)
    tok = pl.BlockSpec((ts, d), lambda i: (i, 0))
    prev = pl.BlockSpec((SUBLANES, d), lambda i: (jnp.maximum(i * (ts // SUBLANES) - 1, 0), 0))
    weights = (mix, w_r, w_k, w_v, w1, w2, a1, a2, g1, g2, vecs, ones)
    out = jax.ShapeDtypeStruct((n, d), f32)
    return pl.pallas_call(
        functools.partial(_rwkv_proj_kernel, tiles_per_seq=seq // ts),
        grid=(n // ts,),
        in_specs=[tok, prev] + [full(w) for w in weights],
        out_specs=[tok] * 7,
        out_shape=[out] * 7,
        compiler_params=pltpu.CompilerParams(
            dimension_semantics=("arbitrary",), vmem_limit_bytes=VMEM_LIMIT),
        name="rwkv_proj",
    )(x, x, *weights)


def _rwkv_scan_kernel(r_ref, w_ref, k_ref, v_ref, a_ref, b_ref, anext_ref, o_ref, s_ref, sa_ref):
    nk = s_ref.shape[0]
    tc = r_ref.shape[0]

    @pl.when(pl.program_id(0) == 0)
    def _():
        s_ref[...] = jnp.zeros_like(s_ref)
        sa_ref[...] = jnp.zeros_like(sa_ref)

    def step(t, sa, a_next_row):
        vt = v_ref[t]
        o = [jnp.zeros(s_ref.shape[1:], f32)] * 2
        nsa = [jnp.zeros(s_ref.shape[1:], f32)] * 2
        for kk in range(nk):
            new = (s_ref[kk] * w_ref[t, kk:kk + 1, :]
                   + (sa * b_ref[t, kk:kk + 1, :] + vt * k_ref[t, kk:kk + 1, :]))
            s_ref[kk] = new
            o[kk % 2] = o[kk % 2] + new * r_ref[t, kk:kk + 1, :]
            nsa[kk % 2] = nsa[kk % 2] + new * a_next_row(kk)
        ot = o[0] + o[1]
        mu = jnp.mean(ot, axis=0, keepdims=True)
        oc = ot - mu
        var = jnp.mean(oc * oc, axis=0, keepdims=True)
        o_ref[t] = oc * lax.rsqrt(var + RWKV_GN_EPS)
        return nsa[0] + nsa[1]

    sa = lax.fori_loop(0, tc - 1, lambda t, sa: step(t, sa, lambda kk: a_ref[t + 1, kk:kk + 1, :]),
                       sa_ref[...])
    sa_ref[...] = step(tc - 1, sa, lambda kk: anext_ref[0, kk:kk + 1, :])


def rwkv_scan(r, w, k, v, a, b, *, tc):
    s, hd, chains = r.shape
    blk = pl.BlockSpec((tc, hd, chains), lambda i: (i, 0, 0))
    nxt = pl.BlockSpec((1, hd, chains), lambda i: (jnp.minimum((i + 1) * tc, s - 1), 0, 0))
    return pl.pallas_call(
        _rwkv_scan_kernel,
        grid=(s // tc,),
        in_specs=[blk] * 6 + [nxt],
        out_specs=blk,
        out_shape=jax.ShapeDtypeStruct((s, hd, chains), f32),
        scratch_shapes=[pltpu.VMEM((hd, hd, chains), f32), pltpu.VMEM((hd, chains), f32)],
        compiler_params=pltpu.CompilerParams(
            dimension_semantics=("arbitrary",), vmem_limit_bytes=VMEM_LIMIT),
        name="rwkv_scan",
    )(r, w, k, v, a, b, a)


def _rwkv_out_kernel(x_ref, o_ref, r_ref, k_ref, v_ref, g_ref, vec_ref, ones_ref, wo_ref, lng_ref, lnb_ref,
                     out_ref):
    lnx_g, lnx_b, r_k = (vec_ref[n:n + 1, :] for n in range(3))
    bonus = _head_sums(r_ref[...] * k_ref[...] * r_k, ones_ref) * v_ref[...]
    y = (o_ref[...] * lnx_g + lnx_b + bonus) * g_ref[...]
    m = jnp.dot(y.astype(bf16), wo_ref[...], preferred_element_type=f32)
    out_ref[...] = _layer_norm_rows(ALPHA * x_ref[...] + m, lng_ref[...], lnb_ref[...])


def rwkv_out(x, o, r, k, v, g, vecs, ones, w_o, ln_g, ln_b, *, ts):
    n, d = x.shape
    full = lambda arr: pl.BlockSpec(arr.shape, lambda i: (0,) * arr.ndim)
    tok = pl.BlockSpec((ts, d), lambda i: (i, 0))
    consts = (vecs, ones, w_o, ln_g.reshape(1, d), ln_b.reshape(1, d))
    return pl.pallas_call(
        _rwkv_out_kernel,
        grid=(n // ts,),
        in_specs=[tok] * 6 + [full(c) for c in consts],
        out_specs=tok,
        out_shape=jax.ShapeDtypeStruct((n, d), f32),
        compiler_params=pltpu.CompilerParams(
            dimension_semantics=("arbitrary",), vmem_limit_bytes=VMEM_LIMIT),
        name="rwkv_out",
    )(x, o, r, k, v, g, *consts)


def _head_ones(d, head):
    seg = jnp.arange(d) // head
    return (seg[:, None] == seg[None, :]).astype(bf16)


def rwkv_layer(x, seq, mix, w_r, w_k, w_v, w0, w1, w2, a0, a1, a2, g1, g2, k_k, k_a, r_k, lnx_g, lnx_b, w_o,
               ln_g, ln_b, *, ts=256, tc=16):
    n, d = x.shape
    bsz = n // seq
    nh, hd = RWKV_HEADS, RWKV_HEAD
    ones = _head_ones(d, hd)
    c = lambda w: w.astype(bf16)
    vecs_in = jnp.stack([w0, a0, k_k, k_a])
    r, w, k, v, a, b, g = rwkv_proj(x, seq, mix, c(w_r), c(w_k), c(w_v), c(w1), c(w2), c(a1), c(a2),
                                    c(g1), c(g2), vecs_in, ones, ts=ts)
    to_chains = lambda t: t.reshape(bsz, seq, nh, hd).transpose(1, 3, 0, 2).reshape(seq, hd, bsz * nh)
    o = rwkv_scan(*(to_chains(t) for t in (r, w, k, v, a, b)), tc=tc)
    o = o.reshape(seq, hd, bsz, nh).transpose(2, 0, 3, 1).reshape(n, d)
    vecs_out = jnp.stack([lnx_g, lnx_b, r_k.reshape(d)])
    return rwkv_out(x, o, r, k, v, g, vecs_out, ones, c(w_o), ln_g, ln_b, ts=ts)


def kernel(x, rg_w_in, rg_conv_w, rg_conv_b, rg_w_a, rg_b_a, rg_w_x, rg_b_x, rg_lambda, rg_w_out, rw_mix, rw_w_r, rw_w_k, rw_w_v, rw_w0, rw_w1, rw_w2, rw_a0, rw_a1, rw_a2, rw_g1, rw_g2, rw_k_k, rw_k_a, rw_r_k, rw_lnx_g, rw_lnx_b, rw_w_o, peer_w_q, peer_sub_keys, peer_u, peer_v, ln_g, ln_b):
    bsz, s, d = x.shape
    x = x.reshape(bsz * s, d)
    for i in range(DEPTH):
        j = i // 2
        if i % 2 == 0:
            x1 = rglru_layer(x, s, rg_w_in[j], rg_conv_w[j], rg_conv_b[j], rg_w_a[j], rg_b_a[j],
                             rg_w_x[j], rg_b_x[j], rg_lambda[j], rg_w_out[j], ln_g[i, 0], ln_b[i, 0])
        else:
            x1 = rwkv_layer(x, s, rw_mix[j], rw_w_r[j], rw_w_k[j], rw_w_v[j], rw_w0[j], rw_w1[j],
                            rw_w2[j], rw_a0[j], rw_a1[j], rw_a2[j], rw_g1[j], rw_g2[j],
                            rw_k_k[j], rw_k_a[j], rw_r_k[j], rw_lnx_g[j], rw_lnx_b[j], rw_w_o[j],
                            ln_g[i, 0], ln_b[i, 0])
        x = peer_layer(x1, peer_w_q[i], peer_sub_keys[i], peer_u[i], peer_v[i], ln_g[i, 1], ln_b[i, 1])
    return x.reshape(bsz, s, d)
```

```python
import functools
import math
import jax, jax.numpy as jnp
from jax import lax
from jax.experimental import pallas as pl
from jax.experimental.pallas import tpu as pltpu

D_MODEL = 1024
DEPTH = 2
RG_WIDTH = 1408
RG_HEADS = 16
RG_BLOCK = 88
RG_CONV = 4
RG_C = 8.0
RWKV_HEAD = 64
RWKV_HEADS = 16
RWKV_GN_EPS = 64e-5
RWKV_L2_EPS = 1e-12
PEER_HEADS = 8
PEER_NKEYS = 128
PEER_DHALF = 128
PEER_TOPK = 16
PEER_BLOCK = 128
ALPHA = (2 * DEPTH) ** 0.25
LN_EPS = 1e-5

LANES = 128
SUBLANES = 8
MXU_COLS = 256
VMEM_LIMIT = 56 << 20

f32 = jnp.float32
bf16 = jnp.bfloat16
u32 = jnp.uint32


def _gelu_tanh(x):
    return 0.5 * x * (1.0 + jnp.tanh(math.sqrt(2.0 / math.pi) * (x + 0.044715 * (x * x * x))))


def _layer_norm_rows(z, g, b):
    mu = jnp.mean(z, axis=-1, keepdims=True)
    zc = z - mu
    var = jnp.mean(zc * zc, axis=-1, keepdims=True)
    return zc * lax.rsqrt(var + LN_EPS) * g + b


def _top_rows(s, k, with_rank=False):
    rows = []
    rank = jnp.full(s.shape, float(k), f32)
    for j in range(k):
        m = jnp.max(s, axis=0, keepdims=True)
        rows.append(m)
        hit = s == m
        if with_rank:
            rank = jnp.where(hit, float(j), rank)
        s = jnp.where(hit, -jnp.inf, s)
    return (rows, rank) if with_rank else rows


def _peer_select_kernel(x_ref, wq_ref, keys_ref, cnt_ref, rk1_ref, p0_ref, p1_ref, q_ref):
    h = pl.program_id(1)

    @pl.when(h == 0)
    def _():
        q_ref[...] = jnp.dot(x_ref[...].astype(bf16), wq_ref[...], preferred_element_type=f32)

    def scores(p):
        off = pl.multiple_of((2 * h + p) * PEER_DHALF, PEER_DHALF)
        qhp = q_ref[:, pl.ds(off, PEER_DHALF)].astype(bf16)
        return lax.dot_general(keys_ref[0, p], qhp, (((1,), (1,)), ((), ())),
                               preferred_element_type=f32)

    def select_all(stable):
        s0_all = scores(0)
        s1_all = scores(1)
        tied = 0.0
        for c in range(s0_all.shape[1] // LANES):
            lanes = slice(c * LANES, (c + 1) * LANES)
            cnt, rank1, p0, p1, tied_c = _select_slab(s0_all[:, lanes], s1_all[:, lanes], stable=stable)
            cnt_ref[0, c] = _bf16_twice(cnt)
            p0_ref[0, c] = _bf16_twice(p0)
            rk1_ref[0, c] = pltpu.bitcast(rank1.astype(bf16), u32)
            p1_ref[0, c] = pltpu.bitcast(p1.astype(bf16), u32)
            if not stable:
                tied = jnp.maximum(tied, jnp.max(tied_c))
        return tied

    tied = select_all(stable=False)

    @pl.when(tied > 0.0)
    def _():
        select_all(stable=True)


def _top_rows_stable(s, k):
    rows = []
    idx = lax.broadcasted_iota(jnp.int32, s.shape, 0)
    rank = jnp.full(s.shape, float(k), f32)
    for j in range(k):
        m = jnp.max(s, axis=0, keepdims=True)
        rows.append(m)
        first = jnp.min(jnp.where(s == m, idx, s.shape[0]), axis=0, keepdims=True)
        pick = idx == first
        rank = jnp.where(pick, float(j), rank)
        s = jnp.where(pick, -jnp.inf, s)
    return rows, rank


def _select_slab(s0, s1, *, stable):
    if stable:
        a, rank0 = _top_rows_stable(s0, PEER_TOPK)
        b, rank1 = _top_rows_stable(s1, PEER_TOPK)
    else:
        a = _top_rows(s0, PEER_TOPK)
        b, rank1 = _top_rows(s1, PEER_TOPK, with_rank=True)
    row = lax.broadcasted_iota(jnp.int32, (PEER_TOPK, s0.shape[1]), 0)
    bmat = jnp.zeros((PEER_TOPK, s0.shape[1]), f32)
    for j in range(PEER_TOPK):
        bmat = jnp.where(row == j, b[j], bmat)
    cands = [a[0] + bmat] + [a[i] + bmat[:SUBLANES] for i in range(1, PEER_TOPK)]
    cand = jnp.concatenate(cands, axis=0)
    if stable:
        picked = _top_rows_stable(cand, PEER_TOPK)[1] < float(PEER_TOPK)
    else:
        picked = cand >= _top_rows(cand, PEER_TOPK)[-1]
    picked = jnp.where(picked, 1.0, 0.0)
    cmax = a[0] + b[0]
    z = jnp.sum(picked * jnp.exp(cand - cmax), axis=0, keepdims=True)
    cnt = jnp.zeros_like(s0)
    total = jnp.zeros_like(z)
    lo = 0
    for i in range(PEER_TOPK):
        hi = lo + cands[i].shape[0]
        cnt_i = jnp.sum(picked[lo:hi], axis=0, keepdims=True)
        lo = hi
        total = total + cnt_i
        cnt = jnp.where((rank0 == float(i)) if stable else (s0 == a[i]), cnt_i, cnt)
    p0 = jnp.exp(s0 - a[0]) / z
    p1 = jnp.exp(s1 - b[0])
    if stable:
        return cnt, rank1, p0, p1, None
    k = float(PEER_TOPK)
    n0 = jnp.sum(jnp.where(s0 >= a[-1], 1.0, 0.0), axis=0, keepdims=True)
    n1 = jnp.sum(jnp.where(rank1 < k, 1.0, 0.0), axis=0, keepdims=True)
    tied = jnp.abs(n0 - k) + jnp.abs(n1 - k) + jnp.abs(total - k)
    return cnt, rank1, p0, p1, tied


def _bf16_twice(t):
    bits = pltpu.bitcast(t.astype(bf16).astype(f32), u32) >> 16
    return bits | (bits << 16)


def peer_select(x, wq_b, keys_b, *, tq):
    n, d = x.shape
    nh = PEER_HEADS
    full_spec = pl.BlockSpec((1, tq // LANES, PEER_NKEYS, LANES), lambda i, h: (h, i, 0, 0))
    half_spec = pl.BlockSpec((1, tq // LANES, PEER_NKEYS // 2, LANES), lambda i, h: (h, i, 0, 0))
    full = jax.ShapeDtypeStruct((nh, n // LANES, PEER_NKEYS, LANES), u32)
    half = jax.ShapeDtypeStruct((nh, n // LANES, PEER_NKEYS // 2, LANES), u32)
    return pl.pallas_call(
        _peer_select_kernel,
        grid=(n // tq, nh),
        in_specs=[pl.BlockSpec((tq, d), lambda i, h: (i, 0)),
                  pl.BlockSpec((d, 2 * nh * PEER_DHALF), lambda i, h: (0, 0)),
                  pl.BlockSpec((1, 2, PEER_NKEYS, PEER_DHALF), lambda i, h: (h, 0, 0, 0))],
        out_specs=[full_spec, half_spec, full_spec, half_spec],
        out_shape=[full, half, full, half],
        scratch_shapes=[pltpu.VMEM((tq, 2 * nh * PEER_DHALF), f32)],
        compiler_params=pltpu.CompilerParams(
            dimension_semantics=("arbitrary", "arbitrary"), vmem_limit_bytes=VMEM_LIMIT),
        name="peer_select",
    )(x, wq_b, keys_b)


def _peer_main_kernel(xb_ref, x_ref, cnt_ref, rk1_ref, p0_ref, p1_ref, u_ref, vt_ref,
                      g_ref, b_ref, o_ref, acc_ref, *slab_refs, te, tt):
    j = pl.program_id(1)
    n_i0 = te // PEER_NKEYS
    n_slab = tt // LANES
    per_chunk = MXU_COLS // LANES
    at_refs, ga_refs = slab_refs[:n_slab], slab_refs[n_slab:]
    half = PEER_NKEYS // 2
    unpack = lambda words: pltpu.bitcast(words, bf16)

    @pl.when(j == 0)
    def _():
        acc_ref[...] = jnp.zeros_like(acc_ref)

    def act_chunk(c):
        at = lax.dot_general(u_ref[...], xb_ref[c * MXU_COLS:(c + 1) * MXU_COLS, :],
                             (((1,), (1,)), ((), ())), preferred_element_type=f32)
        for k in range(per_chunk):
            at_refs[c * per_chunk + k][...] = at[:, k * LANES:(k + 1) * LANES]

    def gate_slab(s):
        for il in range(n_i0):
            g = jnp.zeros((PEER_NKEYS, LANES), bf16)
            for h in range(PEER_HEADS):
                cntrow = jnp.broadcast_to(cnt_ref[h, s, il:il + 1, :], (half, LANES))
                p0row = jnp.broadcast_to(p0_ref[h, s, il:il + 1, :], (half, LANES))
                sel = unpack(rk1_ref[h, s]) < unpack(cntrow)
                g = g + jnp.where(sel, unpack(p1_ref[h, s]) * unpack(p0row), jnp.zeros((), bf16))
            ga = _gelu_tanh(at_refs[s][il * PEER_NKEYS:(il + 1) * PEER_NKEYS, :]).astype(bf16) * g
            ga_refs[s][il * half:(il + 1) * half, :] = pltpu.bitcast(ga, u32)

    def out_chunk(c):
        ga = jnp.concatenate([unpack(ga_refs[c * per_chunk + k][...]) for k in range(per_chunk)], axis=1)
        lanes = slice(c * MXU_COLS, (c + 1) * MXU_COLS)
        acc_ref[:, lanes] += jnp.dot(vt_ref[...], ga, preferred_element_type=f32)

    n_chunk = tt // MXU_COLS
    act_chunk(0)
    for c in range(n_chunk):
        gate_slab(c * per_chunk)
        if c + 1 < n_chunk:
            act_chunk(c + 1)
        for k in range(1, per_chunk):
            gate_slab(c * per_chunk + k)
        out_chunk(c)

    @pl.when(j == pl.num_programs(1) - 1)
    def _():
        z = ALPHA * x_ref[...] + acc_ref[...].T
        o_ref[...] = _layer_norm_rows(z, g_ref[...], b_ref[...])


def peer_main(x, xb, stats, u_b, vt_b, ln_g, ln_b, *, tt, te):
    n, d = x.shape
    e = u_b.shape[0]
    nh = PEER_HEADS
    cnt, rk1, p0, p1 = stats
    full_spec = pl.BlockSpec((nh, tt // LANES, te // PEER_NKEYS, LANES), lambda i, j: (0, i, j, 0))
    half_spec = pl.BlockSpec((nh, tt // LANES, PEER_NKEYS // 2, LANES), lambda i, j: (0, i, 0, 0))
    return pl.pallas_call(
        functools.partial(_peer_main_kernel, te=te, tt=tt),
        grid=(n // tt, e // te),
        in_specs=[pl.BlockSpec((tt, d), lambda i, j: (i, 0)),
                  pl.BlockSpec((tt, d), lambda i, j: (i, 0)),
                  full_spec, half_spec, full_spec, half_spec,
                  pl.BlockSpec((te, d), lambda i, j: (j, 0)),
                  pl.BlockSpec((d, te), lambda i, j: (0, j)),
                  pl.BlockSpec((1, d), lambda i, j: (0, 0)),
                  pl.BlockSpec((1, d), lambda i, j: (0, 0))],
        out_specs=pl.BlockSpec((tt, d), lambda i, j: (i, 0)),
        out_shape=jax.ShapeDtypeStruct((n, d), f32),
        scratch_shapes=([pltpu.VMEM((d, tt), f32)]
                        + [pltpu.VMEM((te, LANES), f32)] * (tt // LANES)
                        + [pltpu.VMEM((te // 2, LANES), u32)] * (tt // LANES)),
        compiler_params=pltpu.CompilerParams(
            dimension_semantics=("arbitrary", "arbitrary"), vmem_limit_bytes=VMEM_LIMIT),
        name="peer_main",
    )(xb, x, cnt, rk1, p0, p1, u_b, vt_b, ln_g.reshape(1, d), ln_b.reshape(1, d))


def peer_layer(x, w_q, sub_keys, u, v, ln_g, ln_b, *, tq=256, tt=1024, te=1024):
    stats = peer_select(x, w_q.astype(bf16), sub_keys.astype(bf16), tq=tq)
    return peer_main(x, x.astype(bf16), stats, u.astype(bf16), v.T.astype(bf16), ln_g, ln_b, tt=tt, te=te)


def _rglru_kernel(x_ref, win_ref, cw_ref, vec_ref, wg_ref, wout_ref, lng_ref, lnb_ref, o_ref,
                  tail_ref, h_ref, *, tiles_per_seq):
    i = pl.program_id(0)
    w = RG_WIDTH
    x = x_ref[...]
    ts = x.shape[0]

    @pl.when(i % tiles_per_seq == 0)
    def _():
        tail_ref[...] = jnp.zeros_like(tail_ref)
        h_ref[...] = jnp.zeros_like(h_ref)

    conv_b, b_a, b_x, lam = (vec_ref[n:n + 1, :] for n in range(4))
    hin = jnp.dot(x.astype(bf16), win_ref[...], preferred_element_type=f32)
    gate_branch = _gelu_tanh(hin[:, :w])
    hx = hin[:, w:]

    row = lax.broadcasted_iota(jnp.int32, (ts, w), 0)
    row8 = lax.broadcasted_iota(jnp.int32, (SUBLANES, w), 0)
    tail = tail_ref[...]
    xc = hx * cw_ref[RG_CONV - 1:RG_CONV, :] + conv_b
    for s in range(1, RG_CONV):
        rolled = pltpu.roll(hx, s, axis=0)
        head = jnp.where(row8 < s, pltpu.roll(tail, s, axis=0), rolled[:SUBLANES])
        shifted = jnp.concatenate([head, rolled[SUBLANES:]], axis=0)
        xc = xc + shifted * cw_ref[RG_CONV - 1 - s:RG_CONV - s, :]
    tail_ref[...] = hx[ts - SUBLANES:]

    gates = jnp.dot(xc.astype(bf16), wg_ref[...], preferred_element_type=f32)
    r = _sigmoid(gates[:, :w] + b_a)
    ig = _sigmoid(gates[:, w:] + b_x)
    log_a = (-RG_C * _softplus(-lam)) * r
    a = jnp.exp(log_a)
    u = jnp.sqrt(-jnp.tanh(log_a) * (a * a + 1.0)) * (ig * xc)

    s = 1
    while s < ts:
        keep = row >= s
        a_sh = jnp.where(keep, pltpu.roll(a, s, axis=0), 1.0)
        u_sh = jnp.where(keep, pltpu.roll(u, s, axis=0), 0.0)
        u = a * u_sh + u
        a = a * a_sh
        s *= 2
    hs = u + a * h_ref[SUBLANES - 1:SUBLANES, :]
    h_ref[...] = hs[ts - SUBLANES:]

    m = jnp.dot((hs * gate_branch).astype(bf16), wout_ref[...], preferred_element_type=f32)
    o_ref[...] = _layer_norm_rows(ALPHA * x + m, lng_ref[...], lnb_ref[...])


def rglru_layer(x, seq, w_in, conv_w, conv_b, w_a, b_a, w_x, b_x, lam, w_out, ln_g, ln_b, *, ts=256):
    n, d = x.shape
    w = RG_WIDTH
    blockdiag = lambda t: jax.scipy.linalg.block_diag(*t)
    w_gates = jnp.concatenate([blockdiag(w_a), blockdiag(w_x)], axis=1).astype(bf16)
    vecs = jnp.stack([conv_b, b_a.reshape(w), b_x.reshape(w), lam.reshape(w)])
    consts = (w_in.astype(bf16), conv_w, vecs, w_gates, w_out.astype(bf16), ln_g.reshape(1, d), ln_b.reshape(1, d))
    full = lambda arr: pl.BlockSpec(arr.shape, lambda i: (0,) * arr.ndim, pipeline_mode=pl.Buffered(1))
    tok = pl.BlockSpec((ts, d), lambda i: (i, 0))
    return pl.pallas_call(
        functools.partial(_rglru_kernel, tiles_per_seq=seq // ts),
        grid=(n // ts,),
        in_specs=[tok] + [full(c) for c in consts],
        out_specs=tok,
        out_shape=jax.ShapeDtypeStruct((n, d), f32),
        scratch_shapes=[pltpu.VMEM((SUBLANES, w), f32), pltpu.VMEM((SUBLANES, w), f32)],
        compiler_params=pltpu.CompilerParams(
            dimension_semantics=("arbitrary",), vmem_limit_bytes=VMEM_LIMIT),
        name="rglru",
    )(x, *consts)


def _sigmoid(x):
    return 1.0 / (1.0 + jnp.exp(-x))


def _softplus(x):
    return jnp.maximum(x, 0.0) + jnp.log1p(jnp.exp(-jnp.abs(x)))


def _head_sums(t, ones_ref):
    hi = t.astype(bf16)
    lo = (t - hi.astype(f32)).astype(bf16)
    return (jnp.dot(hi, ones_ref[...], preferred_element_type=f32)
            + jnp.dot(lo, ones_ref[...], preferred_element_type=f32))


def _rwkv_proj_kernel(x_ref, xp_ref, mix_ref, wr_ref, wk_ref, wv_ref, w1_ref, w2_ref, a1_ref, a2_ref,
                      g1_ref, g2_ref, vec_ref,
                      r_ref, w_ref, k_ref, v_ref, gate_ref, g_ref, *, tiles_per_seq):
    i = pl.program_id(0)
    x = x_ref[...]
    ts = x.shape[0]
    prev_last = jnp.where(i % tiles_per_seq == 0, 0.0, xp_ref[SUBLANES - 1:SUBLANES, :])
    row = lax.broadcasted_iota(jnp.int32, x.shape, 0)
    xprev = jnp.where(row == 0, prev_last, pltpu.roll(x, 1, axis=0))
    xx = xprev - x

    def mixed(m):
        return (x + xx * mix_ref[m:m + 1, :]).astype(bf16)

    def mm(a, w_ref_):
        return jnp.dot(a, w_ref_[...], preferred_element_type=f32)

    w0, a0 = (vec_ref[n:n + 1, :] for n in range(2))
    lw = mm(jnp.tanh(mm(mixed(1), w1_ref)).astype(bf16), w2_ref)
    w_log = -_softplus(-(w0 + lw)) - 0.5
    r_ref[...] = mm(mixed(0), wr_ref)
    w_ref[...] = jnp.exp(-jnp.exp(w_log))
    k_ref[...] = mm(mixed(2), wk_ref)
    v_ref[...] = mm(mixed(3), wv_ref)
    gate_ref[...] = _sigmoid(a0 + mm(mm(mixed(4), a1_ref).astype(bf16), a2_ref))
    g_ref[...] = mm(_sigmoid(mm(mixed(5), g1_ref)).astype(bf16), g2_ref)


def rwkv_proj(x, seq, mix, w_r, w_k, w_v, w1, w2, a1, a2, g1, g2, vecs, *, ts):
    n, d = x.shape
    full = lambda arr: pl.BlockSpec(arr.shape, lambda i: (0,) * arr.ndim)
    tok = pl.BlockSpec((ts, d), lambda i: (i, 0))
    prev = pl.BlockSpec((SUBLANES, d), lambda i: (jnp.maximum(i * (ts // SUBLANES) - 1, 0), 0))
    weights = (mix, w_r, w_k, w_v, w1, w2, a1, a2, g1, g2, vecs)
    out = jax.ShapeDtypeStruct((n, d), f32)
    return pl.pallas_call(
        functools.partial(_rwkv_proj_kernel, tiles_per_seq=seq // ts),
        grid=(n // ts,),
        in_specs=[tok, prev] + [full(w) for w in weights],
        out_specs=[tok] * 6,
        out_shape=[out] * 6,
        compiler_params=pltpu.CompilerParams(
            dimension_semantics=("arbitrary",), vmem_limit_bytes=VMEM_LIMIT),
        name="rwkv_proj",
    )(x, x, *weights)


def _rwkv_scan_kernel(r_ref, w_ref, k_ref, v_ref, gate_ref, knext_ref, kk_ref, ka_ref, o_ref,
                      s_ref, sa_ref, a_ref, b_ref, km_ref):
    nk = s_ref.shape[0]
    tc = r_ref.shape[0]

    @pl.when(pl.program_id(0) == 0)
    def _():
        s_ref[...] = jnp.zeros_like(s_ref)
        sa_ref[...] = jnp.zeros_like(sa_ref)

    def unit_key(k):
        kk = k * kk_ref[...]
        norm = jnp.sqrt(jnp.sum(kk * kk, axis=-2, keepdims=True))
        return kk / jnp.maximum(norm, RWKV_L2_EPS)

    k_blk = k_ref[...]
    gate = gate_ref[...]
    kk_blk = unit_key(k_blk)
    a_ref[0:tc] = -kk_blk
    a_ref[tc:tc + 1] = -unit_key(knext_ref[...])
    b_ref[...] = kk_blk * gate
    km_ref[...] = k_blk * (1.0 + (gate - 1.0) * ka_ref[...])

    def step(t, sa):
        vt = v_ref[t]
        o = [jnp.zeros(s_ref.shape[1:], f32)] * 2
        nsa = [jnp.zeros(s_ref.shape[1:], f32)] * 2
        for kk in range(nk):
            new = (s_ref[kk] * w_ref[t, kk:kk + 1, :]
                   + (sa * b_ref[t, kk:kk + 1, :] + vt * km_ref[t, kk:kk + 1, :]))
            s_ref[kk] = new
            o[kk % 2] = o[kk % 2] + new * r_ref[t, kk:kk + 1, :]
            nsa[kk % 2] = nsa[kk % 2] + new * a_ref[t + 1, kk:kk + 1, :]
        ot = o[0] + o[1]
        mu = jnp.mean(ot, axis=0, keepdims=True)
        oc = ot - mu
        var = jnp.mean(oc * oc, axis=0, keepdims=True)
        o_ref[t] = oc * lax.rsqrt(var + RWKV_GN_EPS)
        return nsa[0] + nsa[1]

    sa_ref[...] = lax.fori_loop(0, tc, step, sa_ref[...])


def rwkv_scan(r, w, k, v, gate, kk_vec, ka_vec, *, tc):
    s, hd, chains = r.shape
    blk = pl.BlockSpec((tc, hd, chains), lambda i: (i, 0, 0))
    nxt = pl.BlockSpec((1, hd, chains), lambda i: (jnp.minimum((i + 1) * tc, s - 1), 0, 0))
    vec = pl.BlockSpec((hd, chains), lambda i: (0, 0))
    return pl.pallas_call(
        _rwkv_scan_kernel,
        grid=(s // tc,),
        in_specs=[blk] * 5 + [nxt, vec, vec],
        out_specs=blk,
        out_shape=jax.ShapeDtypeStruct((s, hd, chains), f32),
        scratch_shapes=[pltpu.VMEM((hd, hd, chains), f32), pltpu.VMEM((hd, chains), f32),
                        pltpu.VMEM((tc + 1, hd, chains), f32), pltpu.VMEM((tc, hd, chains), f32),
                        pltpu.VMEM((tc, hd, chains), f32)],
        compiler_params=pltpu.CompilerParams(
            dimension_semantics=("arbitrary",), vmem_limit_bytes=VMEM_LIMIT),
        name="rwkv_scan",
    )(r, w, k, v, gate, k, kk_vec, ka_vec)


def _rwkv_out_kernel(x_ref, o_ref, r_ref, k_ref, gate_ref, v_ref, g_ref, vec_ref, ones_ref, wo_ref,
                     lng_ref, lnb_ref, out_ref):
    lnx_g, lnx_b, r_k, k_a = (vec_ref[n:n + 1, :] for n in range(4))
    k_mod = k_ref[...] * (1.0 + (gate_ref[...] - 1.0) * k_a)
    bonus = _head_sums(r_ref[...] * k_mod * r_k, ones_ref) * v_ref[...]
    y = (o_ref[...] * lnx_g + lnx_b + bonus) * g_ref[...]
    m = jnp.dot(y.astype(bf16), wo_ref[...], preferred_element_type=f32)
    out_ref[...] = _layer_norm_rows(ALPHA * x_ref[...] + m, lng_ref[...], lnb_ref[...])


def rwkv_out(x, o, r, k, gate, v, g, vecs, ones, w_o, ln_g, ln_b, *, ts):
    n, d = x.shape
    full = lambda arr: pl.BlockSpec(arr.shape, lambda i: (0,) * arr.ndim)
    tok = pl.BlockSpec((ts, d), lambda i: (i, 0))
    consts = (vecs, ones, w_o, ln_g.reshape(1, d), ln_b.reshape(1, d))
    return pl.pallas_call(
        _rwkv_out_kernel,
        grid=(n // ts,),
        in_specs=[tok] * 7 + [full(c) for c in consts],
        out_specs=tok,
        out_shape=jax.ShapeDtypeStruct((n, d), f32),
        compiler_params=pltpu.CompilerParams(
            dimension_semantics=("arbitrary",), vmem_limit_bytes=VMEM_LIMIT),
        name="rwkv_out",
    )(x, o, r, k, gate, v, g, *consts)


def _head_ones(d, head):
    seg = jnp.arange(d) // head
    return (seg[:, None] == seg[None, :]).astype(bf16)


def rwkv_layer(x, seq, mix, w_r, w_k, w_v, w0, w1, w2, a0, a1, a2, g1, g2, k_k, k_a, r_k, lnx_g, lnx_b, w_o,
               ln_g, ln_b, *, ts=256, tc=16):
    n, d = x.shape
    bsz = n // seq
    nh, hd = RWKV_HEADS, RWKV_HEAD
    ones = _head_ones(d, hd)
    c = lambda w: w.astype(bf16)
    r, w, k, v, gate, g = rwkv_proj(x, seq, mix, c(w_r), c(w_k), c(w_v), c(w1), c(w2), c(a1), c(a2),
                                    c(g1), c(g2), jnp.stack([w0, a0]), ts=ts)
    to_chains = lambda t: t.reshape(bsz, seq, nh, hd).transpose(1, 3, 0, 2).reshape(seq, hd, bsz * nh)
    vec_chains = lambda t: jnp.tile(t.reshape(nh, hd).T, (1, bsz))
    o = rwkv_scan(*(to_chains(t) for t in (r, w, k, v, gate)), vec_chains(k_k), vec_chains(k_a), tc=tc)
    o = o.reshape(seq, hd, bsz, nh).transpose(2, 0, 3, 1).reshape(n, d)
    vecs_out = jnp.stack([lnx_g, lnx_b, r_k.reshape(d), k_a])
    return rwkv_out(x, o, r, k, gate, v, g, vecs_out, ones, c(w_o), ln_g, ln_b, ts=ts)


def kernel(x, rg_w_in, rg_conv_w, rg_conv_b, rg_w_a, rg_b_a, rg_w_x, rg_b_x, rg_lambda, rg_w_out, rw_mix, rw_w_r, rw_w_k, rw_w_v, rw_w0, rw_w1, rw_w2, rw_a0, rw_a1, rw_a2, rw_g1, rw_g2, rw_k_k, rw_k_a, rw_r_k, rw_lnx_g, rw_lnx_b, rw_w_o, peer_w_q, peer_sub_keys, peer_u, peer_v, ln_g, ln_b):
    bsz, s, d = x.shape
    x = x.reshape(bsz * s, d)
    for i in range(DEPTH):
        j = i // 2
        if i % 2 == 0:
            x1 = rglru_layer(x, s, rg_w_in[j], rg_conv_w[j], rg_conv_b[j], rg_w_a[j], rg_b_a[j],
                             rg_w_x[j], rg_b_x[j], rg_lambda[j], rg_w_out[j], ln_g[i, 0], ln_b[i, 0])
        else:
            x1 = rwkv_layer(x, s, rw_mix[j], rw_w_r[j], rw_w_k[j], rw_w_v[j], rw_w0[j], rw_w1[j],
                            rw_w2[j], rw_a0[j], rw_a1[j], rw_a2[j], rw_g1[j], rw_g2[j],
                            rw_k_k[j], rw_k_a[j], rw_r_k[j], rw_lnx_g[j], rw_lnx_b[j], rw_w_o[j],
                            ln_g[i, 0], ln_b[i, 0])
        x = peer_layer(x1, peer_w_q[i], peer_sub_keys[i], peer_u[i], peer_v[i], ln_g[i, 1], ln_b[i, 1])
    return x.reshape(bsz, s, d)
```

```python
import functools
import math
import jax, jax.numpy as jnp
from jax import lax
from jax.experimental import pallas as pl
from jax.experimental.pallas import tpu as pltpu

D_MODEL = 1024
DEPTH = 2
RG_WIDTH = 1408
RG_HEADS = 16
RG_BLOCK = 88
RG_CONV = 4
RG_C = 8.0
RWKV_HEAD = 64
RWKV_HEADS = 16
RWKV_GN_EPS = 64e-5
RWKV_L2_EPS = 1e-12
PEER_HEADS = 8
PEER_NKEYS = 128
PEER_DHALF = 128
PEER_TOPK = 16
PEER_BLOCK = 128
ALPHA = (2 * DEPTH) ** 0.25
LN_EPS = 1e-5

LANES = 128
SUBLANES = 8
MXU_COLS = 256
VMEM_LIMIT = 56 << 20

f32 = jnp.float32
bf16 = jnp.bfloat16
u32 = jnp.uint32


def _gelu_tanh(x):
    return 0.5 * x * (1.0 + jnp.tanh(math.sqrt(2.0 / math.pi) * (x + 0.044715 * (x * x * x))))


def _layer_norm_rows(z, g, b):
    mu = jnp.mean(z, axis=-1, keepdims=True)
    zc = z - mu
    var = jnp.mean(zc * zc, axis=-1, keepdims=True)
    return zc * lax.rsqrt(var + LN_EPS) * g + b


def _top_rows(s, k, with_rank=False):
    rows = []
    rank = jnp.full(s.shape, float(k), f32)
    for j in range(k):
        m = jnp.max(s, axis=0, keepdims=True)
        rows.append(m)
        hit = s == m
        if with_rank:
            rank = jnp.where(hit, float(j), rank)
        s = jnp.where(hit, -jnp.inf, s)
    return (rows, rank) if with_rank else rows


def _peer_select_kernel(x_ref, wq_ref, keys_ref, cnt_ref, rk1_ref, p0_ref, p1_ref, q_ref):
    q_ref[...] = jnp.dot(x_ref[...].astype(bf16), wq_ref[...], preferred_element_type=f32)

    def select_head(h, tied, *, stable):
        def scores(p):
            off = pl.multiple_of((2 * h + p) * PEER_DHALF, PEER_DHALF)
            qhp = q_ref[:, pl.ds(off, PEER_DHALF)].astype(bf16)
            return lax.dot_general(keys_ref[h, p], qhp, (((1,), (1,)), ((), ())),
                                   preferred_element_type=f32)

        s0_all = scores(0)
        s1_all = scores(1)
        for c in range(s0_all.shape[1] // LANES):
            lanes = slice(c * LANES, (c + 1) * LANES)
            cnt, rank1, p0, p1, tied_c = _select_slab(s0_all[:, lanes], s1_all[:, lanes], stable=stable)
            cnt_ref[h, c] = _bf16_twice(cnt)
            p0_ref[h, c] = _bf16_twice(p0)
            rk1_ref[h, c] = pltpu.bitcast(rank1.astype(bf16), u32)
            p1_ref[h, c] = pltpu.bitcast(p1.astype(bf16), u32)
            if not stable:
                tied = jnp.maximum(tied, tied_c)
        return tied

    tied = lax.fori_loop(0, PEER_HEADS, functools.partial(select_head, stable=False),
                         jnp.zeros((1, LANES), f32))

    @pl.when(jnp.max(tied) > 0.0)
    def _():
        lax.fori_loop(0, PEER_HEADS, functools.partial(select_head, stable=True), tied)


def _top_rows_stable(s, k):
    rows = []
    idx = lax.broadcasted_iota(jnp.int32, s.shape, 0)
    rank = jnp.full(s.shape, float(k), f32)
    for j in range(k):
        m = jnp.max(s, axis=0, keepdims=True)
        rows.append(m)
        first = jnp.min(jnp.where(s == m, idx, s.shape[0]), axis=0, keepdims=True)
        pick = idx == first
        rank = jnp.where(pick, float(j), rank)
        s = jnp.where(pick, -jnp.inf, s)
    return rows, rank


def _select_slab(s0, s1, *, stable):
    if stable:
        a, rank0 = _top_rows_stable(s0, PEER_TOPK)
        b, rank1 = _top_rows_stable(s1, PEER_TOPK)
    else:
        a = _top_rows(s0, PEER_TOPK)
        b, rank1 = _top_rows(s1, PEER_TOPK, with_rank=True)
    row = lax.broadcasted_iota(jnp.int32, (PEER_TOPK, s0.shape[1]), 0)
    bmat = jnp.zeros((PEER_TOPK, s0.shape[1]), f32)
    for j in range(PEER_TOPK):
        bmat = jnp.where(row == j, b[j], bmat)
    cands = [a[0] + bmat] + [a[i] + bmat[:SUBLANES] for i in range(1, PEER_TOPK)]
    cand = jnp.concatenate(cands, axis=0)
    if stable:
        picked = _top_rows_stable(cand, PEER_TOPK)[1] < float(PEER_TOPK)
    else:
        picked = cand >= _top_rows(cand, PEER_TOPK)[-1]
    picked = jnp.where(picked, 1.0, 0.0)
    cmax = a[0] + b[0]
    z = jnp.sum(picked * jnp.exp(cand - cmax), axis=0, keepdims=True)
    cnt = jnp.zeros_like(s0)
    total = jnp.zeros_like(z)
    lo = 0
    for i in range(PEER_TOPK):
        hi = lo + cands[i].shape[0]
        cnt_i = jnp.sum(picked[lo:hi], axis=0, keepdims=True)
        lo = hi
        total = total + cnt_i
        cnt = jnp.where((rank0 == float(i)) if stable else (s0 == a[i]), cnt_i, cnt)
    p0 = jnp.exp(s0 - a[0]) / z
    p1 = jnp.exp(s1 - b[0])
    if stable:
        return cnt, rank1, p0, p1, None
    k = float(PEER_TOPK)
    n0 = jnp.sum(jnp.where(s0 >= a[-1], 1.0, 0.0), axis=0, keepdims=True)
    n1 = jnp.sum(jnp.where(rank1 < k, 1.0, 0.0), axis=0, keepdims=True)
    tied = jnp.abs(n0 - k) + jnp.abs(n1 - k) + jnp.abs(total - k)
    return cnt, rank1, p0, p1, tied


def _bf16_twice(t):
    bits = pltpu.bitcast(t.astype(bf16).astype(f32), u32) >> 16
    return bits | (bits << 16)


def peer_select(x, wq_b, keys_b, *, tq):
    n, d = x.shape
    nh = PEER_HEADS
    full_spec = pl.BlockSpec((nh, tq // LANES, PEER_NKEYS, LANES), lambda i: (0, i, 0, 0))
    half_spec = pl.BlockSpec((nh, tq // LANES, PEER_NKEYS // 2, LANES), lambda i: (0, i, 0, 0))
    full = jax.ShapeDtypeStruct((nh, n // LANES, PEER_NKEYS, LANES), u32)
    half = jax.ShapeDtypeStruct((nh, n // LANES, PEER_NKEYS // 2, LANES), u32)
    return pl.pallas_call(
        _peer_select_kernel,
        grid=(n // tq,),
        in_specs=[pl.BlockSpec((tq, d), lambda i: (i, 0)),
                  pl.BlockSpec((d, 2 * nh * PEER_DHALF), lambda i: (0, 0)),
                  pl.BlockSpec((nh, 2, PEER_NKEYS, PEER_DHALF), lambda i: (0, 0, 0, 0))],
        out_specs=[full_spec, half_spec, full_spec, half_spec],
        out_shape=[full, half, full, half],
        scratch_shapes=[pltpu.VMEM((tq, 2 * nh * PEER_DHALF), f32)],
        compiler_params=pltpu.CompilerParams(
            dimension_semantics=("arbitrary",), vmem_limit_bytes=VMEM_LIMIT),
        name="peer_select",
    )(x, wq_b, keys_b)


def _peer_main_kernel(xb_ref, x_ref, cnt_ref, rk1_ref, p0_ref, p1_ref, u_ref, vt_ref,
                      g_ref, b_ref, o_ref, acc_ref, *slab_refs, te, tt):
    j = pl.program_id(1)
    n_i0 = te // PEER_NKEYS
    n_slab = tt // LANES
    per_chunk = MXU_COLS // LANES
    at_refs, ga_refs = slab_refs[:n_slab], slab_refs[n_slab:]
    half = PEER_NKEYS // 2
    unpack = lambda words: pltpu.bitcast(words, bf16)

    @pl.when(j == 0)
    def _():
        acc_ref[...] = jnp.zeros_like(acc_ref)

    def act_chunk(c):
        at = lax.dot_general(u_ref[...], xb_ref[c * MXU_COLS:(c + 1) * MXU_COLS, :],
                             (((1,), (1,)), ((), ())), preferred_element_type=f32)
        for k in range(per_chunk):
            at_refs[c * per_chunk + k][...] = at[:, k * LANES:(k + 1) * LANES]

    def gate_slab(s):
        for il in range(n_i0):
            g = jnp.zeros((PEER_NKEYS, LANES), bf16)
            for h in range(PEER_HEADS):
                cntrow = jnp.broadcast_to(cnt_ref[h, s, il:il + 1, :], (half, LANES))
                p0row = jnp.broadcast_to(p0_ref[h, s, il:il + 1, :], (half, LANES))
                sel = unpack(rk1_ref[h, s]) < unpack(cntrow)
                g = g + jnp.where(sel, unpack(p1_ref[h, s]) * unpack(p0row), jnp.zeros((), bf16))
            ga = _gelu_tanh(at_refs[s][il * PEER_NKEYS:(il + 1) * PEER_NKEYS, :]).astype(bf16) * g
            ga_refs[s][il * half:(il + 1) * half, :] = pltpu.bitcast(ga, u32)

    def out_chunk(c):
        ga = jnp.concatenate([unpack(ga_refs[c * per_chunk + k][...]) for k in range(per_chunk)], axis=1)
        lanes = slice(c * MXU_COLS, (c + 1) * MXU_COLS)
        acc_ref[:, lanes] += jnp.dot(vt_ref[...], ga, preferred_element_type=f32)

    n_chunk = tt // MXU_COLS
    act_chunk(0)
    for c in range(n_chunk):
        gate_slab(c * per_chunk)
        if c + 1 < n_chunk:
            act_chunk(c + 1)
        for k in range(1, per_chunk):
            gate_slab(c * per_chunk + k)
        out_chunk(c)

    @pl.when(j == pl.num_programs(1) - 1)
    def _():
        z = ALPHA * x_ref[...] + acc_ref[...].T
        o_ref[...] = _layer_norm_rows(z, g_ref[...], b_ref[...])


def peer_main(x, xb, stats, u_b, vt_b, ln_g, ln_b, *, tt, te):
    n, d = x.shape
    e = u_b.shape[0]
    nh = PEER_HEADS
    cnt, rk1, p0, p1 = stats
    full_spec = pl.BlockSpec((nh, tt // LANES, te // PEER_NKEYS, LANES), lambda i, j: (0, i, j, 0))
    half_spec = pl.BlockSpec((nh, tt // LANES, PEER_NKEYS // 2, LANES), lambda i, j: (0, i, 0, 0))
    return pl.pallas_call(
        functools.partial(_peer_main_kernel, te=te, tt=tt),
        grid=(n // tt, e // te),
        in_specs=[pl.BlockSpec((tt, d), lambda i, j: (i, 0)),
                  pl.BlockSpec((tt, d), lambda i, j: (i, 0)),
                  full_spec, half_spec, full_spec, half_spec,
                  pl.BlockSpec((te, d), lambda i, j: (j, 0)),
                  pl.BlockSpec((d, te), lambda i, j: (0, j)),
                  pl.BlockSpec((1, d), lambda i, j: (0, 0)),
                  pl.BlockSpec((1, d), lambda i, j: (0, 0))],
        out_specs=pl.BlockSpec((tt, d), lambda i, j: (i, 0)),
        out_shape=jax.ShapeDtypeStruct((n, d), f32),
        scratch_shapes=([pltpu.VMEM((d, tt), f32)]
                        + [pltpu.VMEM((te, LANES), f32)] * (tt // LANES)
                        + [pltpu.VMEM((te // 2, LANES), u32)] * (tt // LANES)),
        compiler_params=pltpu.CompilerParams(
            dimension_semantics=("arbitrary", "arbitrary"), vmem_limit_bytes=VMEM_LIMIT),
        name="peer_main",
    )(xb, x, cnt, rk1, p0, p1, u_b, vt_b, ln_g.reshape(1, d), ln_b.reshape(1, d))


def peer_layer(x, w_q, sub_keys, u, v, ln_g, ln_b, *, tq=256, tt=1024, te=1024):
    stats = peer_select(x, w_q.astype(bf16), sub_keys.astype(bf16), tq=tq)
    return peer_main(x, x.astype(bf16), stats, u.astype(bf16), v.T.astype(bf16), ln_g, ln_b, tt=tt, te=te)


def _rglru_kernel(x_ref, win_ref, cw_ref, vec_ref, wg_ref, wout_ref, lng_ref, lnb_ref, o_ref,
                  tail_ref, h_ref, *, tiles_per_seq):
    i = pl.program_id(0)
    w = RG_WIDTH
    x = x_ref[...]
    ts = x.shape[0]

    @pl.when(i % tiles_per_seq == 0)
    def _():
        tail_ref[...] = jnp.zeros_like(tail_ref)
        h_ref[...] = jnp.zeros_like(h_ref)

    conv_b, b_a, b_x, lam = (vec_ref[n:n + 1, :] for n in range(4))
    hin = jnp.dot(x.astype(bf16), win_ref[...], preferred_element_type=f32)
    gate_branch = _gelu_tanh(hin[:, :w])
    hx = hin[:, w:]

    row = lax.broadcasted_iota(jnp.int32, (ts, w), 0)
    row8 = lax.broadcasted_iota(jnp.int32, (SUBLANES, w), 0)
    tail = tail_ref[...]
    xc = hx * cw_ref[RG_CONV - 1:RG_CONV, :] + conv_b
    for s in range(1, RG_CONV):
        rolled = pltpu.roll(hx, s, axis=0)
        head = jnp.where(row8 < s, pltpu.roll(tail, s, axis=0), rolled[:SUBLANES])
        shifted = jnp.concatenate([head, rolled[SUBLANES:]], axis=0)
        xc = xc + shifted * cw_ref[RG_CONV - 1 - s:RG_CONV - s, :]
    tail_ref[...] = hx[ts - SUBLANES:]

    gates = jnp.dot(xc.astype(bf16), wg_ref[...], preferred_element_type=f32)
    r = _sigmoid(gates[:, :w] + b_a)
    ig = _sigmoid(gates[:, w:] + b_x)
    log_a = (-RG_C * _softplus(-lam)) * r
    a = jnp.exp(log_a)
    u = jnp.sqrt(-jnp.tanh(log_a) * (a * a + 1.0)) * (ig * xc)

    s = 1
    while s < ts:
        keep = row >= s
        a_sh = jnp.where(keep, pltpu.roll(a, s, axis=0), 1.0)
        u_sh = jnp.where(keep, pltpu.roll(u, s, axis=0), 0.0)
        u = a * u_sh + u
        a = a * a_sh
        s *= 2
    hs = u + a * h_ref[SUBLANES - 1:SUBLANES, :]
    h_ref[...] = hs[ts - SUBLANES:]

    m = jnp.dot((hs * gate_branch).astype(bf16), wout_ref[...], preferred_element_type=f32)
    o_ref[...] = _layer_norm_rows(ALPHA * x + m, lng_ref[...], lnb_ref[...])


def rglru_layer(x, seq, w_in, conv_w, conv_b, w_a, b_a, w_x, b_x, lam, w_out, ln_g, ln_b, *, ts=256):
    n, d = x.shape
    w = RG_WIDTH
    blockdiag = lambda t: jax.scipy.linalg.block_diag(*t)
    w_gates = jnp.concatenate([blockdiag(w_a), blockdiag(w_x)], axis=1).astype(bf16)
    vecs = jnp.stack([conv_b, b_a.reshape(w), b_x.reshape(w), lam.reshape(w)])
    consts = (w_in.astype(bf16), conv_w, vecs, w_gates, w_out.astype(bf16), ln_g.reshape(1, d), ln_b.reshape(1, d))
    full = lambda arr: pl.BlockSpec(arr.shape, lambda i: (0,) * arr.ndim, pipeline_mode=pl.Buffered(1))
    tok = pl.BlockSpec((ts, d), lambda i: (i, 0))
    return pl.pallas_call(
        functools.partial(_rglru_kernel, tiles_per_seq=seq // ts),
        grid=(n // ts,),
        in_specs=[tok] + [full(c) for c in consts],
        out_specs=tok,
        out_shape=jax.ShapeDtypeStruct((n, d), f32),
        scratch_shapes=[pltpu.VMEM((SUBLANES, w), f32), pltpu.VMEM((SUBLANES, w), f32)],
        compiler_params=pltpu.CompilerParams(
            dimension_semantics=("arbitrary",), vmem_limit_bytes=VMEM_LIMIT),
        name="rglru",
    )(x, *consts)


def _sigmoid(x):
    return 1.0 / (1.0 + jnp.exp(-x))


def _softplus(x):
    return jnp.maximum(x, 0.0) + jnp.log1p(jnp.exp(-jnp.abs(x)))


def _head_sums(t, ones_ref):
    hi = t.astype(bf16)
    lo = (t - hi.astype(f32)).astype(bf16)
    return (jnp.dot(hi, ones_ref[...], preferred_element_type=f32)
            + jnp.dot(lo, ones_ref[...], preferred_element_type=f32))


def _rwkv_proj_kernel(x_ref, xp_ref, mix_ref, wr_ref, wk_ref, wv_ref, w1_ref, w2_ref, a1_ref, a2_ref,
                      g1_ref, g2_ref, vec_ref,
                      r_ref, w_ref, k_ref, v_ref, gate_ref, g_ref, *, tiles_per_seq):
    i = pl.program_id(0)
    x = x_ref[...]
    ts = x.shape[0]
    prev_last = jnp.where(i % tiles_per_seq == 0, 0.0, xp_ref[SUBLANES - 1:SUBLANES, :])
    row = lax.broadcasted_iota(jnp.int32, x.shape, 0)
    xprev = jnp.where(row == 0, prev_last, pltpu.roll(x, 1, axis=0))
    xx = xprev - x

    def mixed(m):
        return (x + xx * mix_ref[m:m + 1, :]).astype(bf16)

    def mm(a, w_ref_):
        return jnp.dot(a, w_ref_[...], preferred_element_type=f32)

    w0, a0 = (vec_ref[n:n + 1, :] for n in range(2))
    lw = mm(jnp.tanh(mm(mixed(1), w1_ref)).astype(bf16), w2_ref)
    w_log = -_softplus(-(w0 + lw)) - 0.5
    r_ref[...] = mm(mixed(0), wr_ref)
    w_ref[...] = jnp.exp(-jnp.exp(w_log))
    k_ref[...] = mm(mixed(2), wk_ref)
    v_ref[...] = mm(mixed(3), wv_ref)
    gate_ref[...] = _sigmoid(a0 + mm(mm(mixed(4), a1_ref).astype(bf16), a2_ref))
    g_ref[...] = mm(_sigmoid(mm(mixed(5), g1_ref)).astype(bf16), g2_ref)


def rwkv_proj(x, seq, mix, w_r, w_k, w_v, w1, w2, a1, a2, g1, g2, vecs, *, ts):
    n, d = x.shape
    full = lambda arr: pl.BlockSpec(arr.shape, lambda i: (0,) * arr.ndim)
    tok = pl.BlockSpec((ts, d), lambda i: (i, 0))
    prev = pl.BlockSpec((SUBLANES, d), lambda i: (jnp.maximum(i * (ts // SUBLANES) - 1, 0), 0))
    weights = (mix, w_r, w_k, w_v, w1, w2, a1, a2, g1, g2, vecs)
    out = jax.ShapeDtypeStruct((n, d), f32)
    return pl.pallas_call(
        functools.partial(_rwkv_proj_kernel, tiles_per_seq=seq // ts),
        grid=(n // ts,),
        in_specs=[tok, prev] + [full(w) for w in weights],
        out_specs=[tok] * 6,
        out_shape=[out] * 6,
        compiler_params=pltpu.CompilerParams(
            dimension_semantics=("arbitrary",), vmem_limit_bytes=VMEM_LIMIT),
        name="rwkv_proj",
    )(x, x, *weights)


def _rwkv_scan_kernel(r_ref, w_ref, k_ref, v_ref, gate_ref, knext_ref, kk_ref, ka_ref, o_ref,
                      s_ref, sa_ref, a_ref, b_ref, km_ref):
    nk = s_ref.shape[0]
    tc = r_ref.shape[0]

    @pl.when(pl.program_id(0) == 0)
    def _():
        s_ref[...] = jnp.zeros_like(s_ref)
        sa_ref[...] = jnp.zeros_like(sa_ref)

    def unit_key(k):
        kk = k * kk_ref[...]
        norm = jnp.sqrt(jnp.sum(kk * kk, axis=-2, keepdims=True))
        return kk / jnp.maximum(norm, RWKV_L2_EPS)

    k_blk = k_ref[...]
    gate = gate_ref[...]
    kk_blk = unit_key(k_blk)
    a_ref[0:tc] = -kk_blk
    a_ref[tc:tc + 1] = -unit_key(knext_ref[...])
    b_ref[...] = kk_blk * gate
    km_ref[...] = k_blk * (1.0 + (gate - 1.0) * ka_ref[...])

    def step(t, sa):
        vt = v_ref[t]
        o = [jnp.zeros(s_ref.shape[1:], f32)] * 2
        nsa = [jnp.zeros(s_ref.shape[1:], f32)] * 2
        for kk in range(nk):
            new = (s_ref[kk] * w_ref[t, kk:kk + 1, :]
                   + (sa * b_ref[t, kk:kk + 1, :] + vt * km_ref[t, kk:kk + 1, :]))
            s_ref[kk] = new
            o[kk % 2] = o[kk % 2] + new * r_ref[t, kk:kk + 1, :]
            nsa[kk % 2] = nsa[kk % 2] + new * a_ref[t + 1, kk:kk + 1, :]
        ot = o[0] + o[1]
        mu = jnp.mean(ot, axis=0, keepdims=True)
        oc = ot - mu
        var = jnp.mean(oc * oc, axis=0, keepdims=True)
        o_ref[t] = oc * lax.rsqrt(var + RWKV_GN_EPS)
        return nsa[0] + nsa[1]

    sa_ref[...] = lax.fori_loop(0, tc, step, sa_ref[...])


def rwkv_scan(r, w, k, v, gate, kk_vec, ka_vec, *, tc):
    s, hd, chains = r.shape
    blk = pl.BlockSpec((tc, hd, chains), lambda i: (i, 0, 0))
    nxt = pl.BlockSpec((1, hd, chains), lambda i: (jnp.minimum((i + 1) * tc, s - 1), 0, 0))
    vec = pl.BlockSpec((hd, chains), lambda i: (0, 0))
    return pl.pallas_call(
        _rwkv_scan_kernel,
        grid=(s // tc,),
        in_specs=[blk] * 5 + [nxt, vec, vec],
        out_specs=blk,
        out_shape=jax.ShapeDtypeStruct((s, hd, chains), f32),
        scratch_shapes=[pltpu.VMEM((hd, hd, chains), f32), pltpu.VMEM((hd, chains), f32),
                        pltpu.VMEM((tc + 1, hd, chains), f32), pltpu.VMEM((tc, hd, chains), f32),
                        pltpu.VMEM((tc, hd, chains), f32)],
        compiler_params=pltpu.CompilerParams(
            dimension_semantics=("arbitrary",), vmem_limit_bytes=VMEM_LIMIT),
        name="rwkv_scan",
    )(r, w, k, v, gate, k, kk_vec, ka_vec)


def _rwkv_out_kernel(x_ref, o_ref, r_ref, k_ref, gate_ref, v_ref, g_ref, vec_ref, ones_ref, wo_ref,
                     lng_ref, lnb_ref, out_ref):
    lnx_g, lnx_b, r_k, k_a = (vec_ref[n:n + 1, :] for n in range(4))
    k_mod = k_ref[...] * (1.0 + (gate_ref[...] - 1.0) * k_a)
    bonus = _head_sums(r_ref[...] * k_mod * r_k, ones_ref) * v_ref[...]
    y = (o_ref[...] * lnx_g + lnx_b + bonus) * g_ref[...]
    m = jnp.dot(y.astype(bf16), wo_ref[...], preferred_element_type=f32)
    out_ref[...] = _layer_norm_rows(ALPHA * x_ref[...] + m, lng_ref[...], lnb_ref[...])


def rwkv_out(x, o, r, k, gate, v, g, vecs, ones, w_o, ln_g, ln_b, *, ts):
    n, d = x.shape
    full = lambda arr: pl.BlockSpec(arr.shape, lambda i: (0,) * arr.ndim)
    tok = pl.BlockSpec((ts, d), lambda i: (i, 0))
    consts = (vecs, ones, w_o, ln_g.reshape(1, d), ln_b.reshape(1, d))
    return pl.pallas_call(
        _rwkv_out_kernel,
        grid=(n // ts,),
        in_specs=[tok] * 7 + [full(c) for c in consts],
        out_specs=tok,
        out_shape=jax.ShapeDtypeStruct((n, d), f32),
        compiler_params=pltpu.CompilerParams(
            dimension_semantics=("arbitrary",), vmem_limit_bytes=VMEM_LIMIT),
        name="rwkv_out",
    )(x, o, r, k, gate, v, g, *consts)


def _head_ones(d, head):
    seg = jnp.arange(d) // head
    return (seg[:, None] == seg[None, :]).astype(bf16)


def rwkv_layer(x, seq, mix, w_r, w_k, w_v, w0, w1, w2, a0, a1, a2, g1, g2, k_k, k_a, r_k, lnx_g, lnx_b, w_o,
               ln_g, ln_b, *, ts=256, tc=16):
    n, d = x.shape
    bsz = n // seq
    nh, hd = RWKV_HEADS, RWKV_HEAD
    ones = _head_ones(d, hd)
    c = lambda w: w.astype(bf16)
    r, w, k, v, gate, g = rwkv_proj(x, seq, mix, c(w_r), c(w_k), c(w_v), c(w1), c(w2), c(a1), c(a2),
                                    c(g1), c(g2), jnp.stack([w0, a0]), ts=ts)
    to_chains = lambda t: t.reshape(bsz, seq, nh, hd).transpose(1, 3, 0, 2).reshape(seq, hd, bsz * nh)
    vec_chains = lambda t: jnp.tile(t.reshape(nh, hd).T, (1, bsz))
    o = rwkv_scan(*(to_chains(t) for t in (r, w, k, v, gate)), vec_chains(k_k), vec_chains(k_a), tc=tc)
    o = o.reshape(seq, hd, bsz, nh).transpose(2, 0, 3, 1).reshape(n, d)
    vecs_out = jnp.stack([lnx_g, lnx_b, r_k.reshape(d), k_a])
    return rwkv_out(x, o, r, k, gate, v, g, vecs_out, ones, c(w_o), ln_g, ln_b, ts=ts)


def kernel(x, rg_w_in, rg_conv_w, rg_conv_b, rg_w_a, rg_b_a, rg_w_x, rg_b_x, rg_lambda, rg_w_out, rw_mix, rw_w_r, rw_w_k, rw_w_v, rw_w0, rw_w1, rw_w2, rw_a0, rw_a1, rw_a2, rw_g1, rw_g2, rw_k_k, rw_k_a, rw_r_k, rw_lnx_g, rw_lnx_b, rw_w_o, peer_w_q, peer_sub_keys, peer_u, peer_v, ln_g, ln_b):
    bsz, s, d = x.shape
    x = x.reshape(bsz * s, d)
    for i in range(DEPTH):
        j = i // 2
        if i % 2 == 0:
            x1 = rglru_layer(x, s, rg_w_in[j], rg_conv_w[j], rg_conv_b[j], rg_w_a[j], rg_b_a[j],
                             rg_w_x[j], rg_b_x[j], rg_lambda[j], rg_w_out[j], ln_g[i, 0], ln_b[i, 0])
        else:
            x1 = rwkv_layer(x, s, rw_mix[j], rw_w_r[j], rw_w_k[j], rw_w_v[j], rw_w0[j], rw_w1[j],
                            rw_w2[j], rw_a0[j], rw_a1[j], rw_a2[j], rw_g1[j], rw_g2[j],
                            rw_k_k[j], rw_k_a[j], rw_r_k[j], rw_lnx_g[j], rw_lnx_b[j], rw_w_o[j],
                            ln_g[i, 0], ln_b[i, 0])
        x = peer_layer(x1, peer_w_q[i], peer_sub_keys[i], peer_u[i], peer_v[i], ln_g[i, 1], ln_b[i, 1])
    return x.reshape(bsz, s, d)
```

```python
import functools
import math
import jax, jax.numpy as jnp
from jax import lax
from jax.experimental import pallas as pl
from jax.experimental.pallas import tpu as pltpu

D_MODEL = 1024
DEPTH = 2
RG_WIDTH = 1408
RG_HEADS = 16
RG_BLOCK = 88
RG_CONV = 4
RG_C = 8.0
RWKV_HEAD = 64
RWKV_HEADS = 16
RWKV_GN_EPS = 64e-5
RWKV_L2_EPS = 1e-12
PEER_HEADS = 8
PEER_NKEYS = 128
PEER_DHALF = 128
PEER_TOPK = 16
PEER_BLOCK = 128
ALPHA = (2 * DEPTH) ** 0.25
LN_EPS = 1e-5

LANES = 128
SUBLANES = 8
MXU_COLS = 256
VMEM_LIMIT = 56 << 20

f32 = jnp.float32
bf16 = jnp.bfloat16
u32 = jnp.uint32


def _gelu_tanh(x):
    return 0.5 * x * (1.0 + jnp.tanh(math.sqrt(2.0 / math.pi) * (x + 0.044715 * (x * x * x))))


def _layer_norm_rows(z, g, b):
    mu = jnp.mean(z, axis=-1, keepdims=True)
    zc = z - mu
    var = jnp.mean(zc * zc, axis=-1, keepdims=True)
    return zc * lax.rsqrt(var + LN_EPS) * g + b


def _top_rows(s, k, with_rank=False):
    rows = []
    rank = jnp.full(s.shape, float(k), f32)
    for j in range(k):
        m = jnp.max(s, axis=0, keepdims=True)
        rows.append(m)
        hit = s == m
        if with_rank:
            rank = jnp.where(hit, float(j), rank)
        s = jnp.where(hit, -jnp.inf, s)
    return (rows, rank) if with_rank else rows


def _peer_select_kernel(x_ref, wq_ref, keys_ref, cnt_ref, rk1_ref, p0_ref, p1_ref, q_ref):
    h = pl.program_id(1)

    @pl.when(h == 0)
    def _():
        q_ref[...] = jnp.dot(x_ref[...].astype(bf16), wq_ref[...], preferred_element_type=f32)

    def scores(p):
        off = pl.multiple_of((2 * h + p) * PEER_DHALF, PEER_DHALF)
        qhp = q_ref[:, pl.ds(off, PEER_DHALF)].astype(bf16)
        return lax.dot_general(keys_ref[0, p], qhp, (((1,), (1,)), ((), ())),
                               preferred_element_type=f32)

    def select_all(stable):
        s0_all = scores(0)
        s1_all = scores(1)
        tied = 0.0
        for c in range(s0_all.shape[1] // LANES):
            lanes = slice(c * LANES, (c + 1) * LANES)
            cnt, rank1, p0, p1, tied_c = _select_slab(s0_all[:, lanes], s1_all[:, lanes], stable=stable)
            cnt_ref[0, c] = _bf16_twice(cnt)
            p0_ref[0, c] = _bf16_twice(p0)
            rk1_ref[0, c] = pltpu.bitcast(rank1.astype(bf16), u32)
            p1_ref[0, c] = pltpu.bitcast(p1.astype(bf16), u32)
            if not stable:
                tied = jnp.maximum(tied, jnp.max(tied_c))
        return tied

    tied = select_all(stable=False)

    @pl.when(tied > 0.0)
    def _():
        select_all(stable=True)


def _top_rows_stable(s, k):
    rows = []
    idx = lax.broadcasted_iota(jnp.int32, s.shape, 0)
    rank = jnp.full(s.shape, float(k), f32)
    for j in range(k):
        m = jnp.max(s, axis=0, keepdims=True)
        rows.append(m)
        first = jnp.min(jnp.where(s == m, idx, s.shape[0]), axis=0, keepdims=True)
        pick = idx == first
        rank = jnp.where(pick, float(j), rank)
        s = jnp.where(pick, -jnp.inf, s)
    return rows, rank


def _select_slab(s0, s1, *, stable):
    if stable:
        a, rank0 = _top_rows_stable(s0, PEER_TOPK)
        b, rank1 = _top_rows_stable(s1, PEER_TOPK)
    else:
        a = _top_rows(s0, PEER_TOPK)
        b, rank1 = _top_rows(s1, PEER_TOPK, with_rank=True)
    row = lax.broadcasted_iota(jnp.int32, (PEER_TOPK, s0.shape[1]), 0)
    bmat = jnp.zeros((PEER_TOPK, s0.shape[1]), f32)
    for j in range(PEER_TOPK):
        bmat = jnp.where(row == j, b[j], bmat)
    cands = [a[0] + bmat] + [a[i] + bmat[:SUBLANES] for i in range(1, PEER_TOPK)]
    cand = jnp.concatenate(cands, axis=0)
    if stable:
        picked = _top_rows_stable(cand, PEER_TOPK)[1] < float(PEER_TOPK)
    else:
        picked = cand >= _top_rows(cand, PEER_TOPK)[-1]
    picked = jnp.where(picked, 1.0, 0.0)
    cmax = a[0] + b[0]
    z = jnp.sum(picked * jnp.exp(cand - cmax), axis=0, keepdims=True)
    cnt = jnp.zeros_like(s0)
    total = jnp.zeros_like(z)
    lo = 0
    for i in range(PEER_TOPK):
        hi = lo + cands[i].shape[0]
        cnt_i = jnp.sum(picked[lo:hi], axis=0, keepdims=True)
        lo = hi
        total = total + cnt_i
        cnt = jnp.where((rank0 == float(i)) if stable else (s0 == a[i]), cnt_i, cnt)
    p0 = jnp.exp(s0 - a[0]) / z
    p1 = jnp.exp(s1 - b[0])
    if stable:
        return cnt, rank1, p0, p1, None
    k = float(PEER_TOPK)
    n0 = jnp.sum(jnp.where(s0 >= a[-1], 1.0, 0.0), axis=0, keepdims=True)
    n1 = jnp.sum(jnp.where(rank1 < k, 1.0, 0.0), axis=0, keepdims=True)
    tied = jnp.abs(n0 - k) + jnp.abs(n1 - k) + jnp.abs(total - k)
    return cnt, rank1, p0, p1, tied


def _bf16_twice(t):
    bits = pltpu.bitcast(t.astype(bf16).astype(f32), u32) >> 16
    return bits | (bits << 16)


def peer_select(x, wq_b, keys_b, *, tq):
    n, d = x.shape
    nh = PEER_HEADS
    full_spec = pl.BlockSpec((1, tq // LANES, PEER_NKEYS, LANES), lambda i, h: (h, i, 0, 0))
    half_spec = pl.BlockSpec((1, tq // LANES, PEER_NKEYS // 2, LANES), lambda i, h: (h, i, 0, 0))
    full = jax.ShapeDtypeStruct((nh, n // LANES, PEER_NKEYS, LANES), u32)
    half = jax.ShapeDtypeStruct((nh, n // LANES, PEER_NKEYS // 2, LANES), u32)
    return pl.pallas_call(
        _peer_select_kernel,
        grid=(n // tq, nh),
        in_specs=[pl.BlockSpec((tq, d), lambda i, h: (i, 0)),
                  pl.BlockSpec((d, 2 * nh * PEER_DHALF), lambda i, h: (0, 0)),
                  pl.BlockSpec((1, 2, PEER_NKEYS, PEER_DHALF), lambda i, h: (h, 0, 0, 0))],
        out_specs=[full_spec, half_spec, full_spec, half_spec],
        out_shape=[full, half, full, half],
        scratch_shapes=[pltpu.VMEM((tq, 2 * nh * PEER_DHALF), f32)],
        compiler_params=pltpu.CompilerParams(
            dimension_semantics=("arbitrary", "arbitrary"), vmem_limit_bytes=VMEM_LIMIT),
        name="peer_select",
    )(x, wq_b, keys_b)


def _peer_main_kernel(xb_ref, x_ref, cnt_ref, rk1_ref, p0_ref, p1_ref, u_ref, vt_ref,
                      g_ref, b_ref, o_ref, acc_ref, *slab_refs, te, tt):
    j = pl.program_id(1)
    n_i0 = te // PEER_NKEYS
    n_slab = tt // LANES
    per_chunk = MXU_COLS // LANES
    at_refs, ga_refs = slab_refs[:n_slab], slab_refs[n_slab:]
    half = PEER_NKEYS // 2
    unpack = lambda words: pltpu.bitcast(words, bf16)

    @pl.when(j == 0)
    def _():
        acc_ref[...] = jnp.zeros_like(acc_ref)

    def act_chunk(c):
        at = lax.dot_general(u_ref[...], xb_ref[c * MXU_COLS:(c + 1) * MXU_COLS, :],
                             (((1,), (1,)), ((), ())), preferred_element_type=f32)
        for k in range(per_chunk):
            at_refs[c * per_chunk + k][...] = at[:, k * LANES:(k + 1) * LANES]

    def gate_slab(s):
        for il in range(n_i0):
            g = jnp.zeros((PEER_NKEYS, LANES), bf16)
            for h in range(PEER_HEADS):
                cntrow = jnp.broadcast_to(cnt_ref[h, s, il:il + 1, :], (half, LANES))
                p0row = jnp.broadcast_to(p0_ref[h, s, il:il + 1, :], (half, LANES))
                sel = unpack(rk1_ref[h, s]) < unpack(cntrow)
                g = g + jnp.where(sel, unpack(p1_ref[h, s]) * unpack(p0row), jnp.zeros((), bf16))
            ga = _gelu_tanh(at_refs[s][il * PEER_NKEYS:(il + 1) * PEER_NKEYS, :].astype(bf16)) * g
            ga_refs[s][il * half:(il + 1) * half, :] = pltpu.bitcast(ga, u32)

    def out_chunk(c):
        ga = jnp.concatenate([unpack(ga_refs[c * per_chunk + k][...]) for k in range(per_chunk)], axis=1)
        lanes = slice(c * MXU_COLS, (c + 1) * MXU_COLS)
        acc_ref[:, lanes] += jnp.dot(vt_ref[...], ga, preferred_element_type=f32)

    n_chunk = tt // MXU_COLS
    act_chunk(0)
    for c in range(n_chunk):
        gate_slab(c * per_chunk)
        if c + 1 < n_chunk:
            act_chunk(c + 1)
        for k in range(1, per_chunk):
            gate_slab(c * per_chunk + k)
        out_chunk(c)

    @pl.when(j == pl.num_programs(1) - 1)
    def _():
        z = ALPHA * x_ref[...] + acc_ref[...].T
        o_ref[...] = _layer_norm_rows(z, g_ref[...], b_ref[...])


def peer_main(x, xb, stats, u_b, vt_b, ln_g, ln_b, *, tt, te):
    n, d = x.shape
    e = u_b.shape[0]
    nh = PEER_HEADS
    cnt, rk1, p0, p1 = stats
    full_spec = pl.BlockSpec((nh, tt // LANES, te // PEER_NKEYS, LANES), lambda i, j: (0, i, j, 0))
    half_spec = pl.BlockSpec((nh, tt // LANES, PEER_NKEYS // 2, LANES), lambda i, j: (0, i, 0, 0))
    return pl.pallas_call(
        functools.partial(_peer_main_kernel, te=te, tt=tt),
        grid=(n // tt, e // te),
        in_specs=[pl.BlockSpec((tt, d), lambda i, j: (i, 0)),
                  pl.BlockSpec((tt, d), lambda i, j: (i, 0)),
                  full_spec, half_spec, full_spec, half_spec,
                  pl.BlockSpec((te, d), lambda i, j: (j, 0)),
                  pl.BlockSpec((d, te), lambda i, j: (0, j)),
                  pl.BlockSpec((1, d), lambda i, j: (0, 0)),
                  pl.BlockSpec((1, d), lambda i, j: (0, 0))],
        out_specs=pl.BlockSpec((tt, d), lambda i, j: (i, 0)),
        out_shape=jax.ShapeDtypeStruct((n, d), f32),
        scratch_shapes=([pltpu.VMEM((d, tt), f32)]
                        + [pltpu.VMEM((te, LANES), f32)] * (tt // LANES)
                        + [pltpu.VMEM((te // 2, LANES), u32)] * (tt // LANES)),
        compiler_params=pltpu.CompilerParams(
            dimension_semantics=("arbitrary", "arbitrary"), vmem_limit_bytes=VMEM_LIMIT),
        name="peer_main",
    )(xb, x, cnt, rk1, p0, p1, u_b, vt_b, ln_g.reshape(1, d), ln_b.reshape(1, d))


def peer_layer(x, w_q, sub_keys, u, v, ln_g, ln_b, *, tq=256, tt=1024, te=1024):
    stats = peer_select(x, w_q.astype(bf16), sub_keys.astype(bf16), tq=tq)
    return peer_main(x, x.astype(bf16), stats, u.astype(bf16), v.T.astype(bf16), ln_g, ln_b, tt=tt, te=te)


def _rglru_kernel(x_ref, win_ref, cw_ref, vec_ref, wg_ref, wout_ref, lng_ref, lnb_ref, o_ref,
                  tail_ref, h_ref, *, tiles_per_seq):
    i = pl.program_id(0)
    w = RG_WIDTH
    x = x_ref[...]
    ts = x.shape[0]

    @pl.when(i % tiles_per_seq == 0)
    def _():
        tail_ref[...] = jnp.zeros_like(tail_ref)
        h_ref[...] = jnp.zeros_like(h_ref)

    conv_b, b_a, b_x, lam = (vec_ref[n:n + 1, :] for n in range(4))
    hin = jnp.dot(x.astype(bf16), win_ref[...], preferred_element_type=f32)
    gate_branch = _gelu_tanh(hin[:, :w])
    hx = hin[:, w:]

    row = lax.broadcasted_iota(jnp.int32, (ts, w), 0)
    row8 = lax.broadcasted_iota(jnp.int32, (SUBLANES, w), 0)
    tail = tail_ref[...]
    xc = hx * cw_ref[RG_CONV - 1:RG_CONV, :] + conv_b
    for s in range(1, RG_CONV):
        rolled = pltpu.roll(hx, s, axis=0)
        head = jnp.where(row8 < s, pltpu.roll(tail, s, axis=0), rolled[:SUBLANES])
        shifted = jnp.concatenate([head, rolled[SUBLANES:]], axis=0)
        xc = xc + shifted * cw_ref[RG_CONV - 1 - s:RG_CONV - s, :]
    tail_ref[...] = hx[ts - SUBLANES:]

    gates = jnp.dot(xc.astype(bf16), wg_ref[...], preferred_element_type=f32)
    r = _sigmoid(gates[:, :w] + b_a)
    ig = _sigmoid(gates[:, w:] + b_x)
    log_a = (-RG_C * _softplus(-lam)) * r
    a = jnp.exp(log_a)
    u = jnp.sqrt(-jnp.tanh(log_a) * (a * a + 1.0)) * (ig * xc)

    s = 1
    while s < ts:
        keep = row >= s
        a_sh = jnp.where(keep, pltpu.roll(a, s, axis=0), 1.0)
        u_sh = jnp.where(keep, pltpu.roll(u, s, axis=0), 0.0)
        u = a * u_sh + u
        a = a * a_sh
        s *= 2
    hs = u + a * h_ref[SUBLANES - 1:SUBLANES, :]
    h_ref[...] = hs[ts - SUBLANES:]

    m = jnp.dot((hs * gate_branch).astype(bf16), wout_ref[...], preferred_element_type=f32)
    o_ref[...] = _layer_norm_rows(ALPHA * x + m, lng_ref[...], lnb_ref[...])


def rglru_layer(x, seq, w_in, conv_w, conv_b, w_a, b_a, w_x, b_x, lam, w_out, ln_g, ln_b, *, ts=256):
    n, d = x.shape
    w = RG_WIDTH
    blockdiag = lambda t: jax.scipy.linalg.block_diag(*t)
    w_gates = jnp.concatenate([blockdiag(w_a), blockdiag(w_x)], axis=1).astype(bf16)
    vecs = jnp.stack([conv_b, b_a.reshape(w), b_x.reshape(w), lam.reshape(w)])
    consts = (w_in.astype(bf16), conv_w, vecs, w_gates, w_out.astype(bf16), ln_g.reshape(1, d), ln_b.reshape(1, d))
    full = lambda arr: pl.BlockSpec(arr.shape, lambda i: (0,) * arr.ndim, pipeline_mode=pl.Buffered(1))
    tok = pl.BlockSpec((ts, d), lambda i: (i, 0))
    return pl.pallas_call(
        functools.partial(_rglru_kernel, tiles_per_seq=seq // ts),
        grid=(n // ts,),
        in_specs=[tok] + [full(c) for c in consts],
        out_specs=tok,
        out_shape=jax.ShapeDtypeStruct((n, d), f32),
        scratch_shapes=[pltpu.VMEM((SUBLANES, w), f32), pltpu.VMEM((SUBLANES, w), f32)],
        compiler_params=pltpu.CompilerParams(
            dimension_semantics=("arbitrary",), vmem_limit_bytes=VMEM_LIMIT),
        name="rglru",
    )(x, *consts)


def _sigmoid(x):
    return 1.0 / (1.0 + jnp.exp(-x))


def _softplus(x):
    return jnp.maximum(x, 0.0) + jnp.log1p(jnp.exp(-jnp.abs(x)))


def _head_sums(t, ones_ref):
    hi = t.astype(bf16)
    lo = (t - hi.astype(f32)).astype(bf16)
    return (jnp.dot(hi, ones_ref[...], preferred_element_type=f32)
            + jnp.dot(lo, ones_ref[...], preferred_element_type=f32))


def _rwkv_proj_kernel(x_ref, xp_ref, mix_ref, wr_ref, wk_ref, wv_ref, w1_ref, w2_ref, a1_ref, a2_ref,
                      g1_ref, g2_ref, vec_ref,
                      r_ref, w_ref, k_ref, v_ref, gate_ref, g_ref, *, tiles_per_seq):
    i = pl.program_id(0)
    x = x_ref[...]
    ts = x.shape[0]
    prev_last = jnp.where(i % tiles_per_seq == 0, 0.0, xp_ref[SUBLANES - 1:SUBLANES, :])
    row = lax.broadcasted_iota(jnp.int32, x.shape, 0)
    xprev = jnp.where(row == 0, prev_last, pltpu.roll(x, 1, axis=0))
    xx = xprev - x

    def mixed(m):
        return (x + xx * mix_ref[m:m + 1, :]).astype(bf16)

    def mm(a, w_ref_):
        return jnp.dot(a, w_ref_[...], preferred_element_type=f32)

    w0, a0 = (vec_ref[n:n + 1, :] for n in range(2))
    lw = mm(jnp.tanh(mm(mixed(1), w1_ref)).astype(bf16), w2_ref)
    w_log = -_softplus(-(w0 + lw)) - 0.5
    r_ref[...] = mm(mixed(0), wr_ref)
    w_ref[...] = jnp.exp(-jnp.exp(w_log))
    k_ref[...] = mm(mixed(2), wk_ref)
    v_ref[...] = mm(mixed(3), wv_ref)
    gate_ref[...] = _sigmoid(a0 + mm(mm(mixed(4), a1_ref).astype(bf16), a2_ref))
    g_ref[...] = mm(_sigmoid(mm(mixed(5), g1_ref)).astype(bf16), g2_ref)


def rwkv_proj(x, seq, mix, w_r, w_k, w_v, w1, w2, a1, a2, g1, g2, vecs, *, ts):
    n, d = x.shape
    full = lambda arr: pl.BlockSpec(arr.shape, lambda i: (0,) * arr.ndim)
    tok = pl.BlockSpec((ts, d), lambda i: (i, 0))
    prev = pl.BlockSpec((SUBLANES, d), lambda i: (jnp.maximum(i * (ts // SUBLANES) - 1, 0), 0))
    weights = (mix, w_r, w_k, w_v, w1, w2, a1, a2, g1, g2, vecs)
    out = jax.ShapeDtypeStruct((n, d), f32)
    return pl.pallas_call(
        functools.partial(_rwkv_proj_kernel, tiles_per_seq=seq // ts),
        grid=(n // ts,),
        in_specs=[tok, prev] + [full(w) for w in weights],
        out_specs=[tok] * 6,
        out_shape=[out] * 6,
        compiler_params=pltpu.CompilerParams(
            dimension_semantics=("arbitrary",), vmem_limit_bytes=VMEM_LIMIT),
        name="rwkv_proj",
    )(x, x, *weights)


def _rwkv_scan_kernel(r_ref, w_ref, k_ref, v_ref, gate_ref, knext_ref, kk_ref, ka_ref, o_ref,
                      s_ref, sa_ref, a_ref, b_ref, km_ref):
    nk = s_ref.shape[0]
    tc = r_ref.shape[0]

    @pl.when(pl.program_id(0) == 0)
    def _():
        s_ref[...] = jnp.zeros_like(s_ref)
        sa_ref[...] = jnp.zeros_like(sa_ref)

    def unit_key(k):
        kk = k * kk_ref[...]
        norm = jnp.sqrt(jnp.sum(kk * kk, axis=-2, keepdims=True))
        return kk / jnp.maximum(norm, RWKV_L2_EPS)

    k_blk = k_ref[...]
    gate = gate_ref[...]
    kk_blk = unit_key(k_blk)
    a_ref[0:tc] = -kk_blk
    a_ref[tc:tc + 1] = -unit_key(knext_ref[...])
    b_ref[...] = kk_blk * gate
    km_ref[...] = k_blk * (1.0 + (gate - 1.0) * ka_ref[...])

    def step(t, sa):
        vt = v_ref[t]
        o = [jnp.zeros(s_ref.shape[1:], f32)] * 2
        nsa = [jnp.zeros(s_ref.shape[1:], f32)] * 2
        for kk in range(nk):
            new = (s_ref[kk] * w_ref[t, kk:kk + 1, :]
                   + (sa * b_ref[t, kk:kk + 1, :] + vt * km_ref[t, kk:kk + 1, :]))
            s_ref[kk] = new
            o[kk % 2] = o[kk % 2] + new * r_ref[t, kk:kk + 1, :]
            nsa[kk % 2] = nsa[kk % 2] + new * a_ref[t + 1, kk:kk + 1, :]
        ot = o[0] + o[1]
        mu = jnp.mean(ot, axis=0, keepdims=True)
        oc = ot - mu
        var = jnp.mean(oc * oc, axis=0, keepdims=True)
        o_ref[t] = oc * lax.rsqrt(var + RWKV_GN_EPS)
        return nsa[0] + nsa[1]

    sa_ref[...] = lax.fori_loop(0, tc, step, sa_ref[...])


def rwkv_scan(r, w, k, v, gate, kk_vec, ka_vec, *, tc):
    s, hd, chains = r.shape
    blk = pl.BlockSpec((tc, hd, chains), lambda i: (i, 0, 0))
    nxt = pl.BlockSpec((1, hd, chains), lambda i: (jnp.minimum((i + 1) * tc, s - 1), 0, 0))
    vec = pl.BlockSpec((hd, chains), lambda i: (0, 0))
    return pl.pallas_call(
        _rwkv_scan_kernel,
        grid=(s // tc,),
        in_specs=[blk] * 5 + [nxt, vec, vec],
        out_specs=blk,
        out_shape=jax.ShapeDtypeStruct((s, hd, chains), f32),
        scratch_shapes=[pltpu.VMEM((hd, hd, chains), f32), pltpu.VMEM((hd, chains), f32),
                        pltpu.VMEM((tc + 1, hd, chains), f32), pltpu.VMEM((tc, hd, chains), f32),
                        pltpu.VMEM((tc, hd, chains), f32)],
        compiler_params=pltpu.CompilerParams(
            dimension_semantics=("arbitrary",), vmem_limit_bytes=VMEM_LIMIT),
        name="rwkv_scan",
    )(r, w, k, v, gate, k, kk_vec, ka_vec)


def _rwkv_out_kernel(x_ref, o_ref, r_ref, k_ref, gate_ref, v_ref, g_ref, vec_ref, ones_ref, wo_ref,
                     lng_ref, lnb_ref, out_ref):
    lnx_g, lnx_b, r_k, k_a = (vec_ref[n:n + 1, :] for n in range(4))
    k_mod = k_ref[...] * (1.0 + (gate_ref[...] - 1.0) * k_a)
    bonus = _head_sums(r_ref[...] * k_mod * r_k, ones_ref) * v_ref[...]
    y = (o_ref[...] * lnx_g + lnx_b + bonus) * g_ref[...]
    m = jnp.dot(y.astype(bf16), wo_ref[...], preferred_element_type=f32)
    out_ref[...] = _layer_norm_rows(ALPHA * x_ref[...] + m, lng_ref[...], lnb_ref[...])


def rwkv_out(x, o, r, k, gate, v, g, vecs, ones, w_o, ln_g, ln_b, *, ts):
    n, d = x.shape
    full = lambda arr: pl.BlockSpec(arr.shape, lambda i: (0,) * arr.ndim)
    tok = pl.BlockSpec((ts, d), lambda i: (i, 0))
    consts = (vecs, ones, w_o, ln_g.reshape(1, d), ln_b.reshape(1, d))
    return pl.pallas_call(
        _rwkv_out_kernel,
        grid=(n // ts,),
        in_specs=[tok] * 7 + [full(c) for c in consts],
        out_specs=tok,
        out_shape=jax.ShapeDtypeStruct((n, d), f32),
        compiler_params=pltpu.CompilerParams(
            dimension_semantics=("arbitrary",), vmem_limit_bytes=VMEM_LIMIT),
        name="rwkv_out",
    )(x, o, r, k, gate, v, g, *consts)


def _head_ones(d, head):
    seg = jnp.arange(d) // head
    return (seg[:, None] == seg[None, :]).astype(bf16)


def rwkv_layer(x, seq, mix, w_r, w_k, w_v, w0, w1, w2, a0, a1, a2, g1, g2, k_k, k_a, r_k, lnx_g, lnx_b, w_o,
               ln_g, ln_b, *, ts=256, tc=16):
    n, d = x.shape
    bsz = n // seq
    nh, hd = RWKV_HEADS, RWKV_HEAD
    ones = _head_ones(d, hd)
    c = lambda w: w.astype(bf16)
    r, w, k, v, gate, g = rwkv_proj(x, seq, mix, c(w_r), c(w_k), c(w_v), c(w1), c(w2), c(a1), c(a2),
                                    c(g1), c(g2), jnp.stack([w0, a0]), ts=ts)
    to_chains = lambda t: t.reshape(bsz, seq, nh, hd).transpose(1, 3, 0, 2).reshape(seq, hd, bsz * nh)
    vec_chains = lambda t: jnp.tile(t.reshape(nh, hd).T, (1, bsz))
    o = rwkv_scan(*(to_chains(t) for t in (r, w, k, v, gate)), vec_chains(k_k), vec_chains(k_a), tc=tc)
    o = o.reshape(seq, hd, bsz, nh).transpose(2, 0, 3, 1).reshape(n, d)
    vecs_out = jnp.stack([lnx_g, lnx_b, r_k.reshape(d), k_a])
    return rwkv_out(x, o, r, k, gate, v, g, vecs_out, ones, c(w_o), ln_g, ln_b, ts=ts)


def kernel(x, rg_w_in, rg_conv_w, rg_conv_b, rg_w_a, rg_b_a, rg_w_x, rg_b_x, rg_lambda, rg_w_out, rw_mix, rw_w_r, rw_w_k, rw_w_v, rw_w0, rw_w1, rw_w2, rw_a0, rw_a1, rw_a2, rw_g1, rw_g2, rw_k_k, rw_k_a, rw_r_k, rw_lnx_g, rw_lnx_b, rw_w_o, peer_w_q, peer_sub_keys, peer_u, peer_v, ln_g, ln_b):
    bsz, s, d = x.shape
    x = x.reshape(bsz * s, d)
    for i in range(DEPTH):
        j = i // 2
        if i % 2 == 0:
            x1 = rglru_layer(x, s, rg_w_in[j], rg_conv_w[j], rg_conv_b[j], rg_w_a[j], rg_b_a[j],
                             rg_w_x[j], rg_b_x[j], rg_lambda[j], rg_w_out[j], ln_g[i, 0], ln_b[i, 0])
        else:
            x1 = rwkv_layer(x, s, rw_mix[j], rw_w_r[j], rw_w_k[j], rw_w_v[j], rw_w0[j], rw_w1[j],
                            rw_w2[j], rw_a0[j], rw_a1[j], rw_a2[j], rw_g1[j], rw_g2[j],
                            rw_k_k[j], rw_k_a[j], rw_r_k[j], rw_lnx_g[j], rw_lnx_b[j], rw_w_o[j],
                            ln_g[i, 0], ln_b[i, 0])
        x = peer_layer(x1, peer_w_q[i], peer_sub_keys[i], peer_u[i], peer_v[i], ln_g[i, 1], ln_b[i, 1])
    return x.reshape(bsz, s, d)
```

```python
import functools
import math
import jax, jax.numpy as jnp
from jax import lax
from jax.experimental import pallas as pl
from jax.experimental.pallas import tpu as pltpu

D_MODEL = 1024
DEPTH = 2
RG_WIDTH = 1408
RG_HEADS = 16
RG_BLOCK = 88
RG_CONV = 4
RG_C = 8.0
RWKV_HEAD = 64
RWKV_HEADS = 16
RWKV_GN_EPS = 64e-5
RWKV_L2_EPS = 1e-12
PEER_HEADS = 8
PEER_NKEYS = 128
PEER_DHALF = 128
PEER_TOPK = 16
PEER_BLOCK = 128
ALPHA = (2 * DEPTH) ** 0.25
LN_EPS = 1e-5

LANES = 128
SUBLANES = 8
MXU_COLS = 256
VMEM_LIMIT = 56 << 20

f32 = jnp.float32
bf16 = jnp.bfloat16
u32 = jnp.uint32


def _gelu_tanh(x):
    return 0.5 * x * (1.0 + jnp.tanh(math.sqrt(2.0 / math.pi) * (x + 0.044715 * (x * x * x))))


def _layer_norm_rows(z, g, b):
    mu = jnp.mean(z, axis=-1, keepdims=True)
    zc = z - mu
    var = jnp.mean(zc * zc, axis=-1, keepdims=True)
    return zc * lax.rsqrt(var + LN_EPS) * g + b


def _top_rows(s, k, with_rank=False):
    rows = []
    rank = jnp.full(s.shape, float(k), f32)
    for j in range(k):
        m = jnp.max(s, axis=0, keepdims=True)
        rows.append(m)
        hit = s == m
        if with_rank:
            rank = jnp.where(hit, float(j), rank)
        s = jnp.where(hit, -jnp.inf, s)
    return (rows, rank) if with_rank else rows


def _peer_select_kernel(x_ref, wq_ref, keys_ref, cnt_ref, rk1_ref, p0_ref, p1_ref, q_ref):
    h = pl.program_id(1)

    @pl.when(h == 0)
    def _():
        q_ref[...] = jnp.dot(x_ref[...].astype(bf16), wq_ref[...], preferred_element_type=f32)

    def scores(p):
        off = pl.multiple_of((2 * h + p) * PEER_DHALF, PEER_DHALF)
        qhp = q_ref[:, pl.ds(off, PEER_DHALF)].astype(bf16)
        return lax.dot_general(keys_ref[0, p], qhp, (((1,), (1,)), ((), ())),
                               preferred_element_type=f32)

    def select_all(stable):
        s0_all = scores(0)
        s1_all = scores(1)
        tied = 0.0
        for c in range(s0_all.shape[1] // LANES):
            lanes = slice(c * LANES, (c + 1) * LANES)
            cnt, rank1, p0, p1, tied_c = _select_slab(s0_all[:, lanes], s1_all[:, lanes], stable=stable)
            cnt_ref[0, c] = _bf16_twice(cnt)
            p0_ref[0, c] = _bf16_twice(p0)
            rk1_ref[0, c] = pltpu.bitcast(rank1.astype(bf16), u32)
            p1_ref[0, c] = pltpu.bitcast(p1.astype(bf16), u32)
            if not stable:
                tied = jnp.maximum(tied, jnp.max(tied_c))
        return tied

    tied = select_all(stable=False)

    @pl.when(tied > 0.0)
    def _():
        select_all(stable=True)


def _top_rows_stable(s, k):
    rows = []
    idx = lax.broadcasted_iota(jnp.int32, s.shape, 0)
    rank = jnp.full(s.shape, float(k), f32)
    for j in range(k):
        m = jnp.max(s, axis=0, keepdims=True)
        rows.append(m)
        first = jnp.min(jnp.where(s == m, idx, s.shape[0]), axis=0, keepdims=True)
        pick = idx == first
        rank = jnp.where(pick, float(j), rank)
        s = jnp.where(pick, -jnp.inf, s)
    return rows, rank


def _select_slab(s0, s1, *, stable):
    if stable:
        a, rank0 = _top_rows_stable(s0, PEER_TOPK)
        b, rank1 = _top_rows_stable(s1, PEER_TOPK)
    else:
        a = _top_rows(s0, PEER_TOPK)
        b, rank1 = _top_rows(s1, PEER_TOPK, with_rank=True)
    row = lax.broadcasted_iota(jnp.int32, (PEER_TOPK, s0.shape[1]), 0)
    bmat = jnp.zeros((PEER_TOPK, s0.shape[1]), f32)
    for j in range(PEER_TOPK):
        bmat = jnp.where(row == j, b[j], bmat)
    cands = [a[0] + bmat] + [a[i] + bmat[:SUBLANES] for i in range(1, PEER_TOPK)]
    cand = jnp.concatenate(cands, axis=0)
    if stable:
        picked = _top_rows_stable(cand, PEER_TOPK)[1] < float(PEER_TOPK)
    else:
        picked = cand >= _top_rows(cand, PEER_TOPK)[-1]
    picked = jnp.where(picked, 1.0, 0.0)
    cmax = a[0] + b[0]
    z = jnp.sum(picked * jnp.exp(cand - cmax), axis=0, keepdims=True)
    cnt = jnp.zeros_like(s0)
    total = jnp.zeros_like(z)
    lo = 0
    for i in range(PEER_TOPK):
        hi = lo + cands[i].shape[0]
        cnt_i = jnp.sum(picked[lo:hi], axis=0, keepdims=True)
        lo = hi
        total = total + cnt_i
        cnt = jnp.where((rank0 == float(i)) if stable else (s0 == a[i]), cnt_i, cnt)
    p0 = jnp.exp(s0 - a[0]) / z
    p1 = jnp.exp(s1 - b[0])
    if stable:
        return cnt, rank1, p0, p1, None
    k = float(PEER_TOPK)
    n0 = jnp.sum(jnp.where(s0 >= a[-1], 1.0, 0.0), axis=0, keepdims=True)
    n1 = jnp.sum(jnp.where(rank1 < k, 1.0, 0.0), axis=0, keepdims=True)
    tied = jnp.abs(n0 - k) + jnp.abs(n1 - k) + jnp.abs(total - k)
    return cnt, rank1, p0, p1, tied


def _bf16_twice(t):
    bits = pltpu.bitcast(t.astype(bf16).astype(f32), u32) >> 16
    return bits | (bits << 16)


def peer_select(x, wq_b, keys_b, *, tq):
    n, d = x.shape
    nh = PEER_HEADS
    full_spec = pl.BlockSpec((1, tq // LANES, PEER_NKEYS, LANES), lambda i, h: (h, i, 0, 0))
    half_spec = pl.BlockSpec((1, tq // LANES, PEER_NKEYS // 2, LANES), lambda i, h: (h, i, 0, 0))
    full = jax.ShapeDtypeStruct((nh, n // LANES, PEER_NKEYS, LANES), u32)
    half = jax.ShapeDtypeStruct((nh, n // LANES, PEER_NKEYS // 2, LANES), u32)
    return pl.pallas_call(
        _peer_select_kernel,
        grid=(n // tq, nh),
        in_specs=[pl.BlockSpec((tq, d), lambda i, h: (i, 0)),
                  pl.BlockSpec((d, 2 * nh * PEER_DHALF), lambda i, h: (0, 0)),
                  pl.BlockSpec((1, 2, PEER_NKEYS, PEER_DHALF), lambda i, h: (h, 0, 0, 0))],
        out_specs=[full_spec, half_spec, full_spec, half_spec],
        out_shape=[full, half, full, half],
        scratch_shapes=[pltpu.VMEM((tq, 2 * nh * PEER_DHALF), f32)],
        compiler_params=pltpu.CompilerParams(
            dimension_semantics=("arbitrary", "arbitrary"), vmem_limit_bytes=VMEM_LIMIT),
        name="peer_select",
    )(x, wq_b, keys_b)


def _peer_main_kernel(xb_ref, x_ref, cnt_ref, rk1_ref, p0_ref, p1_ref, u_ref, unext_ref, vt_ref, vtprev_ref,
                      g_ref, b_ref, o_ref, acc_ref, *slab_refs, te, tt):
    i = pl.program_id(0)
    j = pl.program_id(1)
    last = pl.num_programs(1) - 1
    n_i0 = te // PEER_NKEYS
    n_slab = tt // LANES
    per_chunk = MXU_COLS // LANES
    n_chunk = tt // MXU_COLS
    at_refs, ga_refs = slab_refs[:n_slab], slab_refs[n_slab:]
    half = PEER_NKEYS // 2
    unpack = lambda words: pltpu.bitcast(words, bf16)

    def act_chunk(c, u):
        at = lax.dot_general(u[...], xb_ref[c * MXU_COLS:(c + 1) * MXU_COLS, :],
                             (((1,), (1,)), ((), ())), preferred_element_type=f32)
        for k in range(per_chunk):
            at_refs[c * per_chunk + k][...] = at[:, k * LANES:(k + 1) * LANES]

    def clear_last_gated():
        for k in range(per_chunk):
            ref = ga_refs[(n_chunk - 1) * per_chunk + k]
            ref[...] = jnp.zeros_like(ref)

    @pl.when(jnp.logical_and(i == 0, j == 0))
    def _():
        clear_last_gated()

    @pl.when(j == 0)
    def _():
        acc_ref[...] = jnp.zeros_like(acc_ref)
        act_chunk(0, u_ref)

    def gate_slab(s):
        for il in range(n_i0):
            g = jnp.zeros((PEER_NKEYS, LANES), bf16)
            for h in range(PEER_HEADS):
                cntrow = jnp.broadcast_to(cnt_ref[h, s, il:il + 1, :], (half, LANES))
                p0row = jnp.broadcast_to(p0_ref[h, s, il:il + 1, :], (half, LANES))
                sel = unpack(rk1_ref[h, s]) < unpack(cntrow)
                g = g + jnp.where(sel, unpack(p1_ref[h, s]) * unpack(p0row), jnp.zeros((), bf16))
            ga = _gelu_tanh(at_refs[s][il * PEER_NKEYS:(il + 1) * PEER_NKEYS, :].astype(bf16)) * g
            ga_refs[s][il * half:(il + 1) * half, :] = pltpu.bitcast(ga, u32)

    def out_chunk(c, vt):
        ga = jnp.concatenate([unpack(ga_refs[c * per_chunk + k][...]) for k in range(per_chunk)], axis=1)
        lanes = slice(c * MXU_COLS, (c + 1) * MXU_COLS)
        acc_ref[:, lanes] += jnp.dot(vt[...], ga, preferred_element_type=f32)

    out_chunk(n_chunk - 1, vtprev_ref)
    for c in range(n_chunk):
        gate_slab(c * per_chunk)
        if c + 1 < n_chunk:
            act_chunk(c + 1, u_ref)
        for k in range(1, per_chunk):
            gate_slab(c * per_chunk + k)
        if c + 1 < n_chunk:
            out_chunk(c, vt_ref)
    act_chunk(0, unext_ref)

    @pl.when(j == last)
    def _():
        out_chunk(n_chunk - 1, vt_ref)
        clear_last_gated()
        z = ALPHA * x_ref[...] + acc_ref[...].T
        o_ref[...] = _layer_norm_rows(z, g_ref[...], b_ref[...])


def peer_main(x, xb, stats, u_b, vt_b, ln_g, ln_b, *, tt, te):
    n, d = x.shape
    e = u_b.shape[0]
    nh = PEER_HEADS
    cnt, rk1, p0, p1 = stats
    full_spec = pl.BlockSpec((nh, tt // LANES, te // PEER_NKEYS, LANES), lambda i, j: (0, i, j, 0))
    half_spec = pl.BlockSpec((nh, tt // LANES, PEER_NKEYS // 2, LANES), lambda i, j: (0, i, 0, 0))
    n_e = e // te
    tok_spec = pl.BlockSpec((tt, d), lambda i, j: (i, 0), pipeline_mode=pl.Buffered(1))
    return pl.pallas_call(
        functools.partial(_peer_main_kernel, te=te, tt=tt),
        grid=(n // tt, n_e),
        in_specs=[tok_spec, tok_spec,
                  full_spec, half_spec, full_spec, half_spec,
                  pl.BlockSpec((te, d), lambda i, j: (j, 0)),
                  pl.BlockSpec((te, d), lambda i, j: (jnp.minimum(j + 1, n_e - 1), 0)),
                  pl.BlockSpec((d, te), lambda i, j: (0, j)),
                  pl.BlockSpec((d, te), lambda i, j: (0, jnp.maximum(j - 1, 0))),
                  pl.BlockSpec((1, d), lambda i, j: (0, 0)),
                  pl.BlockSpec((1, d), lambda i, j: (0, 0))],
        out_specs=pl.BlockSpec((tt, d), lambda i, j: (i, 0)),
        out_shape=jax.ShapeDtypeStruct((n, d), f32),
        scratch_shapes=([pltpu.VMEM((d, tt), f32)]
                        + [pltpu.VMEM((te, LANES), f32)] * (tt // LANES)
                        + [pltpu.VMEM((te // 2, LANES), u32)] * (tt // LANES)),
        compiler_params=pltpu.CompilerParams(
            dimension_semantics=("arbitrary", "arbitrary"), vmem_limit_bytes=VMEM_LIMIT),
        name="peer_main",
    )(xb, x, cnt, rk1, p0, p1, u_b, u_b, vt_b, vt_b, ln_g.reshape(1, d), ln_b.reshape(1, d))


def peer_layer(x, w_q, sub_keys, u, v, ln_g, ln_b, *, tq=256, tt=1024, te=1024):
    stats = peer_select(x, w_q.astype(bf16), sub_keys.astype(bf16), tq=tq)
    return peer_main(x, x.astype(bf16), stats, u.astype(bf16), v.T.astype(bf16), ln_g, ln_b, tt=tt, te=te)


def _rglru_kernel(x_ref, win_ref, cw_ref, vec_ref, wg_ref, wout_ref, lng_ref, lnb_ref, o_ref,
                  tail_ref, h_ref, *, tiles_per_seq):
    i = pl.program_id(0)
    w = RG_WIDTH
    x = x_ref[...]
    ts = x.shape[0]

    @pl.when(i % tiles_per_seq == 0)
    def _():
        tail_ref[...] = jnp.zeros_like(tail_ref)
        h_ref[...] = jnp.zeros_like(h_ref)

    conv_b, b_a, b_x, lam = (vec_ref[n:n + 1, :] for n in range(4))
    hin = jnp.dot(x.astype(bf16), win_ref[...], preferred_element_type=f32)
    gate_branch = _gelu_tanh(hin[:, :w])
    hx = hin[:, w:]

    row = lax.broadcasted_iota(jnp.int32, (ts, w), 0)
    row8 = lax.broadcasted_iota(jnp.int32, (SUBLANES, w), 0)
    tail = tail_ref[...]
    xc = hx * cw_ref[RG_CONV - 1:RG_CONV, :] + conv_b
    for s in range(1, RG_CONV):
        rolled = pltpu.roll(hx, s, axis=0)
        head = jnp.where(row8 < s, pltpu.roll(tail, s, axis=0), rolled[:SUBLANES])
        shifted = jnp.concatenate([head, rolled[SUBLANES:]], axis=0)
        xc = xc + shifted * cw_ref[RG_CONV - 1 - s:RG_CONV - s, :]
    tail_ref[...] = hx[ts - SUBLANES:]

    gates = jnp.dot(xc.astype(bf16), wg_ref[...], preferred_element_type=f32)
    r = _sigmoid(gates[:, :w] + b_a)
    ig = _sigmoid(gates[:, w:] + b_x)
    log_a = (-RG_C * _softplus(-lam)) * r
    a = jnp.exp(log_a)
    u = jnp.sqrt(-jnp.tanh(log_a) * (a * a + 1.0)) * (ig * xc)

    s = 1
    while s < ts:
        keep = row >= s
        a_sh = jnp.where(keep, pltpu.roll(a, s, axis=0), 1.0)
        u_sh = jnp.where(keep, pltpu.roll(u, s, axis=0), 0.0)
        u = a * u_sh + u
        a = a * a_sh
        s *= 2
    hs = u + a * h_ref[SUBLANES - 1:SUBLANES, :]
    h_ref[...] = hs[ts - SUBLANES:]

    m = jnp.dot((hs * gate_branch).astype(bf16), wout_ref[...], preferred_element_type=f32)
    o_ref[...] = _layer_norm_rows(ALPHA * x + m, lng_ref[...], lnb_ref[...])


def rglru_layer(x, seq, w_in, conv_w, conv_b, w_a, b_a, w_x, b_x, lam, w_out, ln_g, ln_b, *, ts=256):
    n, d = x.shape
    w = RG_WIDTH
    blockdiag = lambda t: jax.scipy.linalg.block_diag(*t)
    w_gates = jnp.concatenate([blockdiag(w_a), blockdiag(w_x)], axis=1).astype(bf16)
    vecs = jnp.stack([conv_b, b_a.reshape(w), b_x.reshape(w), lam.reshape(w)])
    consts = (w_in.astype(bf16), conv_w, vecs, w_gates, w_out.astype(bf16), ln_g.reshape(1, d), ln_b.reshape(1, d))
    full = lambda arr: pl.BlockSpec(arr.shape, lambda i: (0,) * arr.ndim, pipeline_mode=pl.Buffered(1))
    tok = pl.BlockSpec((ts, d), lambda i: (i, 0))
    return pl.pallas_call(
        functools.partial(_rglru_kernel, tiles_per_seq=seq // ts),
        grid=(n // ts,),
        in_specs=[tok] + [full(c) for c in consts],
        out_specs=tok,
        out_shape=jax.ShapeDtypeStruct((n, d), f32),
        scratch_shapes=[pltpu.VMEM((SUBLANES, w), f32), pltpu.VMEM((SUBLANES, w), f32)],
        compiler_params=pltpu.CompilerParams(
            dimension_semantics=("arbitrary",), vmem_limit_bytes=VMEM_LIMIT),
        name="rglru",
    )(x, *consts)


def _sigmoid(x):
    return 1.0 / (1.0 + jnp.exp(-x))


def _softplus(x):
    return jnp.maximum(x, 0.0) + jnp.log1p(jnp.exp(-jnp.abs(x)))


def _head_sums(t, ones_ref):
    hi = t.astype(bf16)
    lo = (t - hi.astype(f32)).astype(bf16)
    return (jnp.dot(hi, ones_ref[...], preferred_element_type=f32)
            + jnp.dot(lo, ones_ref[...], preferred_element_type=f32))


def _rwkv_proj_kernel(x_ref, xp_ref, mix_ref, wr_ref, wk_ref, wv_ref, w1_ref, w2_ref, a1_ref, a2_ref,
                      g1_ref, g2_ref, vec_ref,
                      r_ref, w_ref, k_ref, v_ref, gate_ref, g_ref, *, tiles_per_seq):
    i = pl.program_id(0)
    x = x_ref[...]
    ts = x.shape[0]
    prev_last = jnp.where(i % tiles_per_seq == 0, 0.0, xp_ref[SUBLANES - 1:SUBLANES, :])
    row = lax.broadcasted_iota(jnp.int32, x.shape, 0)
    xprev = jnp.where(row == 0, prev_last, pltpu.roll(x, 1, axis=0))
    xx = xprev - x

    def mixed(m):
        return (x + xx * mix_ref[m:m + 1, :]).astype(bf16)

    def mm(a, w_ref_):
        return jnp.dot(a, w_ref_[...], preferred_element_type=f32)

    w0, a0 = (vec_ref[n:n + 1, :] for n in range(2))
    lw = mm(jnp.tanh(mm(mixed(1), w1_ref)).astype(bf16), w2_ref)
    w_log = -_softplus(-(w0 + lw)) - 0.5
    r_ref[...] = mm(mixed(0), wr_ref)
    w_ref[...] = jnp.exp(-jnp.exp(w_log))
    k_ref[...] = mm(mixed(2), wk_ref)
    v_ref[...] = mm(mixed(3), wv_ref)
    gate_ref[...] = _sigmoid(a0 + mm(mm(mixed(4), a1_ref).astype(bf16), a2_ref))
    g_ref[...] = mm(_sigmoid(mm(mixed(5), g1_ref)).astype(bf16), g2_ref)


def rwkv_proj(x, seq, mix, w_r, w_k, w_v, w1, w2, a1, a2, g1, g2, vecs, *, ts):
    n, d = x.shape
    full = lambda arr: pl.BlockSpec(arr.shape, lambda i: (0,) * arr.ndim)
    tok = pl.BlockSpec((ts, d), lambda i: (i, 0))
    prev = pl.BlockSpec((SUBLANES, d), lambda i: (jnp.maximum(i * (ts // SUBLANES) - 1, 0), 0))
    weights = (mix, w_r, w_k, w_v, w1, w2, a1, a2, g1, g2, vecs)
    out = jax.ShapeDtypeStruct((n, d), f32)
    return pl.pallas_call(
        functools.partial(_rwkv_proj_kernel, tiles_per_seq=seq // ts),
        grid=(n // ts,),
        in_specs=[tok, prev] + [full(w) for w in weights],
        out_specs=[tok] * 6,
        out_shape=[out] * 6,
        compiler_params=pltpu.CompilerParams(
            dimension_semantics=("arbitrary",), vmem_limit_bytes=VMEM_LIMIT),
        name="rwkv_proj",
    )(x, x, *weights)


def _rwkv_scan_kernel(r_ref, w_ref, k_ref, v_ref, gate_ref, knext_ref, kk_ref, ka_ref, o_ref,
                      s_ref, sa_ref, a_ref, b_ref, km_ref):
    nk = s_ref.shape[0]
    tc = r_ref.shape[0]

    @pl.when(pl.program_id(0) == 0)
    def _():
        s_ref[...] = jnp.zeros_like(s_ref)
        sa_ref[...] = jnp.zeros_like(sa_ref)

    def unit_key(k):
        kk = k * kk_ref[...]
        norm = jnp.sqrt(jnp.sum(kk * kk, axis=-2, keepdims=True))
        return kk / jnp.maximum(norm, RWKV_L2_EPS)

    k_blk = k_ref[...]
    gate = gate_ref[...]
    kk_blk = unit_key(k_blk)
    a_ref[0:tc] = -kk_blk
    a_ref[tc:tc + 1] = -unit_key(knext_ref[...])
    b_ref[...] = kk_blk * gate
    km_ref[...] = k_blk * (1.0 + (gate - 1.0) * ka_ref[...])

    def step(t, sa):
        vt = v_ref[t]
        o = [jnp.zeros(s_ref.shape[1:], f32)] * 2
        nsa = [jnp.zeros(s_ref.shape[1:], f32)] * 2
        for kk in range(nk):
            new = (s_ref[kk] * w_ref[t, kk:kk + 1, :]
                   + (sa * b_ref[t, kk:kk + 1, :] + vt * km_ref[t, kk:kk + 1, :]))
            s_ref[kk] = new
            o[kk % 2] = o[kk % 2] + new * r_ref[t, kk:kk + 1, :]
            nsa[kk % 2] = nsa[kk % 2] + new * a_ref[t + 1, kk:kk + 1, :]
        ot = o[0] + o[1]
        mu = jnp.mean(ot, axis=0, keepdims=True)
        oc = ot - mu
        var = jnp.mean(oc * oc, axis=0, keepdims=True)
        o_ref[t] = oc * lax.rsqrt(var + RWKV_GN_EPS)
        return nsa[0] + nsa[1]

    sa_ref[...] = lax.fori_loop(0, tc, step, sa_ref[...])


def rwkv_scan(r, w, k, v, gate, kk_vec, ka_vec, *, tc):
    s, hd, chains = r.shape
    blk = pl.BlockSpec((tc, hd, chains), lambda i: (i, 0, 0))
    nxt = pl.BlockSpec((1, hd, chains), lambda i: (jnp.minimum((i + 1) * tc, s - 1), 0, 0))
    vec = pl.BlockSpec((hd, chains), lambda i: (0, 0))
    return pl.pallas_call(
        _rwkv_scan_kernel,
        grid=(s // tc,),
        in_specs=[blk] * 5 + [nxt, vec, vec],
        out_specs=blk,
        out_shape=jax.ShapeDtypeStruct((s, hd, chains), f32),
        scratch_shapes=[pltpu.VMEM((hd, hd, chains), f32), pltpu.VMEM((hd, chains), f32),
                        pltpu.VMEM((tc + 1, hd, chains), f32), pltpu.VMEM((tc, hd, chains), f32),
                        pltpu.VMEM((tc, hd, chains), f32)],
        compiler_params=pltpu.CompilerParams(
            dimension_semantics=("arbitrary",), vmem_limit_bytes=VMEM_LIMIT),
        name="rwkv_scan",
    )(r, w, k, v, gate, k, kk_vec, ka_vec)


def _rwkv_out_kernel(x_ref, o_ref, r_ref, k_ref, gate_ref, v_ref, g_ref, vec_ref, ones_ref, wo_ref,
                     lng_ref, lnb_ref, out_ref):
    lnx_g, lnx_b, r_k, k_a = (vec_ref[n:n + 1, :] for n in range(4))
    k_mod = k_ref[...] * (1.0 + (gate_ref[...] - 1.0) * k_a)
    bonus = _head_sums(r_ref[...] * k_mod * r_k, ones_ref) * v_ref[...]
    y = (o_ref[...] * lnx_g + lnx_b + bonus) * g_ref[...]
    m = jnp.dot(y.astype(bf16), wo_ref[...], preferred_element_type=f32)
    out_ref[...] = _layer_norm_rows(ALPHA * x_ref[...] + m, lng_ref[...], lnb_ref[...])


def rwkv_out(x, o, r, k, gate, v, g, vecs, ones, w_o, ln_g, ln_b, *, ts):
    n, d = x.shape
    full = lambda arr: pl.BlockSpec(arr.shape, lambda i: (0,) * arr.ndim)
    tok = pl.BlockSpec((ts, d), lambda i: (i, 0))
    consts = (vecs, ones, w_o, ln_g.reshape(1, d), ln_b.reshape(1, d))
    return pl.pallas_call(
        _rwkv_out_kernel,
        grid=(n // ts,),
        in_specs=[tok] * 7 + [full(c) for c in consts],
        out_specs=tok,
        out_shape=jax.ShapeDtypeStruct((n, d), f32),
        compiler_params=pltpu.CompilerParams(
            dimension_semantics=("arbitrary",), vmem_limit_bytes=VMEM_LIMIT),
        name="rwkv_out",
    )(x, o, r, k, gate, v, g, *consts)


def _head_ones(d, head):
    seg = jnp.arange(d) // head
    return (seg[:, None] == seg[None, :]).astype(bf16)


def rwkv_layer(x, seq, mix, w_r, w_k, w_v, w0, w1, w2, a0, a1, a2, g1, g2, k_k, k_a, r_k, lnx_g, lnx_b, w_o,
               ln_g, ln_b, *, ts=256, tc=16):
    n, d = x.shape
    bsz = n // seq
    nh, hd = RWKV_HEADS, RWKV_HEAD
    ones = _head_ones(d, hd)
    c = lambda w: w.astype(bf16)
    r, w, k, v, gate, g = rwkv_proj(x, seq, mix, c(w_r), c(w_k), c(w_v), c(w1), c(w2), c(a1), c(a2),
                                    c(g1), c(g2), jnp.stack([w0, a0]), ts=ts)
    to_chains = lambda t: t.reshape(bsz, seq, nh, hd).transpose(1, 3, 0, 2).reshape(seq, hd, bsz * nh)
    vec_chains = lambda t: jnp.tile(t.reshape(nh, hd).T, (1, bsz))
    o = rwkv_scan(*(to_chains(t) for t in (r, w, k, v, gate)), vec_chains(k_k), vec_chains(k_a), tc=tc)
    o = o.reshape(seq, hd, bsz, nh).transpose(2, 0, 3, 1).reshape(n, d)
    vecs_out = jnp.stack([lnx_g, lnx_b, r_k.reshape(d), k_a])
    return rwkv_out(x, o, r, k, gate, v, g, vecs_out, ones, c(w_o), ln_g, ln_b, ts=ts)


def kernel(x, rg_w_in, rg_conv_w, rg_conv_b, rg_w_a, rg_b_a, rg_w_x, rg_b_x, rg_lambda, rg_w_out, rw_mix, rw_w_r, rw_w_k, rw_w_v, rw_w0, rw_w1, rw_w2, rw_a0, rw_a1, rw_a2, rw_g1, rw_g2, rw_k_k, rw_k_a, rw_r_k, rw_lnx_g, rw_lnx_b, rw_w_o, peer_w_q, peer_sub_keys, peer_u, peer_v, ln_g, ln_b):
    bsz, s, d = x.shape
    x = x.reshape(bsz * s, d)
    for i in range(DEPTH):
        j = i // 2
        if i % 2 == 0:
            x1 = rglru_layer(x, s, rg_w_in[j], rg_conv_w[j], rg_conv_b[j], rg_w_a[j], rg_b_a[j],
                             rg_w_x[j], rg_b_x[j], rg_lambda[j], rg_w_out[j], ln_g[i, 0], ln_b[i, 0])
        else:
            x1 = rwkv_layer(x, s, rw_mix[j], rw_w_r[j], rw_w_k[j], rw_w_v[j], rw_w0[j], rw_w1[j],
                            rw_w2[j], rw_a0[j], rw_a1[j], rw_a2[j], rw_g1[j], rw_g2[j],
                            rw_k_k[j], rw_k_a[j], rw_r_k[j], rw_lnx_g[j], rw_lnx_b[j], rw_w_o[j],
                            ln_g[i, 0], ln_b[i, 0])
        x = peer_layer(x1, peer_w_q[i], peer_sub_keys[i], peer_u[i], peer_v[i], ln_g[i, 1], ln_b[i, 1])
    return x.reshape(bsz, s, d)
```

```python
import functools
import math
import jax, jax.numpy as jnp
from jax import lax
from jax.experimental import pallas as pl
from jax.experimental.pallas import tpu as pltpu

D_MODEL = 1024
DEPTH = 2
RG_WIDTH = 1408
RG_HEADS = 16
RG_BLOCK = 88
RG_CONV = 4
RG_C = 8.0
RWKV_HEAD = 64
RWKV_HEADS = 16
RWKV_GN_EPS = 64e-5
RWKV_L2_EPS = 1e-12
PEER_HEADS = 8
PEER_NKEYS = 128
PEER_DHALF = 128
PEER_TOPK = 16
PEER_BLOCK = 128
ALPHA = (2 * DEPTH) ** 0.25
LN_EPS = 1e-5

LANES = 128
SUBLANES = 8
MXU_COLS = 256
VMEM_LIMIT = 56 << 20

f32 = jnp.float32
bf16 = jnp.bfloat16
u32 = jnp.uint32


def _gelu_tanh(x):
    return 0.5 * x * (1.0 + jnp.tanh(math.sqrt(2.0 / math.pi) * (x + 0.044715 * (x * x * x))))


def _layer_norm_rows(z, g, b):
    mu = jnp.mean(z, axis=-1, keepdims=True)
    zc = z - mu
    var = jnp.mean(zc * zc, axis=-1, keepdims=True)
    return zc * lax.rsqrt(var + LN_EPS) * g + b


def _top_rows(s, k, with_rank=False):
    rows = []
    rank = jnp.full(s.shape, float(k), f32)
    for j in range(k):
        m = jnp.max(s, axis=0, keepdims=True)
        rows.append(m)
        hit = s == m
        if with_rank:
            rank = jnp.where(hit, float(j), rank)
        s = jnp.where(hit, -jnp.inf, s)
    return (rows, rank) if with_rank else rows


def _peer_select_kernel(x_ref, wq_ref, keys_ref, cnt_ref, rk1_ref, p0_ref, p1_ref, q_ref):
    h = pl.program_id(1)

    @pl.when(h == 0)
    def _():
        q_ref[...] = jnp.dot(x_ref[...].astype(bf16), wq_ref[...], preferred_element_type=f32)

    def scores(p):
        off = pl.multiple_of((2 * h + p) * PEER_DHALF, PEER_DHALF)
        qhp = q_ref[:, pl.ds(off, PEER_DHALF)].astype(bf16)
        return lax.dot_general(keys_ref[0, p], qhp, (((1,), (1,)), ((), ())),
                               preferred_element_type=f32)

    def select_all(stable):
        s0_all = scores(0)
        s1_all = scores(1)
        tied = 0.0
        for c in range(s0_all.shape[1] // LANES):
            lanes = slice(c * LANES, (c + 1) * LANES)
            cnt, rank1, p0, p1, tied_c = _select_slab(s0_all[:, lanes], s1_all[:, lanes], stable=stable)
            cnt_ref[0, c] = _bf16_twice(cnt)
            p0_ref[0, c] = _bf16_twice(p0)
            rk1_ref[0, c] = pltpu.bitcast(rank1.astype(bf16), u32)
            p1_ref[0, c] = pltpu.bitcast(p1.astype(bf16), u32)
            if not stable:
                tied = jnp.maximum(tied, jnp.max(tied_c))
        return tied

    tied = select_all(stable=False)

    @pl.when(tied > 0.0)
    def _():
        select_all(stable=True)


def _top_rows_stable(s, k):
    rows = []
    idx = lax.broadcasted_iota(jnp.int32, s.shape, 0)
    rank = jnp.full(s.shape, float(k), f32)
    for j in range(k):
        m = jnp.max(s, axis=0, keepdims=True)
        rows.append(m)
        first = jnp.min(jnp.where(s == m, idx, s.shape[0]), axis=0, keepdims=True)
        pick = idx == first
        rank = jnp.where(pick, float(j), rank)
        s = jnp.where(pick, -jnp.inf, s)
    return rows, rank


def _select_slab(s0, s1, *, stable):
    if stable:
        a, rank0 = _top_rows_stable(s0, PEER_TOPK)
        b, rank1 = _top_rows_stable(s1, PEER_TOPK)
    else:
        a = _top_rows(s0, PEER_TOPK)
        b, rank1 = _top_rows(s1, PEER_TOPK, with_rank=True)
    row = lax.broadcasted_iota(jnp.int32, (PEER_TOPK, s0.shape[1]), 0)
    bmat = jnp.zeros((PEER_TOPK, s0.shape[1]), f32)
    for j in range(PEER_TOPK):
        bmat = jnp.where(row == j, b[j], bmat)
    cands = [a[0] + bmat] + [a[i] + bmat[:SUBLANES] for i in range(1, PEER_TOPK)]
    cand = jnp.concatenate(cands, axis=0)
    if stable:
        picked = _top_rows_stable(cand, PEER_TOPK)[1] < float(PEER_TOPK)
    else:
        picked = cand >= _top_rows(cand, PEER_TOPK)[-1]
    picked = jnp.where(picked, 1.0, 0.0)
    cmax = a[0] + b[0]
    z = jnp.sum(picked * jnp.exp(cand - cmax), axis=0, keepdims=True)
    cnt = jnp.zeros_like(s0)
    total = jnp.zeros_like(z)
    lo = 0
    for i in range(PEER_TOPK):
        hi = lo + cands[i].shape[0]
        cnt_i = jnp.sum(picked[lo:hi], axis=0, keepdims=True)
        lo = hi
        total = total + cnt_i
        cnt = jnp.where((rank0 == float(i)) if stable else (s0 == a[i]), cnt_i, cnt)
    p0 = jnp.exp(s0 - a[0]) / z
    p1 = jnp.exp(s1 - b[0])
    if stable:
        return cnt, rank1, p0, p1, None
    k = float(PEER_TOPK)
    n0 = jnp.sum(jnp.where(s0 >= a[-1], 1.0, 0.0), axis=0, keepdims=True)
    n1 = jnp.sum(jnp.where(rank1 < k, 1.0, 0.0), axis=0, keepdims=True)
    tied = jnp.abs(n0 - k) + jnp.abs(n1 - k) + jnp.abs(total - k)
    return cnt, rank1, p0, p1, tied


def _bf16_twice(t):
    bits = pltpu.bitcast(t.astype(bf16).astype(f32), u32) >> 16
    return bits | (bits << 16)


def peer_select(x, wq_b, keys_b, *, tq):
    n, d = x.shape
    nh = PEER_HEADS
    full_spec = pl.BlockSpec((1, tq // LANES, PEER_NKEYS, LANES), lambda i, h: (h, i, 0, 0))
    half_spec = pl.BlockSpec((1, tq // LANES, PEER_NKEYS // 2, LANES), lambda i, h: (h, i, 0, 0))
    full = jax.ShapeDtypeStruct((nh, n // LANES, PEER_NKEYS, LANES), u32)
    half = jax.ShapeDtypeStruct((nh, n // LANES, PEER_NKEYS // 2, LANES), u32)
    return pl.pallas_call(
        _peer_select_kernel,
        grid=(n // tq, nh),
        in_specs=[pl.BlockSpec((tq, d), lambda i, h: (i, 0)),
                  pl.BlockSpec((d, 2 * nh * PEER_DHALF), lambda i, h: (0, 0)),
                  pl.BlockSpec((1, 2, PEER_NKEYS, PEER_DHALF), lambda i, h: (h, 0, 0, 0))],
        out_specs=[full_spec, half_spec, full_spec, half_spec],
        out_shape=[full, half, full, half],
        scratch_shapes=[pltpu.VMEM((tq, 2 * nh * PEER_DHALF), f32)],
        compiler_params=pltpu.CompilerParams(
            dimension_semantics=("arbitrary", "arbitrary"), vmem_limit_bytes=VMEM_LIMIT),
        name="peer_select",
    )(x, wq_b, keys_b)


def _peer_main_kernel(xb_ref, x_ref, cnt_ref, rk1_ref, p0_ref, p1_ref, u_ref, vt_ref,
                      g_ref, b_ref, o_ref, acc_ref, *slab_refs, te, tt):
    j = pl.program_id(1)
    n_i0 = te // PEER_NKEYS
    n_slab = tt // LANES
    per_chunk = MXU_COLS // LANES
    at_refs, ga_refs = slab_refs[:n_slab], slab_refs[n_slab:]
    half = PEER_NKEYS // 2
    unpack = lambda words: pltpu.bitcast(words, bf16)

    @pl.when(j == 0)
    def _():
        acc_ref[...] = jnp.zeros_like(acc_ref)

    def act_chunk(c):
        at = lax.dot_general(u_ref[...], xb_ref[c * MXU_COLS:(c + 1) * MXU_COLS, :],
                             (((1,), (1,)), ((), ())), preferred_element_type=f32)
        for k in range(per_chunk):
            at_refs[c * per_chunk + k][...] = at[:, k * LANES:(k + 1) * LANES]

    def gate_slab(s):
        for il in range(n_i0):
            g = jnp.zeros((PEER_NKEYS, LANES), bf16)
            for h in range(PEER_HEADS):
                cntrow = jnp.broadcast_to(cnt_ref[h, s, il:il + 1, :], (half, LANES))
                p0row = jnp.broadcast_to(p0_ref[h, s, il:il + 1, :], (half, LANES))
                sel = unpack(rk1_ref[h, s]) < unpack(cntrow)
                g = g + jnp.where(sel, unpack(p1_ref[h, s]) * unpack(p0row), jnp.zeros((), bf16))
            ga = _gelu_tanh(at_refs[s][il * PEER_NKEYS:(il + 1) * PEER_NKEYS, :].astype(bf16)) * g
            ga_refs[s][il * half:(il + 1) * half, :] = pltpu.bitcast(ga, u32)

    def out_chunk(c):
        ga = jnp.concatenate([unpack(ga_refs[c * per_chunk + k][...]) for k in range(per_chunk)], axis=1)
        lanes = slice(c * MXU_COLS, (c + 1) * MXU_COLS)
        acc_ref[:, lanes] += jnp.dot(vt_ref[...], ga, preferred_element_type=f32)

    n_chunk = tt // MXU_COLS
    act_chunk(0)
    for c in range(n_chunk):
        gate_slab(c * per_chunk)
        if c + 1 < n_chunk:
            act_chunk(c + 1)
        for k in range(1, per_chunk):
            gate_slab(c * per_chunk + k)
        out_chunk(c)

    @pl.when(j == pl.num_programs(1) - 1)
    def _():
        z = ALPHA * x_ref[...] + acc_ref[...].T
        o_ref[...] = _layer_norm_rows(z, g_ref[...], b_ref[...])


def peer_main(x, xb, stats, u_b, vt_b, ln_g, ln_b, *, tt, te):
    n, d = x.shape
    e = u_b.shape[0]
    nh = PEER_HEADS
    cnt, rk1, p0, p1 = stats
    full_spec = pl.BlockSpec((nh, tt // LANES, te // PEER_NKEYS, LANES), lambda i, j: (0, i, j, 0))
    half_spec = pl.BlockSpec((nh, tt // LANES, PEER_NKEYS // 2, LANES), lambda i, j: (0, i, 0, 0))
    return pl.pallas_call(
        functools.partial(_peer_main_kernel, te=te, tt=tt),
        grid=(n // tt, e // te),
        in_specs=[pl.BlockSpec((tt, d), lambda i, j: (i, 0)),
                  pl.BlockSpec((tt, d), lambda i, j: (i, 0)),
                  full_spec, half_spec, full_spec, half_spec,
                  pl.BlockSpec((te, d), lambda i, j: (j, 0)),
                  pl.BlockSpec((d, te), lambda i, j: (0, j)),
                  pl.BlockSpec((1, d), lambda i, j: (0, 0)),
                  pl.BlockSpec((1, d), lambda i, j: (0, 0))],
        out_specs=pl.BlockSpec((tt, d), lambda i, j: (i, 0)),
        out_shape=jax.ShapeDtypeStruct((n, d), f32),
        scratch_shapes=([pltpu.VMEM((d, tt), f32)]
                        + [pltpu.VMEM((te, LANES), f32)] * (tt // LANES)
                        + [pltpu.VMEM((te // 2, LANES), u32)] * (tt // LANES)),
        compiler_params=pltpu.CompilerParams(
            dimension_semantics=("arbitrary", "arbitrary"), vmem_limit_bytes=VMEM_LIMIT),
        name="peer_main",
    )(xb, x, cnt, rk1, p0, p1, u_b, vt_b, ln_g.reshape(1, d), ln_b.reshape(1, d))


def peer_layer(x, w_q, sub_keys, u, v, ln_g, ln_b, *, tq=256, tt=1024, te=1024):
    stats = peer_select(x, w_q.astype(bf16), sub_keys.astype(bf16), tq=tq)
    return peer_main(x, x.astype(bf16), stats, u.astype(bf16), v.T.astype(bf16), ln_g, ln_b, tt=tt, te=te)


def _rglru_kernel(x_ref, win_ref, cw_ref, vec_ref, wg_ref, wout_ref, lng_ref, lnb_ref, o_ref,
                  tail_ref, h_ref, *, tiles_per_seq):
    i = pl.program_id(0)
    w = RG_WIDTH
    x = x_ref[...]
    ts = x.shape[0]

    @pl.when(i % tiles_per_seq == 0)
    def _():
        tail_ref[...] = jnp.zeros_like(tail_ref)
        h_ref[...] = jnp.zeros_like(h_ref)

    conv_b, b_a, b_x, lam = (vec_ref[n:n + 1, :] for n in range(4))
    hin = jnp.dot(x.astype(bf16), win_ref[...], preferred_element_type=f32)
    gate_branch = _gelu_tanh(hin[:, :w])
    hx = hin[:, w:]

    row = lax.broadcasted_iota(jnp.int32, (ts, w), 0)
    row8 = lax.broadcasted_iota(jnp.int32, (SUBLANES, w), 0)
    tail = tail_ref[...]
    xc = hx * cw_ref[RG_CONV - 1:RG_CONV, :] + conv_b
    for s in range(1, RG_CONV):
        rolled = pltpu.roll(hx, s, axis=0)
        head = jnp.where(row8 < s, pltpu.roll(tail, s, axis=0), rolled[:SUBLANES])
        shifted = jnp.concatenate([head, rolled[SUBLANES:]], axis=0)
        xc = xc + shifted * cw_ref[RG_CONV - 1 - s:RG_CONV - s, :]
    tail_ref[...] = hx[ts - SUBLANES:]

    gates = jnp.dot(xc.astype(bf16), wg_ref[...], preferred_element_type=f32)
    r = _sigmoid(gates[:, :w] + b_a)
    ig = _sigmoid(gates[:, w:] + b_x)
    log_a = (-RG_C * _softplus(-lam)) * r
    a = jnp.exp(log_a)
    u = jnp.sqrt(-jnp.tanh(log_a) * (a * a + 1.0)) * (ig * xc)

    s = 1
    while s < ts:
        keep = row >= s
        a_sh = jnp.where(keep, pltpu.roll(a, s, axis=0), 1.0)
        u_sh = jnp.where(keep, pltpu.roll(u, s, axis=0), 0.0)
        u = a * u_sh + u
        a = a * a_sh
        s *= 2
    hs = u + a * h_ref[SUBLANES - 1:SUBLANES, :]
    h_ref[...] = hs[ts - SUBLANES:]

    m = jnp.dot((hs * gate_branch).astype(bf16), wout_ref[...], preferred_element_type=f32)
    o_ref[...] = _layer_norm_rows(ALPHA * x + m, lng_ref[...], lnb_ref[...])


def rglru_layer(x, seq, w_in, conv_w, conv_b, w_a, b_a, w_x, b_x, lam, w_out, ln_g, ln_b, *, ts=256):
    n, d = x.shape
    w = RG_WIDTH
    blockdiag = lambda t: jax.scipy.linalg.block_diag(*t)
    w_gates = jnp.concatenate([blockdiag(w_a), blockdiag(w_x)], axis=1).astype(bf16)
    vecs = jnp.stack([conv_b, b_a.reshape(w), b_x.reshape(w), lam.reshape(w)])
    consts = (w_in.astype(bf16), conv_w, vecs, w_gates, w_out.astype(bf16), ln_g.reshape(1, d), ln_b.reshape(1, d))
    full = lambda arr: pl.BlockSpec(arr.shape, lambda i: (0,) * arr.ndim, pipeline_mode=pl.Buffered(1))
    tok = pl.BlockSpec((ts, d), lambda i: (i, 0))
    return pl.pallas_call(
        functools.partial(_rglru_kernel, tiles_per_seq=seq // ts),
        grid=(n // ts,),
        in_specs=[tok] + [full(c) for c in consts],
        out_specs=tok,
        out_shape=jax.ShapeDtypeStruct((n, d), f32),
        scratch_shapes=[pltpu.VMEM((SUBLANES, w), f32), pltpu.VMEM((SUBLANES, w), f32)],
        compiler_params=pltpu.CompilerParams(
            dimension_semantics=("arbitrary",), vmem_limit_bytes=VMEM_LIMIT),
        name="rglru",
    )(x, *consts)


def _sigmoid(x):
    return 1.0 / (1.0 + jnp.exp(-x))


def _softplus(x):
    return jnp.maximum(x, 0.0) + jnp.log1p(jnp.exp(-jnp.abs(x)))


def _head_sums(t, ones_ref):
    hi = t.astype(bf16)
    lo = (t - hi.astype(f32)).astype(bf16)
    return (jnp.dot(hi, ones_ref[...], preferred_element_type=f32)
            + jnp.dot(lo, ones_ref[...], preferred_element_type=f32))


def _rwkv_proj_kernel(x_ref, xp_ref, mix_ref, wr_ref, wk_ref, wv_ref, w1_ref, w2_ref, a1_ref, a2_ref,
                      g1_ref, g2_ref, vec_ref,
                      r_ref, w_ref, k_ref, v_ref, gate_ref, g_ref, *, tiles_per_seq):
    i = pl.program_id(0)
    x = x_ref[...]
    ts = x.shape[0]
    prev_last = jnp.where(i % tiles_per_seq == 0, 0.0, xp_ref[SUBLANES - 1:SUBLANES, :])
    row = lax.broadcasted_iota(jnp.int32, x.shape, 0)
    xprev = jnp.where(row == 0, prev_last, pltpu.roll(x, 1, axis=0))
    xx = xprev - x

    def mixed(m):
        return (x + xx * mix_ref[m:m + 1, :]).astype(bf16)

    def mm(a, w_ref_):
        return jnp.dot(a, w_ref_[...], preferred_element_type=f32)

    w0, a0 = (vec_ref[n:n + 1, :] for n in range(2))
    lw = mm(jnp.tanh(mm(mixed(1), w1_ref)).astype(bf16), w2_ref)
    w_log = -_softplus(-(w0 + lw)) - 0.5
    r_ref[...] = mm(mixed(0), wr_ref)
    w_ref[...] = jnp.exp(-jnp.exp(w_log))
    k_ref[...] = mm(mixed(2), wk_ref)
    v_ref[...] = mm(mixed(3), wv_ref)
    gate_ref[...] = _sigmoid(a0 + mm(mm(mixed(4), a1_ref).astype(bf16), a2_ref))
    g_ref[...] = mm(_sigmoid(mm(mixed(5), g1_ref)).astype(bf16), g2_ref)


def rwkv_proj(x, seq, mix, w_r, w_k, w_v, w1, w2, a1, a2, g1, g2, vecs, *, ts):
    n, d = x.shape
    full = lambda arr: pl.BlockSpec(arr.shape, lambda i: (0,) * arr.ndim)
    tok = pl.BlockSpec((ts, d), lambda i: (i, 0))
    prev = pl.BlockSpec((SUBLANES, d), lambda i: (jnp.maximum(i * (ts // SUBLANES) - 1, 0), 0))
    weights = (mix, w_r, w_k, w_v, w1, w2, a1, a2, g1, g2, vecs)
    out = jax.ShapeDtypeStruct((n, d), f32)
    return pl.pallas_call(
        functools.partial(_rwkv_proj_kernel, tiles_per_seq=seq // ts),
        grid=(n // ts,),
        in_specs=[tok, prev] + [full(w) for w in weights],
        out_specs=[tok] * 6,
        out_shape=[out] * 6,
        compiler_params=pltpu.CompilerParams(
            dimension_semantics=("arbitrary",), vmem_limit_bytes=VMEM_LIMIT),
        name="rwkv_proj",
    )(x, x, *weights)


def _rwkv_scan_kernel(rin_ref, w_ref, k_ref, vin_ref, gate_ref, knext_ref, kk_ref, ka_ref, o_ref,
                      s_ref, sa_ref, a_ref, b_ref, km_ref, r_ref, v_ref):
    nk = s_ref.shape[0]
    tc = rin_ref.shape[0]
    r_ref[...] = rin_ref[...].astype(f32)
    v_ref[...] = vin_ref[...].astype(f32)

    @pl.when(pl.program_id(0) == 0)
    def _():
        s_ref[...] = jnp.zeros_like(s_ref)
        sa_ref[...] = jnp.zeros_like(sa_ref)

    def unit_key(k):
        kk = k * kk_ref[...]
        norm = jnp.sqrt(jnp.sum(kk * kk, axis=-2, keepdims=True))
        return kk / jnp.maximum(norm, RWKV_L2_EPS)

    k_blk = k_ref[...].astype(f32)
    gate = gate_ref[...].astype(f32)
    kk_blk = unit_key(k_blk)
    a_ref[0:tc] = -kk_blk
    a_ref[tc:tc + 1] = -unit_key(knext_ref[...].astype(f32))
    b_ref[...] = kk_blk * gate
    km_ref[...] = k_blk * (1.0 + (gate - 1.0) * ka_ref[...])

    def step(t, sa):
        vt = v_ref[t]
        o = [jnp.zeros(s_ref.shape[1:], f32)] * 2
        nsa = [jnp.zeros(s_ref.shape[1:], f32)] * 2
        for kk in range(nk):
            new = (s_ref[kk] * w_ref[t, kk:kk + 1, :]
                   + (sa * b_ref[t, kk:kk + 1, :] + vt * km_ref[t, kk:kk + 1, :]))
            s_ref[kk] = new
            o[kk % 2] = o[kk % 2] + new * r_ref[t, kk:kk + 1, :]
            nsa[kk % 2] = nsa[kk % 2] + new * a_ref[t + 1, kk:kk + 1, :]
        ot = o[0] + o[1]
        mu = jnp.mean(ot, axis=0, keepdims=True)
        oc = ot - mu
        var = jnp.mean(oc * oc, axis=0, keepdims=True)
        o_ref[t] = oc * lax.rsqrt(var + RWKV_GN_EPS)
        return nsa[0] + nsa[1]

    sa_ref[...] = lax.fori_loop(0, tc, step, sa_ref[...])


def rwkv_scan(r, w, k, v, gate, kk_vec, ka_vec, *, tc):
    s, hd, chains = r.shape
    blk = pl.BlockSpec((tc, hd, chains), lambda i: (i, 0, 0))
    nxt = pl.BlockSpec((1, hd, chains), lambda i: (jnp.minimum((i + 1) * tc, s - 1), 0, 0))
    vec = pl.BlockSpec((hd, chains), lambda i: (0, 0))
    return pl.pallas_call(
        _rwkv_scan_kernel,
        grid=(s // tc,),
        in_specs=[blk] * 5 + [nxt, vec, vec],
        out_specs=blk,
        out_shape=jax.ShapeDtypeStruct((s, hd, chains), f32),
        scratch_shapes=[pltpu.VMEM((hd, hd, chains), f32), pltpu.VMEM((hd, chains), f32),
                        pltpu.VMEM((tc + 1, hd, chains), f32)] + [pltpu.VMEM((tc, hd, chains), f32)] * 4,
        compiler_params=pltpu.CompilerParams(
            dimension_semantics=("arbitrary",), vmem_limit_bytes=VMEM_LIMIT),
        name="rwkv_scan",
    )(r, w, k, v, gate, k, kk_vec, ka_vec)


def _rwkv_out_kernel(x_ref, o_ref, r_ref, k_ref, gate_ref, v_ref, g_ref, vec_ref, ones_ref, wo_ref,
                     lng_ref, lnb_ref, out_ref):
    lnx_g, lnx_b, r_k, k_a = (vec_ref[n:n + 1, :] for n in range(4))
    k_mod = k_ref[...] * (1.0 + (gate_ref[...] - 1.0) * k_a)
    bonus = _head_sums(r_ref[...] * k_mod * r_k, ones_ref) * v_ref[...]
    y = (o_ref[...] * lnx_g + lnx_b + bonus) * g_ref[...]
    m = jnp.dot(y.astype(bf16), wo_ref[...], preferred_element_type=f32)
    out_ref[...] = _layer_norm_rows(ALPHA * x_ref[...] + m, lng_ref[...], lnb_ref[...])


def rwkv_out(x, o, r, k, gate, v, g, vecs, ones, w_o, ln_g, ln_b, *, ts):
    n, d = x.shape
    full = lambda arr: pl.BlockSpec(arr.shape, lambda i: (0,) * arr.ndim)
    tok = pl.BlockSpec((ts, d), lambda i: (i, 0))
    consts = (vecs, ones, w_o, ln_g.reshape(1, d), ln_b.reshape(1, d))
    return pl.pallas_call(
        _rwkv_out_kernel,
        grid=(n // ts,),
        in_specs=[tok] * 7 + [full(c) for c in consts],
        out_specs=tok,
        out_shape=jax.ShapeDtypeStruct((n, d), f32),
        compiler_params=pltpu.CompilerParams(
            dimension_semantics=("arbitrary",), vmem_limit_bytes=VMEM_LIMIT),
        name="rwkv_out",
    )(x, o, r, k, gate, v, g, *consts)


def _head_ones(d, head):
    seg = jnp.arange(d) // head
    return (seg[:, None] == seg[None, :]).astype(bf16)


def rwkv_layer(x, seq, mix, w_r, w_k, w_v, w0, w1, w2, a0, a1, a2, g1, g2, k_k, k_a, r_k, lnx_g, lnx_b, w_o,
               ln_g, ln_b, *, ts=256, tc=16):
    n, d = x.shape
    bsz = n // seq
    nh, hd = RWKV_HEADS, RWKV_HEAD
    ones = _head_ones(d, hd)
    c = lambda w: w.astype(bf16)
    r, w, k, v, gate, g = rwkv_proj(x, seq, mix, c(w_r), c(w_k), c(w_v), c(w1), c(w2), c(a1), c(a2),
                                    c(g1), c(g2), jnp.stack([w0, a0]), ts=ts)
    to_chains = lambda t: t.reshape(bsz, seq, nh, hd).transpose(1, 3, 0, 2).reshape(seq, hd, bsz * nh)
    vec_chains = lambda t: jnp.tile(t.reshape(nh, hd).T, (1, bsz))
    half_width = lambda t: to_chains(t.astype(bf16))
    o = rwkv_scan(half_width(r), to_chains(w), half_width(k), half_width(v), half_width(gate),
                  vec_chains(k_k), vec_chains(k_a), tc=tc)
    o = o.reshape(seq, hd, bsz, nh).transpose(2, 0, 3, 1).reshape(n, d)
    vecs_out = jnp.stack([lnx_g, lnx_b, r_k.reshape(d), k_a])
    return rwkv_out(x, o, r, k, gate, v, g, vecs_out, ones, c(w_o), ln_g, ln_b, ts=ts)


def kernel(x, rg_w_in, rg_conv_w, rg_conv_b, rg_w_a, rg_b_a, rg_w_x, rg_b_x, rg_lambda, rg_w_out, rw_mix, rw_w_r, rw_w_k, rw_w_v, rw_w0, rw_w1, rw_w2, rw_a0, rw_a1, rw_a2, rw_g1, rw_g2, rw_k_k, rw_k_a, rw_r_k, rw_lnx_g, rw_lnx_b, rw_w_o, peer_w_q, peer_sub_keys, peer_u, peer_v, ln_g, ln_b):
    bsz, s, d = x.shape
    x = x.reshape(bsz * s, d)
    for i in range(DEPTH):
        j = i // 2
        if i % 2 == 0:
            x1 = rglru_layer(x, s, rg_w_in[j], rg_conv_w[j], rg_conv_b[j], rg_w_a[j], rg_b_a[j],
                             rg_w_x[j], rg_b_x[j], rg_lambda[j], rg_w_out[j], ln_g[i, 0], ln_b[i, 0])
        else:
            x1 = rwkv_layer(x, s, rw_mix[j], rw_w_r[j], rw_w_k[j], rw_w_v[j], rw_w0[j], rw_w1[j],
                            rw_w2[j], rw_a0[j], rw_a1[j], rw_a2[j], rw_g1[j], rw_g2[j],
                            rw_k_k[j], rw_k_a[j], rw_r_k[j], rw_lnx_g[j], rw_lnx_b[j], rw_w_o[j],
                            ln_g[i, 0], ln_b[i, 0])
        x = peer_layer(x1, peer_w_q[i], peer_sub_keys[i], peer_u[i], peer_v[i], ln_g[i, 1], ln_b[i, 1])
    return x.reshape(bsz, s, d)
```

```python
import functools
import math
import jax, jax.numpy as jnp
from jax import lax
from jax.experimental import pallas as pl
from jax.experimental.pallas import tpu as pltpu

D_MODEL = 1024
DEPTH = 2
RG_WIDTH = 1408
RG_HEADS = 16
RG_BLOCK = 88
RG_CONV = 4
RG_C = 8.0
RWKV_HEAD = 64
RWKV_HEADS = 16
RWKV_GN_EPS = 64e-5
RWKV_L2_EPS = 1e-12
PEER_HEADS = 8
PEER_NKEYS = 128
PEER_DHALF = 128
PEER_TOPK = 16
PEER_BLOCK = 128
ALPHA = (2 * DEPTH) ** 0.25
LN_EPS = 1e-5

LANES = 128
SUBLANES = 8
MXU_COLS = 256
VMEM_LIMIT = 56 << 20

f32 = jnp.float32
bf16 = jnp.bfloat16
u32 = jnp.uint32


def _gelu_tanh(x):
    return 0.5 * x * (1.0 + jnp.tanh(math.sqrt(2.0 / math.pi) * (x + 0.044715 * (x * x * x))))


def _layer_norm_rows(z, g, b):
    mu = jnp.mean(z, axis=-1, keepdims=True)
    zc = z - mu
    var = jnp.mean(zc * zc, axis=-1, keepdims=True)
    return zc * lax.rsqrt(var + LN_EPS) * g + b


def _top_rows(s, k, with_rank=False):
    rows = []
    rank = jnp.full(s.shape, float(k), f32)
    for j in range(k):
        m = jnp.max(s, axis=0, keepdims=True)
        rows.append(m)
        hit = s == m
        if with_rank:
            rank = jnp.where(hit, float(j), rank)
        s = jnp.where(hit, -jnp.inf, s)
    return (rows, rank) if with_rank else rows


def _peer_select_kernel(x_ref, wq_ref, keys_ref, cnt_ref, rk1_ref, p0_ref, p1_ref, q_ref):
    h = pl.program_id(1)

    @pl.when(h == 0)
    def _():
        q_ref[...] = jnp.dot(x_ref[...].astype(bf16), wq_ref[...], preferred_element_type=f32)

    def scores(p):
        off = pl.multiple_of((2 * h + p) * PEER_DHALF, PEER_DHALF)
        qhp = q_ref[:, pl.ds(off, PEER_DHALF)].astype(bf16)
        return lax.dot_general(keys_ref[0, p], qhp, (((1,), (1,)), ((), ())),
                               preferred_element_type=f32)

    def select_all(stable):
        s0_all = scores(0)
        s1_all = scores(1)
        tied = 0.0
        for c in range(s0_all.shape[1] // LANES):
            lanes = slice(c * LANES, (c + 1) * LANES)
            cnt, rank1, p0, p1, tied_c = _select_slab(s0_all[:, lanes], s1_all[:, lanes], stable=stable)
            cnt_ref[0, c] = _bf16_twice(cnt)
            p0_ref[0, c] = _bf16_twice(p0)
            rk1_ref[0, c] = pltpu.bitcast(rank1.astype(bf16), u32)
            p1_ref[0, c] = pltpu.bitcast(p1.astype(bf16), u32)
            if not stable:
                tied = jnp.maximum(tied, jnp.max(tied_c))
        return tied

    tied = select_all(stable=False)

    @pl.when(tied > 0.0)
    def _():
        select_all(stable=True)


def _top_rows_stable(s, k):
    rows = []
    idx = lax.broadcasted_iota(jnp.int32, s.shape, 0)
    rank = jnp.full(s.shape, float(k), f32)
    for j in range(k):
        m = jnp.max(s, axis=0, keepdims=True)
        rows.append(m)
        first = jnp.min(jnp.where(s == m, idx, s.shape[0]), axis=0, keepdims=True)
        pick = idx == first
        rank = jnp.where(pick, float(j), rank)
        s = jnp.where(pick, -jnp.inf, s)
    return rows, rank


def _select_slab(s0, s1, *, stable):
    if stable:
        a, rank0 = _top_rows_stable(s0, PEER_TOPK)
        b, rank1 = _top_rows_stable(s1, PEER_TOPK)
    else:
        a = _top_rows(s0, PEER_TOPK)
        b, rank1 = _top_rows(s1, PEER_TOPK, with_rank=True)
    row = lax.broadcasted_iota(jnp.int32, (PEER_TOPK, s0.shape[1]), 0)
    bmat = jnp.zeros((PEER_TOPK, s0.shape[1]), f32)
    for j in range(PEER_TOPK):
        bmat = jnp.where(row == j, b[j], bmat)
    cands = [a[0] + bmat] + [a[i] + bmat[:SUBLANES] for i in range(1, PEER_TOPK)]
    cand = jnp.concatenate(cands, axis=0)
    if stable:
        picked = _top_rows_stable(cand, PEER_TOPK)[1] < float(PEER_TOPK)
    else:
        picked = cand >= _top_rows(cand, PEER_TOPK)[-1]
    picked = jnp.where(picked, 1.0, 0.0)
    cmax = a[0] + b[0]
    z = jnp.sum(picked * jnp.exp(cand - cmax), axis=0, keepdims=True)
    cnt = jnp.zeros_like(s0)
    total = jnp.zeros_like(z)
    lo = 0
    for i in range(PEER_TOPK):
        hi = lo + cands[i].shape[0]
        cnt_i = jnp.sum(picked[lo:hi], axis=0, keepdims=True)
        lo = hi
        total = total + cnt_i
        cnt = jnp.where((rank0 == float(i)) if stable else (s0 == a[i]), cnt_i, cnt)
    p0 = jnp.exp(s0 - a[0]) / z
    p1 = jnp.exp(s1 - b[0])
    if stable:
        return cnt, rank1, p0, p1, None
    k = float(PEER_TOPK)
    n0 = jnp.sum(jnp.where(s0 >= a[-1], 1.0, 0.0), axis=0, keepdims=True)
    n1 = jnp.sum(jnp.where(rank1 < k, 1.0, 0.0), axis=0, keepdims=True)
    tied = jnp.abs(n0 - k) + jnp.abs(n1 - k) + jnp.abs(total - k)
    return cnt, rank1, p0, p1, tied


def _bf16_twice(t):
    bits = pltpu.bitcast(t.astype(bf16).astype(f32), u32) >> 16
    return bits | (bits << 16)


def peer_select(x, wq_b, keys_b, *, tq):
    n, d = x.shape
    nh = PEER_HEADS
    full_spec = pl.BlockSpec((1, tq // LANES, PEER_NKEYS, LANES), lambda i, h: (h, i, 0, 0))
    half_spec = pl.BlockSpec((1, tq // LANES, PEER_NKEYS // 2, LANES), lambda i, h: (h, i, 0, 0))
    full = jax.ShapeDtypeStruct((nh, n // LANES, PEER_NKEYS, LANES), u32)
    half = jax.ShapeDtypeStruct((nh, n // LANES, PEER_NKEYS // 2, LANES), u32)
    return pl.pallas_call(
        _peer_select_kernel,
        grid=(n // tq, nh),
        in_specs=[pl.BlockSpec((tq, d), lambda i, h: (i, 0)),
                  pl.BlockSpec((d, 2 * nh * PEER_DHALF), lambda i, h: (0, 0)),
                  pl.BlockSpec((1, 2, PEER_NKEYS, PEER_DHALF), lambda i, h: (h, 0, 0, 0))],
        out_specs=[full_spec, half_spec, full_spec, half_spec],
        out_shape=[full, half, full, half],
        scratch_shapes=[pltpu.VMEM((tq, 2 * nh * PEER_DHALF), f32)],
        compiler_params=pltpu.CompilerParams(
            dimension_semantics=("arbitrary", "arbitrary"), vmem_limit_bytes=VMEM_LIMIT),
        name="peer_select",
    )(x, wq_b, keys_b)


def _peer_main_kernel(xb_ref, x_ref, cnt_ref, rk1_ref, p0_ref, p1_ref, u_ref, vt_ref,
                      g_ref, b_ref, o_ref, acc_ref, *slab_refs, te, tt):
    j = pl.program_id(1)
    n_i0 = te // PEER_NKEYS
    n_slab = tt // LANES
    per_chunk = MXU_COLS // LANES
    at_refs, ga_refs = slab_refs[:n_slab], slab_refs[n_slab:]
    half = PEER_NKEYS // 2
    unpack = lambda words: pltpu.bitcast(words, bf16)

    @pl.when(j == 0)
    def _():
        acc_ref[...] = jnp.zeros_like(acc_ref)

    def act_chunk(c):
        at = lax.dot_general(u_ref[...], xb_ref[c * MXU_COLS:(c + 1) * MXU_COLS, :],
                             (((1,), (1,)), ((), ())), preferred_element_type=f32)
        for k in range(per_chunk):
            at_refs[c * per_chunk + k][...] = at[:, k * LANES:(k + 1) * LANES]

    def gate_slab(s):
        for il in range(n_i0):
            g = jnp.zeros((PEER_NKEYS, LANES), bf16)
            for h in range(PEER_HEADS):
                cntrow = jnp.broadcast_to(cnt_ref[h, s, il:il + 1, :], (half, LANES))
                p0row = jnp.broadcast_to(p0_ref[h, s, il:il + 1, :], (half, LANES))
                sel = unpack(rk1_ref[h, s]) < unpack(cntrow)
                g = g + jnp.where(sel, unpack(p1_ref[h, s]) * unpack(p0row), jnp.zeros((), bf16))
            ga = _gelu_tanh(at_refs[s][il * PEER_NKEYS:(il + 1) * PEER_NKEYS, :].astype(bf16)) * g
            ga_refs[s][il * half:(il + 1) * half, :] = pltpu.bitcast(ga, u32)

    def out_chunk(c):
        ga = jnp.concatenate([unpack(ga_refs[c * per_chunk + k][...]) for k in range(per_chunk)], axis=1)
        lanes = slice(c * MXU_COLS, (c + 1) * MXU_COLS)
        acc_ref[:, lanes] += jnp.dot(vt_ref[...], ga, preferred_element_type=f32)

    n_chunk = tt // MXU_COLS
    act_chunk(0)
    for c in range(n_chunk):
        gate_slab(c * per_chunk)
        if c + 1 < n_chunk:
            act_chunk(c + 1)
        for k in range(1, per_chunk):
            gate_slab(c * per_chunk + k)
        out_chunk(c)

    @pl.when(j == pl.num_programs(1) - 1)
    def _():
        z = ALPHA * x_ref[...] + acc_ref[...].T
        o_ref[...] = _layer_norm_rows(z, g_ref[...], b_ref[...])


def peer_main(x, xb, stats, u_b, vt_b, ln_g, ln_b, *, tt, te):
    n, d = x.shape
    e = u_b.shape[0]
    nh = PEER_HEADS
    cnt, rk1, p0, p1 = stats
    full_spec = pl.BlockSpec((nh, tt // LANES, te // PEER_NKEYS, LANES), lambda i, j: (0, i, j, 0))
    half_spec = pl.BlockSpec((nh, tt // LANES, PEER_NKEYS // 2, LANES), lambda i, j: (0, i, 0, 0))
    return pl.pallas_call(
        functools.partial(_peer_main_kernel, te=te, tt=tt),
        grid=(n // tt, e // te),
        in_specs=[pl.BlockSpec((tt, d), lambda i, j: (i, 0)),
                  pl.BlockSpec((tt, d), lambda i, j: (i, 0)),
                  full_spec, half_spec, full_spec, half_spec,
                  pl.BlockSpec((te, d), lambda i, j: (j, 0)),
                  pl.BlockSpec((d, te), lambda i, j: (0, j)),
                  pl.BlockSpec((1, d), lambda i, j: (0, 0)),
                  pl.BlockSpec((1, d), lambda i, j: (0, 0))],
        out_specs=pl.BlockSpec((tt, d), lambda i, j: (i, 0)),
        out_shape=jax.ShapeDtypeStruct((n, d), f32),
        scratch_shapes=([pltpu.VMEM((d, tt), f32)]
                        + [pltpu.VMEM((te, LANES), f32)] * (tt // LANES)
                        + [pltpu.VMEM((te // 2, LANES), u32)] * (tt // LANES)),
        compiler_params=pltpu.CompilerParams(
            dimension_semantics=("arbitrary", "arbitrary"), vmem_limit_bytes=VMEM_LIMIT),
        name="peer_main",
    )(xb, x, cnt, rk1, p0, p1, u_b, vt_b, ln_g.reshape(1, d), ln_b.reshape(1, d))


def peer_layer(x, w_q, sub_keys, u, v, ln_g, ln_b, *, tq=256, tt=1024, te=1024):
    stats = peer_select(x, w_q.astype(bf16), sub_keys.astype(bf16), tq=tq)
    return peer_main(x, x.astype(bf16), stats, u.astype(bf16), v.T.astype(bf16), ln_g, ln_b, tt=tt, te=te)


def _rglru_kernel(x_ref, win_ref, cw_ref, vec_ref, wg_ref, wout_ref, lng_ref, lnb_ref, o_ref,
                  tail_ref, h_ref, *, tiles_per_seq):
    i = pl.program_id(0)
    w = RG_WIDTH
    x = x_ref[...]
    ts = x.shape[0]

    @pl.when(i % tiles_per_seq == 0)
    def _():
        tail_ref[...] = jnp.zeros_like(tail_ref)
        h_ref[...] = jnp.zeros_like(h_ref)

    conv_b, b_a, b_x, lam = (vec_ref[n:n + 1, :] for n in range(4))
    hin = jnp.dot(x.astype(bf16), win_ref[...], preferred_element_type=f32)
    gate_branch = _gelu_tanh(hin[:, :w])
    hx = hin[:, w:]

    row = lax.broadcasted_iota(jnp.int32, (ts, w), 0)
    row8 = lax.broadcasted_iota(jnp.int32, (SUBLANES, w), 0)
    tail = tail_ref[...]
    xc = hx * cw_ref[RG_CONV - 1:RG_CONV, :] + conv_b
    for s in range(1, RG_CONV):
        rolled = pltpu.roll(hx, s, axis=0)
        head = jnp.where(row8 < s, pltpu.roll(tail, s, axis=0), rolled[:SUBLANES])
        shifted = jnp.concatenate([head, rolled[SUBLANES:]], axis=0)
        xc = xc + shifted * cw_ref[RG_CONV - 1 - s:RG_CONV - s, :]
    tail_ref[...] = hx[ts - SUBLANES:]

    gates = jnp.dot(xc.astype(bf16), wg_ref[...], preferred_element_type=f32)
    r = _sigmoid(gates[:, :w] + b_a)
    ig = _sigmoid(gates[:, w:] + b_x)
    log_a = (-RG_C * _softplus(-lam)) * r
    a = jnp.exp(log_a)
    u = jnp.sqrt(-jnp.tanh(log_a) * (a * a + 1.0)) * (ig * xc)

    s = 1
    while s < ts:
        keep = row >= s
        a_sh = jnp.where(keep, pltpu.roll(a, s, axis=0), 1.0)
        u_sh = jnp.where(keep, pltpu.roll(u, s, axis=0), 0.0)
        u = a * u_sh + u
        a = a * a_sh
        s *= 2
    hs = u + a * h_ref[SUBLANES - 1:SUBLANES, :]
    h_ref[...] = hs[ts - SUBLANES:]

    m = jnp.dot((hs * gate_branch).astype(bf16), wout_ref[...], preferred_element_type=f32)
    o_ref[...] = _layer_norm_rows(ALPHA * x + m, lng_ref[...], lnb_ref[...])


def rglru_layer(x, seq, w_in, conv_w, conv_b, w_a, b_a, w_x, b_x, lam, w_out, ln_g, ln_b, *, ts=256):
    n, d = x.shape
    w = RG_WIDTH
    blockdiag = lambda t: jax.scipy.linalg.block_diag(*t)
    w_gates = jnp.concatenate([blockdiag(w_a), blockdiag(w_x)], axis=1).astype(bf16)
    vecs = jnp.stack([conv_b, b_a.reshape(w), b_x.reshape(w), lam.reshape(w)])
    consts = (w_in.astype(bf16), conv_w, vecs, w_gates, w_out.astype(bf16), ln_g.reshape(1, d), ln_b.reshape(1, d))
    full = lambda arr: pl.BlockSpec(arr.shape, lambda i: (0,) * arr.ndim, pipeline_mode=pl.Buffered(1))
    tok = pl.BlockSpec((ts, d), lambda i: (i, 0))
    return pl.pallas_call(
        functools.partial(_rglru_kernel, tiles_per_seq=seq // ts),
        grid=(n // ts,),
        in_specs=[tok] + [full(c) for c in consts],
        out_specs=tok,
        out_shape=jax.ShapeDtypeStruct((n, d), f32),
        scratch_shapes=[pltpu.VMEM((SUBLANES, w), f32), pltpu.VMEM((SUBLANES, w), f32)],
        compiler_params=pltpu.CompilerParams(
            dimension_semantics=("arbitrary",), vmem_limit_bytes=VMEM_LIMIT),
        name="rglru",
    )(x, *consts)


def _sigmoid(x):
    return 1.0 / (1.0 + jnp.exp(-x))


def _softplus(x):
    return jnp.maximum(x, 0.0) + jnp.log1p(jnp.exp(-jnp.abs(x)))


def _head_sums(t, ones_ref):
    hi = t.astype(bf16)
    lo = (t - hi.astype(f32)).astype(bf16)
    return (jnp.dot(hi, ones_ref[...], preferred_element_type=f32)
            + jnp.dot(lo, ones_ref[...], preferred_element_type=f32))


def _rwkv_proj_kernel(x_ref, xp_ref, mix_ref, wr_ref, wk_ref, wv_ref, w1_ref, w2_ref, a1_ref, a2_ref,
                      g1_ref, g2_ref, vec_ref,
                      r_ref, w_ref, k_ref, v_ref, gate_ref, g_ref, *, tiles_per_seq):
    i = pl.program_id(0)
    x = x_ref[...]
    ts = x.shape[0]
    prev_last = jnp.where(i % tiles_per_seq == 0, 0.0, xp_ref[SUBLANES - 1:SUBLANES, :])
    row = lax.broadcasted_iota(jnp.int32, x.shape, 0)
    xprev = jnp.where(row == 0, prev_last, pltpu.roll(x, 1, axis=0))
    xx = xprev - x

    def mixed(m):
        return (x + xx * mix_ref[m:m + 1, :]).astype(bf16)

    def mm(a, w_ref_):
        return jnp.dot(a, w_ref_[...], preferred_element_type=f32)

    w0, a0 = (vec_ref[n:n + 1, :] for n in range(2))
    lw = mm(jnp.tanh(mm(mixed(1), w1_ref)).astype(bf16), w2_ref)
    w_log = -_softplus(-(w0 + lw)) - 0.5
    r_ref[...] = mm(mixed(0), wr_ref).astype(bf16)
    w_ref[...] = jnp.exp(-jnp.exp(w_log))
    k_ref[...] = mm(mixed(2), wk_ref).astype(bf16)
    v_ref[...] = mm(mixed(3), wv_ref).astype(bf16)
    gate_ref[...] = _sigmoid(a0 + mm(mm(mixed(4), a1_ref).astype(bf16), a2_ref)).astype(bf16)
    g_ref[...] = mm(_sigmoid(mm(mixed(5), g1_ref)).astype(bf16), g2_ref)


def rwkv_proj(x, seq, mix, w_r, w_k, w_v, w1, w2, a1, a2, g1, g2, vecs, *, ts):
    n, d = x.shape
    full = lambda arr: pl.BlockSpec(arr.shape, lambda i: (0,) * arr.ndim)
    tok = pl.BlockSpec((ts, d), lambda i: (i, 0))
    prev = pl.BlockSpec((SUBLANES, d), lambda i: (jnp.maximum(i * (ts // SUBLANES) - 1, 0), 0))
    weights = (mix, w_r, w_k, w_v, w1, w2, a1, a2, g1, g2, vecs)
    out = lambda dt: jax.ShapeDtypeStruct((n, d), dt)
    return pl.pallas_call(
        functools.partial(_rwkv_proj_kernel, tiles_per_seq=seq // ts),
        grid=(n // ts,),
        in_specs=[tok, prev] + [full(w) for w in weights],
        out_specs=[tok] * 6,
        out_shape=[out(bf16), out(f32), out(bf16), out(bf16), out(bf16), out(f32)],
        compiler_params=pltpu.CompilerParams(
            dimension_semantics=("arbitrary",), vmem_limit_bytes=VMEM_LIMIT),
        name="rwkv_proj",
    )(x, x, *weights)


def _rwkv_scan_kernel(rin_ref, w_ref, k_ref, vin_ref, gate_ref, knext_ref, kk_ref, ka_ref, o_ref,
                      s_ref, sa_ref, a_ref, b_ref, km_ref, r_ref, v_ref):
    nk = s_ref.shape[0]
    tc = rin_ref.shape[0]
    r_ref[...] = rin_ref[...].astype(f32)
    v_ref[...] = vin_ref[...].astype(f32)

    @pl.when(pl.program_id(0) == 0)
    def _():
        s_ref[...] = jnp.zeros_like(s_ref)
        sa_ref[...] = jnp.zeros_like(sa_ref)

    def unit_key(k):
        kk = k * kk_ref[...]
        norm = jnp.sqrt(jnp.sum(kk * kk, axis=-2, keepdims=True))
        return kk / jnp.maximum(norm, RWKV_L2_EPS)

    k_blk = k_ref[...].astype(f32)
    gate = gate_ref[...].astype(f32)
    kk_blk = unit_key(k_blk)
    a_ref[0:tc] = -kk_blk
    a_ref[tc:tc + 1] = -unit_key(knext_ref[...].astype(f32))
    b_ref[...] = kk_blk * gate
    km_ref[...] = k_blk * (1.0 + (gate - 1.0) * ka_ref[...])

    def step(t, sa):
        vt = v_ref[t]
        o = [jnp.zeros(s_ref.shape[1:], f32)] * 2
        nsa = [jnp.zeros(s_ref.shape[1:], f32)] * 2
        for kk in range(nk):
            new = (s_ref[kk] * w_ref[t, kk:kk + 1, :]
                   + (sa * b_ref[t, kk:kk + 1, :] + vt * km_ref[t, kk:kk + 1, :]))
            s_ref[kk] = new
            o[kk % 2] = o[kk % 2] + new * r_ref[t, kk:kk + 1, :]
            nsa[kk % 2] = nsa[kk % 2] + new * a_ref[t + 1, kk:kk + 1, :]
        ot = o[0] + o[1]
        mu = jnp.mean(ot, axis=0, keepdims=True)
        oc = ot - mu
        var = jnp.mean(oc * oc, axis=0, keepdims=True)
        o_ref[t] = oc * lax.rsqrt(var + RWKV_GN_EPS)
        return nsa[0] + nsa[1]

    sa_ref[...] = lax.fori_loop(0, tc, step, sa_ref[...])


def rwkv_scan(r, w, k, v, gate, kk_vec, ka_vec, *, tc):
    s, hd, chains = r.shape
    blk = pl.BlockSpec((tc, hd, chains), lambda i: (i, 0, 0))
    nxt = pl.BlockSpec((1, hd, chains), lambda i: (jnp.minimum((i + 1) * tc, s - 1), 0, 0))
    vec = pl.BlockSpec((hd, chains), lambda i: (0, 0))
    return pl.pallas_call(
        _rwkv_scan_kernel,
        grid=(s // tc,),
        in_specs=[blk] * 5 + [nxt, vec, vec],
        out_specs=blk,
        out_shape=jax.ShapeDtypeStruct((s, hd, chains), f32),
        scratch_shapes=[pltpu.VMEM((hd, hd, chains), f32), pltpu.VMEM((hd, chains), f32),
                        pltpu.VMEM((tc + 1, hd, chains), f32)] + [pltpu.VMEM((tc, hd, chains), f32)] * 4,
        compiler_params=pltpu.CompilerParams(
            dimension_semantics=("arbitrary",), vmem_limit_bytes=VMEM_LIMIT),
        name="rwkv_scan",
    )(r, w, k, v, gate, k, kk_vec, ka_vec)


def _rwkv_out_kernel(x_ref, o_ref, r_ref, k_ref, gate_ref, v_ref, g_ref, vec_ref, ones_ref, wo_ref,
                     lng_ref, lnb_ref, out_ref):
    lnx_g, lnx_b, r_k, k_a = (vec_ref[n:n + 1, :] for n in range(4))
    widen = lambda ref: ref[...].astype(f32)
    k_mod = widen(k_ref) * (1.0 + (widen(gate_ref) - 1.0) * k_a)
    bonus = _head_sums(widen(r_ref) * k_mod * r_k, ones_ref) * widen(v_ref)
    y = (o_ref[...] * lnx_g + lnx_b + bonus) * g_ref[...]
    m = jnp.dot(y.astype(bf16), wo_ref[...], preferred_element_type=f32)
    out_ref[...] = _layer_norm_rows(ALPHA * x_ref[...] + m, lng_ref[...], lnb_ref[...])


def rwkv_out(x, o, r, k, gate, v, g, vecs, ones, w_o, ln_g, ln_b, *, ts):
    n, d = x.shape
    full = lambda arr: pl.BlockSpec(arr.shape, lambda i: (0,) * arr.ndim)
    tok = pl.BlockSpec((ts, d), lambda i: (i, 0))
    consts = (vecs, ones, w_o, ln_g.reshape(1, d), ln_b.reshape(1, d))
    return pl.pallas_call(
        _rwkv_out_kernel,
        grid=(n // ts,),
        in_specs=[tok] * 7 + [full(c) for c in consts],
        out_specs=tok,
        out_shape=jax.ShapeDtypeStruct((n, d), f32),
        compiler_params=pltpu.CompilerParams(
            dimension_semantics=("arbitrary",), vmem_limit_bytes=VMEM_LIMIT),
        name="rwkv_out",
    )(x, o, r, k, gate, v, g, *consts)


def _head_ones(d, head):
    seg = jnp.arange(d) // head
    return (seg[:, None] == seg[None, :]).astype(bf16)


def rwkv_layer(x, seq, mix, w_r, w_k, w_v, w0, w1, w2, a0, a1, a2, g1, g2, k_k, k_a, r_k, lnx_g, lnx_b, w_o,
               ln_g, ln_b, *, ts=256, tc=16):
    n, d = x.shape
    bsz = n // seq
    nh, hd = RWKV_HEADS, RWKV_HEAD
    ones = _head_ones(d, hd)
    c = lambda w: w.astype(bf16)
    r, w, k, v, gate, g = rwkv_proj(x, seq, mix, c(w_r), c(w_k), c(w_v), c(w1), c(w2), c(a1), c(a2),
                                    c(g1), c(g2), jnp.stack([w0, a0]), ts=ts)
    to_chains = lambda t: t.reshape(bsz, seq, nh, hd).transpose(1, 3, 0, 2).reshape(seq, hd, bsz * nh)
    vec_chains = lambda t: jnp.tile(t.reshape(nh, hd).T, (1, bsz))
    o = rwkv_scan(*(to_chains(t) for t in (r, w, k, v, gate)), vec_chains(k_k), vec_chains(k_a), tc=tc)
    o = o.reshape(seq, hd, bsz, nh).transpose(2, 0, 3, 1).reshape(n, d)
    vecs_out = jnp.stack([lnx_g, lnx_b, r_k.reshape(d), k_a])
    return rwkv_out(x, o, r, k, gate, v, g, vecs_out, ones, c(w_o), ln_g, ln_b, ts=ts)


def kernel(x, rg_w_in, rg_conv_w, rg_conv_b, rg_w_a, rg_b_a, rg_w_x, rg_b_x, rg_lambda, rg_w_out, rw_mix, rw_w_r, rw_w_k, rw_w_v, rw_w0, rw_w1, rw_w2, rw_a0, rw_a1, rw_a2, rw_g1, rw_g2, rw_k_k, rw_k_a, rw_r_k, rw_lnx_g, rw_lnx_b, rw_w_o, peer_w_q, peer_sub_keys, peer_u, peer_v, ln_g, ln_b):
    bsz, s, d = x.shape
    x = x.reshape(bsz * s, d)
    for i in range(DEPTH):
        j = i // 2
        if i % 2 == 0:
            x1 = rglru_layer(x, s, rg_w_in[j], rg_conv_w[j], rg_conv_b[j], rg_w_a[j], rg_b_a[j],
                             rg_w_x[j], rg_b_x[j], rg_lambda[j], rg_w_out[j], ln_g[i, 0], ln_b[i, 0])
        else:
            x1 = rwkv_layer(x, s, rw_mix[j], rw_w_r[j], rw_w_k[j], rw_w_v[j], rw_w0[j], rw_w1[j],
                            rw_w2[j], rw_a0[j], rw_a1[j], rw_a2[j], rw_g1[j], rw_g2[j],
                            rw_k_k[j], rw_k_a[j], rw_r_k[j], rw_lnx_g[j], rw_lnx_b[j], rw_w_o[j],
                            ln_g[i, 0], ln_b[i, 0])
        x = peer_layer(x1, peer_w_q[i], peer_sub_keys[i], peer_u[i], peer_v[i], ln_g[i, 1], ln_b[i, 1])
    return x.reshape(bsz, s, d)
```

```python
import functools
import math
import jax, jax.numpy as jnp
from jax import lax
from jax.experimental import pallas as pl
from jax.experimental.pallas import tpu as pltpu

D_MODEL = 1024
DEPTH = 2
RG_WIDTH = 1408
RG_HEADS = 16
RG_BLOCK = 88
RG_CONV = 4
RG_C = 8.0
RWKV_HEAD = 64
RWKV_HEADS = 16
RWKV_GN_EPS = 64e-5
RWKV_L2_EPS = 1e-12
PEER_HEADS = 8
PEER_NKEYS = 128
PEER_DHALF = 128
PEER_TOPK = 16
PEER_BLOCK = 128
ALPHA = (2 * DEPTH) ** 0.25
LN_EPS = 1e-5

LANES = 128
SUBLANES = 8
MXU_COLS = 256
VMEM_LIMIT = 56 << 20

f32 = jnp.float32
bf16 = jnp.bfloat16
u32 = jnp.uint32


def _gelu_tanh(x):
    return 0.5 * x * (1.0 + jnp.tanh(math.sqrt(2.0 / math.pi) * (x + 0.044715 * (x * x * x))))


def _layer_norm_rows(z, g, b):
    mu = jnp.mean(z, axis=-1, keepdims=True)
    zc = z - mu
    var = jnp.mean(zc * zc, axis=-1, keepdims=True)
    return zc * lax.rsqrt(var + LN_EPS) * g + b


def _top_rows(s, k, with_rank=False):
    rows = []
    rank = jnp.full(s.shape, float(k), f32)
    for j in range(k):
        m = jnp.max(s, axis=0, keepdims=True)
        rows.append(m)
        hit = s == m
        if with_rank:
            rank = jnp.where(hit, float(j), rank)
        s = jnp.where(hit, -jnp.inf, s)
    return (rows, rank) if with_rank else rows


def _peer_select_kernel(x_ref, wq_ref, keys_ref, cnt_ref, rk1_ref, p0_ref, p1_ref, q_ref):
    h = pl.program_id(1)

    @pl.when(h == 0)
    def _():
        q_ref[...] = jnp.dot(x_ref[...].astype(bf16), wq_ref[...], preferred_element_type=f32)

    def scores(p):
        off = pl.multiple_of((2 * h + p) * PEER_DHALF, PEER_DHALF)
        qhp = q_ref[:, pl.ds(off, PEER_DHALF)].astype(bf16)
        return lax.dot_general(keys_ref[0, p], qhp, (((1,), (1,)), ((), ())),
                               preferred_element_type=f32)

    def select_all(stable):
        s0_all = scores(0)
        s1_all = scores(1)
        tied = 0.0
        for c in range(s0_all.shape[1] // LANES):
            lanes = slice(c * LANES, (c + 1) * LANES)
            cnt, rank1, p0, p1, tied_c = _select_slab(s0_all[:, lanes], s1_all[:, lanes], stable=stable)
            cnt_ref[0, c] = _bf16_twice(cnt)
            p0_ref[0, c] = _bf16_twice(p0)
            rk1_ref[0, c] = pltpu.bitcast(rank1.astype(bf16), u32)
            p1_ref[0, c] = pltpu.bitcast(p1.astype(bf16), u32)
            if not stable:
                tied = jnp.maximum(tied, jnp.max(tied_c))
        return tied

    tied = select_all(stable=False)

    @pl.when(tied > 0.0)
    def _():
        select_all(stable=True)


def _top_rows_stable(s, k):
    rows = []
    idx = lax.broadcasted_iota(jnp.int32, s.shape, 0)
    rank = jnp.full(s.shape, float(k), f32)
    for j in range(k):
        m = jnp.max(s, axis=0, keepdims=True)
        rows.append(m)
        first = jnp.min(jnp.where(s == m, idx, s.shape[0]), axis=0, keepdims=True)
        pick = idx == first
        rank = jnp.where(pick, float(j), rank)
        s = jnp.where(pick, -jnp.inf, s)
    return rows, rank


def _select_slab(s0, s1, *, stable):
    if stable:
        a, rank0 = _top_rows_stable(s0, PEER_TOPK)
        b, rank1 = _top_rows_stable(s1, PEER_TOPK)
    else:
        a = _top_rows(s0, PEER_TOPK)
        b, rank1 = _top_rows(s1, PEER_TOPK, with_rank=True)
    row = lax.broadcasted_iota(jnp.int32, (PEER_TOPK, s0.shape[1]), 0)
    bmat = jnp.zeros((PEER_TOPK, s0.shape[1]), f32)
    for j in range(PEER_TOPK):
        bmat = jnp.where(row == j, b[j], bmat)
    cands = [a[0] + bmat] + [a[i] + bmat[:SUBLANES] for i in range(1, PEER_TOPK)]
    cand = jnp.concatenate(cands, axis=0)
    if stable:
        picked = _top_rows_stable(cand, PEER_TOPK)[1] < float(PEER_TOPK)
    else:
        picked = cand >= _top_rows(cand, PEER_TOPK)[-1]
    picked = jnp.where(picked, 1.0, 0.0)
    cmax = a[0] + b[0]
    z = jnp.sum(picked * jnp.exp(cand - cmax), axis=0, keepdims=True)
    cnt = jnp.zeros_like(s0)
    total = jnp.zeros_like(z)
    lo = 0
    for i in range(PEER_TOPK):
        hi = lo + cands[i].shape[0]
        cnt_i = jnp.sum(picked[lo:hi], axis=0, keepdims=True)
        lo = hi
        total = total + cnt_i
        cnt = jnp.where((rank0 == float(i)) if stable else (s0 == a[i]), cnt_i, cnt)
    p0 = jnp.exp(s0 - a[0]) / z
    p1 = jnp.exp(s1 - b[0])
    if stable:
        return cnt, rank1, p0, p1, None
    k = float(PEER_TOPK)
    n0 = jnp.sum(jnp.where(s0 >= a[-1], 1.0, 0.0), axis=0, keepdims=True)
    n1 = jnp.sum(jnp.where(rank1 < k, 1.0, 0.0), axis=0, keepdims=True)
    tied = jnp.abs(n0 - k) + jnp.abs(n1 - k) + jnp.abs(total - k)
    return cnt, rank1, p0, p1, tied


def _bf16_twice(t):
    bits = pltpu.bitcast(t.astype(bf16).astype(f32), u32) >> 16
    return bits | (bits << 16)


def peer_select(x, wq_b, keys_b, *, tq):
    n, d = x.shape
    nh = PEER_HEADS
    full_spec = pl.BlockSpec((1, tq // LANES, PEER_NKEYS, LANES), lambda i, h: (h, i, 0, 0))
    half_spec = pl.BlockSpec((1, tq // LANES, PEER_NKEYS // 2, LANES), lambda i, h: (h, i, 0, 0))
    full = jax.ShapeDtypeStruct((nh, n // LANES, PEER_NKEYS, LANES), u32)
    half = jax.ShapeDtypeStruct((nh, n // LANES, PEER_NKEYS // 2, LANES), u32)
    return pl.pallas_call(
        _peer_select_kernel,
        grid=(n // tq, nh),
        in_specs=[pl.BlockSpec((tq, d), lambda i, h: (i, 0)),
                  pl.BlockSpec((d, 2 * nh * PEER_DHALF), lambda i, h: (0, 0)),
                  pl.BlockSpec((1, 2, PEER_NKEYS, PEER_DHALF), lambda i, h: (h, 0, 0, 0))],
        out_specs=[full_spec, half_spec, full_spec, half_spec],
        out_shape=[full, half, full, half],
        scratch_shapes=[pltpu.VMEM((tq, 2 * nh * PEER_DHALF), f32)],
        compiler_params=pltpu.CompilerParams(
            dimension_semantics=("arbitrary", "arbitrary"), vmem_limit_bytes=VMEM_LIMIT),
        name="peer_select",
    )(x, wq_b, keys_b)


def _peer_main_kernel(xb_ref, x_ref, cnt_ref, rk1_ref, p0_ref, p1_ref, u_ref, vt_ref,
                      g_ref, b_ref, o_ref, acc_ref, *slab_refs, te, tt):
    j = pl.program_id(1)
    n_i0 = te // PEER_NKEYS
    n_slab = tt // LANES
    per_chunk = MXU_COLS // LANES
    at_refs, ga_refs = slab_refs[:n_slab], slab_refs[n_slab:]
    half = PEER_NKEYS // 2
    unpack = lambda words: pltpu.bitcast(words, bf16)

    @pl.when(j == 0)
    def _():
        acc_ref[...] = jnp.zeros_like(acc_ref)

    def act_chunk(c):
        at = lax.dot_general(u_ref[...], xb_ref[c * MXU_COLS:(c + 1) * MXU_COLS, :],
                             (((1,), (1,)), ((), ())), preferred_element_type=f32)
        for k in range(per_chunk):
            at_refs[c * per_chunk + k][...] = at[:, k * LANES:(k + 1) * LANES]

    def gate_slab(s):
        for il in range(n_i0):
            g = jnp.zeros((PEER_NKEYS, LANES), bf16)
            for h in range(PEER_HEADS):
                cntrow = jnp.broadcast_to(cnt_ref[h, s, il:il + 1, :], (half, LANES))
                p0row = jnp.broadcast_to(p0_ref[h, s, il:il + 1, :], (half, LANES))
                sel = unpack(rk1_ref[h, s]) < unpack(cntrow)
                g = g + jnp.where(sel, unpack(p1_ref[h, s]) * unpack(p0row), jnp.zeros((), bf16))
            ga = _gelu_tanh(at_refs[s][il * PEER_NKEYS:(il + 1) * PEER_NKEYS, :].astype(bf16)) * g
            ga_refs[s][il * half:(il + 1) * half, :] = pltpu.bitcast(ga, u32)

    def out_chunk(c):
        ga = jnp.concatenate([unpack(ga_refs[c * per_chunk + k][...]) for k in range(per_chunk)], axis=1)
        lanes = slice(c * MXU_COLS, (c + 1) * MXU_COLS)
        acc_ref[:, lanes] += jnp.dot(vt_ref[...], ga, preferred_element_type=f32)

    n_chunk = tt // MXU_COLS
    act_chunk(0)
    for c in range(n_chunk):
        gate_slab(c * per_chunk)
        if c + 1 < n_chunk:
            act_chunk(c + 1)
        for k in range(1, per_chunk):
            gate_slab(c * per_chunk + k)
        out_chunk(c)

    @pl.when(j == pl.num_programs(1) - 1)
    def _():
        z = ALPHA * x_ref[...] + acc_ref[...].T
        o_ref[...] = _layer_norm_rows(z, g_ref[...], b_ref[...])


def peer_main(x, xb, stats, u_b, vt_b, ln_g, ln_b, *, tt, te):
    n, d = x.shape
    e = u_b.shape[0]
    nh = PEER_HEADS
    cnt, rk1, p0, p1 = stats
    full_spec = pl.BlockSpec((nh, tt // LANES, te // PEER_NKEYS, LANES), lambda i, j: (0, i, j, 0))
    half_spec = pl.BlockSpec((nh, tt // LANES, PEER_NKEYS // 2, LANES), lambda i, j: (0, i, 0, 0))
    return pl.pallas_call(
        functools.partial(_peer_main_kernel, te=te, tt=tt),
        grid=(n // tt, e // te),
        in_specs=[pl.BlockSpec((tt, d), lambda i, j: (i, 0)),
                  pl.BlockSpec((tt, d), lambda i, j: (i, 0)),
                  full_spec, half_spec, full_spec, half_spec,
                  pl.BlockSpec((te, d), lambda i, j: (j, 0)),
                  pl.BlockSpec((d, te), lambda i, j: (0, j)),
                  pl.BlockSpec((1, d), lambda i, j: (0, 0)),
                  pl.BlockSpec((1, d), lambda i, j: (0, 0))],
        out_specs=pl.BlockSpec((tt, d), lambda i, j: (i, 0)),
        out_shape=jax.ShapeDtypeStruct((n, d), f32),
        scratch_shapes=([pltpu.VMEM((d, tt), f32)]
                        + [pltpu.VMEM((te, LANES), f32)] * (tt // LANES)
                        + [pltpu.VMEM((te // 2, LANES), u32)] * (tt // LANES)),
        compiler_params=pltpu.CompilerParams(
            dimension_semantics=("arbitrary", "arbitrary"), vmem_limit_bytes=VMEM_LIMIT),
        name="peer_main",
    )(xb, x, cnt, rk1, p0, p1, u_b, vt_b, ln_g.reshape(1, d), ln_b.reshape(1, d))


def peer_layer(x, w_q, sub_keys, u, v, ln_g, ln_b, *, tq=256, tt=1024, te=1024):
    stats = peer_select(x, w_q.astype(bf16), sub_keys.astype(bf16), tq=tq)
    return peer_main(x, x.astype(bf16), stats, u.astype(bf16), v.T.astype(bf16), ln_g, ln_b, tt=tt, te=te)


def _rglru_kernel(x_ref, win_ref, cw_ref, vec_ref, wg_ref, wout_ref, lng_ref, lnb_ref, o_ref,
                  tail_ref, h_ref, *, tiles_per_seq):
    i = pl.program_id(0)
    w = RG_WIDTH
    x = x_ref[...]
    ts = x.shape[0]

    @pl.when(i % tiles_per_seq == 0)
    def _():
        tail_ref[...] = jnp.zeros_like(tail_ref)
        h_ref[...] = jnp.zeros_like(h_ref)

    conv_b, b_a, b_x, lam = (vec_ref[n:n + 1, :] for n in range(4))
    hin = jnp.dot(x.astype(bf16), win_ref[...], preferred_element_type=f32)
    gate_branch = _gelu_tanh(hin[:, :w])
    hx = hin[:, w:]

    row = lax.broadcasted_iota(jnp.int32, (ts, w), 0)
    row8 = lax.broadcasted_iota(jnp.int32, (SUBLANES, w), 0)
    tail = tail_ref[...]
    xc = hx * cw_ref[RG_CONV - 1:RG_CONV, :] + conv_b
    for s in range(1, RG_CONV):
        rolled = pltpu.roll(hx, s, axis=0)
        head = jnp.where(row8 < s, pltpu.roll(tail, s, axis=0), rolled[:SUBLANES])
        shifted = jnp.concatenate([head, rolled[SUBLANES:]], axis=0)
        xc = xc + shifted * cw_ref[RG_CONV - 1 - s:RG_CONV - s, :]
    tail_ref[...] = hx[ts - SUBLANES:]

    gates = jnp.dot(xc.astype(bf16), wg_ref[...], preferred_element_type=f32)
    r = _sigmoid(gates[:, :w] + b_a)
    ig = _sigmoid(gates[:, w:] + b_x)
    log_a = (-RG_C * _softplus(-lam)) * r
    a = jnp.exp(log_a)
    u = jnp.sqrt(-jnp.tanh(log_a) * (a * a + 1.0)) * (ig * xc)

    s = 1
    while s < ts:
        keep = row >= s
        a_sh = jnp.where(keep, pltpu.roll(a, s, axis=0), 1.0)
        u_sh = jnp.where(keep, pltpu.roll(u, s, axis=0), 0.0)
        u = a * u_sh + u
        a = a * a_sh
        s *= 2
    hs = u + a * h_ref[SUBLANES - 1:SUBLANES, :]
    h_ref[...] = hs[ts - SUBLANES:]

    m = jnp.dot((hs * gate_branch).astype(bf16), wout_ref[...], preferred_element_type=f32)
    o_ref[...] = _layer_norm_rows(ALPHA * x + m, lng_ref[...], lnb_ref[...])


def rglru_layer(x, seq, w_in, conv_w, conv_b, w_a, b_a, w_x, b_x, lam, w_out, ln_g, ln_b, *, ts=256):
    n, d = x.shape
    w = RG_WIDTH
    blockdiag = lambda t: jax.scipy.linalg.block_diag(*t)
    w_gates = jnp.concatenate([blockdiag(w_a), blockdiag(w_x)], axis=1).astype(bf16)
    vecs = jnp.stack([conv_b, b_a.reshape(w), b_x.reshape(w), lam.reshape(w)])
    consts = (w_in.astype(bf16), conv_w, vecs, w_gates, w_out.astype(bf16), ln_g.reshape(1, d), ln_b.reshape(1, d))
    full = lambda arr: pl.BlockSpec(arr.shape, lambda i: (0,) * arr.ndim, pipeline_mode=pl.Buffered(1))
    tok = pl.BlockSpec((ts, d), lambda i: (i, 0))
    return pl.pallas_call(
        functools.partial(_rglru_kernel, tiles_per_seq=seq // ts),
        grid=(n // ts,),
        in_specs=[tok] + [full(c) for c in consts],
        out_specs=tok,
        out_shape=jax.ShapeDtypeStruct((n, d), f32),
        scratch_shapes=[pltpu.VMEM((SUBLANES, w), f32), pltpu.VMEM((SUBLANES, w), f32)],
        compiler_params=pltpu.CompilerParams(
            dimension_semantics=("arbitrary",), vmem_limit_bytes=VMEM_LIMIT),
        name="rglru",
    )(x, *consts)


def _sigmoid(x):
    return 1.0 / (1.0 + jnp.exp(-x))


def _softplus(x):
    return jnp.maximum(x, 0.0) + jnp.log1p(jnp.exp(-jnp.abs(x)))


def _head_sums(t, ones_ref):
    hi = t.astype(bf16)
    lo = (t - hi.astype(f32)).astype(bf16)
    return (jnp.dot(hi, ones_ref[...], preferred_element_type=f32)
            + jnp.dot(lo, ones_ref[...], preferred_element_type=f32))


def _rwkv_proj_kernel(x_ref, xp_ref, mix_ref, wr_ref, wk_ref, wv_ref, w1_ref, w2_ref, a1_ref, a2_ref,
                      g1_ref, g2_ref, vec_ref,
                      r_ref, w_ref, k_ref, v_ref, gate_ref, g_ref, *, tiles_per_seq):
    i = pl.program_id(0)
    x = x_ref[...]
    ts = x.shape[0]
    prev_last = jnp.where(i % tiles_per_seq == 0, 0.0, xp_ref[SUBLANES - 1:SUBLANES, :])
    row = lax.broadcasted_iota(jnp.int32, x.shape, 0)
    xprev = jnp.where(row == 0, prev_last, pltpu.roll(x, 1, axis=0))
    xx = xprev - x

    def mixed(m):
        return (x + xx * mix_ref[m:m + 1, :]).astype(bf16)

    def mm(a, w_ref_):
        return jnp.dot(a, w_ref_[...], preferred_element_type=f32)

    w0, a0 = (vec_ref[n:n + 1, :] for n in range(2))
    lw = mm(jnp.tanh(mm(mixed(1), w1_ref)).astype(bf16), w2_ref)
    w_log = -_softplus(-(w0 + lw)) - 0.5
    r_ref[...] = mm(mixed(0), wr_ref).astype(bf16)
    w_ref[...] = jnp.exp(w_log).astype(bf16)
    k_ref[...] = mm(mixed(2), wk_ref).astype(bf16)
    v_ref[...] = mm(mixed(3), wv_ref).astype(bf16)
    gate_ref[...] = _sigmoid(a0 + mm(mm(mixed(4), a1_ref).astype(bf16), a2_ref)).astype(bf16)
    g_ref[...] = mm(_sigmoid(mm(mixed(5), g1_ref)).astype(bf16), g2_ref)


def rwkv_proj(x, seq, mix, w_r, w_k, w_v, w1, w2, a1, a2, g1, g2, vecs, *, ts):
    n, d = x.shape
    full = lambda arr: pl.BlockSpec(arr.shape, lambda i: (0,) * arr.ndim)
    tok = pl.BlockSpec((ts, d), lambda i: (i, 0))
    prev = pl.BlockSpec((SUBLANES, d), lambda i: (jnp.maximum(i * (ts // SUBLANES) - 1, 0), 0))
    weights = (mix, w_r, w_k, w_v, w1, w2, a1, a2, g1, g2, vecs)
    out = lambda dt: jax.ShapeDtypeStruct((n, d), dt)
    return pl.pallas_call(
        functools.partial(_rwkv_proj_kernel, tiles_per_seq=seq // ts),
        grid=(n // ts,),
        in_specs=[tok, prev] + [full(w) for w in weights],
        out_specs=[tok] * 6,
        out_shape=[out(bf16), out(bf16), out(bf16), out(bf16), out(bf16), out(f32)],
        compiler_params=pltpu.CompilerParams(
            dimension_semantics=("arbitrary",), vmem_limit_bytes=VMEM_LIMIT),
        name="rwkv_proj",
    )(x, x, *weights)


def _rwkv_scan_kernel(rin_ref, win_ref, k_ref, vin_ref, gate_ref, knext_ref, kk_ref, ka_ref, o_ref,
                      s_ref, sa_ref, a_ref, b_ref, km_ref, r_ref, v_ref, w_ref):
    nk = s_ref.shape[0]
    tc = rin_ref.shape[0]
    r_ref[...] = rin_ref[...].astype(f32)
    v_ref[...] = vin_ref[...].astype(f32)
    w_ref[...] = jnp.exp(-win_ref[...].astype(f32))

    @pl.when(pl.program_id(0) == 0)
    def _():
        s_ref[...] = jnp.zeros_like(s_ref)
        sa_ref[...] = jnp.zeros_like(sa_ref)

    def unit_key(k):
        kk = k * kk_ref[...]
        norm = jnp.sqrt(jnp.sum(kk * kk, axis=-2, keepdims=True))
        return kk / jnp.maximum(norm, RWKV_L2_EPS)

    k_blk = k_ref[...].astype(f32)
    gate = gate_ref[...].astype(f32)
    kk_blk = unit_key(k_blk)
    a_ref[0:tc] = -kk_blk
    a_ref[tc:tc + 1] = -unit_key(knext_ref[...].astype(f32))
    b_ref[...] = kk_blk * gate
    km_ref[...] = k_blk * (1.0 + (gate - 1.0) * ka_ref[...])

    def step(t, sa):
        vt = v_ref[t]
        o = [jnp.zeros(s_ref.shape[1:], f32)] * 2
        nsa = [jnp.zeros(s_ref.shape[1:], f32)] * 2
        for kk in range(nk):
            new = (s_ref[kk] * w_ref[t, kk:kk + 1, :]
                   + (sa * b_ref[t, kk:kk + 1, :] + vt * km_ref[t, kk:kk + 1, :]))
            s_ref[kk] = new
            o[kk % 2] = o[kk % 2] + new * r_ref[t, kk:kk + 1, :]
            nsa[kk % 2] = nsa[kk % 2] + new * a_ref[t + 1, kk:kk + 1, :]
        ot = o[0] + o[1]
        mu = jnp.mean(ot, axis=0, keepdims=True)
        oc = ot - mu
        var = jnp.mean(oc * oc, axis=0, keepdims=True)
        o_ref[t] = (oc * lax.rsqrt(var + RWKV_GN_EPS)).astype(o_ref.dtype)
        return nsa[0] + nsa[1]

    sa_ref[...] = lax.fori_loop(0, tc, step, sa_ref[...])


def rwkv_scan(r, w, k, v, gate, kk_vec, ka_vec, *, tc):
    s, hd, chains = r.shape
    blk = pl.BlockSpec((tc, hd, chains), lambda i: (i, 0, 0))
    nxt = pl.BlockSpec((1, hd, chains), lambda i: (jnp.minimum((i + 1) * tc, s - 1), 0, 0))
    vec = pl.BlockSpec((hd, chains), lambda i: (0, 0))
    return pl.pallas_call(
        _rwkv_scan_kernel,
        grid=(s // tc,),
        in_specs=[blk] * 5 + [nxt, vec, vec],
        out_specs=blk,
        out_shape=jax.ShapeDtypeStruct((s, hd, chains), bf16),
        scratch_shapes=[pltpu.VMEM((hd, hd, chains), f32), pltpu.VMEM((hd, chains), f32),
                        pltpu.VMEM((tc + 1, hd, chains), f32)] + [pltpu.VMEM((tc, hd, chains), f32)] * 5,
        compiler_params=pltpu.CompilerParams(
            dimension_semantics=("arbitrary",), vmem_limit_bytes=VMEM_LIMIT),
        name="rwkv_scan",
    )(r, w, k, v, gate, k, kk_vec, ka_vec)


def _rwkv_out_kernel(x_ref, o_ref, r_ref, k_ref, gate_ref, v_ref, g_ref, vec_ref, ones_ref, wo_ref,
                     lng_ref, lnb_ref, out_ref):
    lnx_g, lnx_b, r_k, k_a = (vec_ref[n:n + 1, :] for n in range(4))
    widen = lambda ref: ref[...].astype(f32)
    k_mod = widen(k_ref) * (1.0 + (widen(gate_ref) - 1.0) * k_a)
    bonus = _head_sums(widen(r_ref) * k_mod * r_k, ones_ref) * widen(v_ref)
    y = (widen(o_ref) * lnx_g + lnx_b + bonus) * g_ref[...]
    m = jnp.dot(y.astype(bf16), wo_ref[...], preferred_element_type=f32)
    out_ref[...] = _layer_norm_rows(ALPHA * x_ref[...] + m, lng_ref[...], lnb_ref[...])


def rwkv_out(x, o, r, k, gate, v, g, vecs, ones, w_o, ln_g, ln_b, *, ts):
    n, d = x.shape
    full = lambda arr: pl.BlockSpec(arr.shape, lambda i: (0,) * arr.ndim)
    tok = pl.BlockSpec((ts, d), lambda i: (i, 0))
    consts = (vecs, ones, w_o, ln_g.reshape(1, d), ln_b.reshape(1, d))
    return pl.pallas_call(
        _rwkv_out_kernel,
        grid=(n // ts,),
        in_specs=[tok] * 7 + [full(c) for c in consts],
        out_specs=tok,
        out_shape=jax.ShapeDtypeStruct((n, d), f32),
        compiler_params=pltpu.CompilerParams(
            dimension_semantics=("arbitrary",), vmem_limit_bytes=VMEM_LIMIT),
        name="rwkv_out",
    )(x, o, r, k, gate, v, g, *consts)


def _head_ones(d, head):
    seg = jnp.arange(d) // head
    return (seg[:, None] == seg[None, :]).astype(bf16)


def rwkv_layer(x, seq, mix, w_r, w_k, w_v, w0, w1, w2, a0, a1, a2, g1, g2, k_k, k_a, r_k, lnx_g, lnx_b, w_o,
               ln_g, ln_b, *, ts=256, tc=16):
    n, d = x.shape
    bsz = n // seq
    nh, hd = RWKV_HEADS, RWKV_HEAD
    ones = _head_ones(d, hd)
    c = lambda w: w.astype(bf16)
    r, w, k, v, gate, g = rwkv_proj(x, seq, mix, c(w_r), c(w_k), c(w_v), c(w1), c(w2), c(a1), c(a2),
                                    c(g1), c(g2), jnp.stack([w0, a0]), ts=ts)
    to_chains = lambda t: t.reshape(bsz, seq, nh, hd).transpose(1, 3, 0, 2).reshape(seq, hd, bsz * nh)
    vec_chains = lambda t: jnp.tile(t.reshape(nh, hd).T, (1, bsz))
    o = rwkv_scan(*(to_chains(t) for t in (r, w, k, v, gate)), vec_chains(k_k), vec_chains(k_a), tc=tc)
    o = o.reshape(seq, hd, bsz, nh).transpose(2, 0, 3, 1).reshape(n, d)
    vecs_out = jnp.stack([lnx_g, lnx_b, r_k.reshape(d), k_a])
    return rwkv_out(x, o, r, k, gate, v, g, vecs_out, ones, c(w_o), ln_g, ln_b, ts=ts)


def kernel(x, rg_w_in, rg_conv_w, rg_conv_b, rg_w_a, rg_b_a, rg_w_x, rg_b_x, rg_lambda, rg_w_out, rw_mix, rw_w_r, rw_w_k, rw_w_v, rw_w0, rw_w1, rw_w2, rw_a0, rw_a1, rw_a2, rw_g1, rw_g2, rw_k_k, rw_k_a, rw_r_k, rw_lnx_g, rw_lnx_b, rw_w_o, peer_w_q, peer_sub_keys, peer_u, peer_v, ln_g, ln_b):
    bsz, s, d = x.shape
    x = x.reshape(bsz * s, d)
    for i in range(DEPTH):
        j = i // 2
        if i % 2 == 0:
            x1 = rglru_layer(x, s, rg_w_in[j], rg_conv_w[j], rg_conv_b[j], rg_w_a[j], rg_b_a[j],
                             rg_w_x[j], rg_b_x[j], rg_lambda[j], rg_w_out[j], ln_g[i, 0], ln_b[i, 0])
        else:
            x1 = rwkv_layer(x, s, rw_mix[j], rw_w_r[j], rw_w_k[j], rw_w_v[j], rw_w0[j], rw_w1[j],
                            rw_w2[j], rw_a0[j], rw_a1[j], rw_a2[j], rw_g1[j], rw_g2[j],
                            rw_k_k[j], rw_k_a[j], rw_r_k[j], rw_lnx_g[j], rw_lnx_b[j], rw_w_o[j],
                            ln_g[i, 0], ln_b[i, 0])
        x = peer_layer(x1, peer_w_q[i], peer_sub_keys[i], peer_u[i], peer_v[i], ln_g[i, 1], ln_b[i, 1])
    return x.reshape(bsz, s, d)
```

```python
import functools
import math
import jax, jax.numpy as jnp
from jax import lax
from jax.experimental import pallas as pl
from jax.experimental.pallas import tpu as pltpu

D_MODEL = 1024
DEPTH = 2
RG_WIDTH = 1408
RG_HEADS = 16
RG_BLOCK = 88
RG_CONV = 4
RG_C = 8.0
RWKV_HEAD = 64
RWKV_HEADS = 16
RWKV_GN_EPS = 64e-5
RWKV_L2_EPS = 1e-12
PEER_HEADS = 8
PEER_NKEYS = 128
PEER_DHALF = 128
PEER_TOPK = 16
PEER_BLOCK = 128
ALPHA = (2 * DEPTH) ** 0.25
LN_EPS = 1e-5

LANES = 128
SUBLANES = 8
MXU_COLS = 256
VMEM_LIMIT = 56 << 20

f32 = jnp.float32
bf16 = jnp.bfloat16
u32 = jnp.uint32


def _gelu_tanh(x):
    return 0.5 * x * (1.0 + jnp.tanh(math.sqrt(2.0 / math.pi) * (x + 0.044715 * (x * x * x))))


def _layer_norm_rows(z, g, b):
    mu = jnp.mean(z, axis=-1, keepdims=True)
    zc = z - mu
    var = jnp.mean(zc * zc, axis=-1, keepdims=True)
    return zc * lax.rsqrt(var + LN_EPS) * g + b


def _top_rows(s, k, with_rank=False):
    rows = []
    rank = jnp.full(s.shape, float(k), f32)
    for j in range(k):
        m = jnp.max(s, axis=0, keepdims=True)
        rows.append(m)
        hit = s == m
        if with_rank:
            rank = jnp.where(hit, float(j), rank)
        s = jnp.where(hit, -jnp.inf, s)
    return (rows, rank) if with_rank else rows


def _peer_select_kernel(x_ref, wq_ref, keys_ref, cnt_ref, rk1_ref, p0_ref, p1_ref, q_ref):
    h = pl.program_id(1)

    @pl.when(h == 0)
    def _():
        q_ref[...] = jnp.dot(x_ref[...].astype(bf16), wq_ref[...], preferred_element_type=f32)

    def scores(p):
        off = pl.multiple_of((2 * h + p) * PEER_DHALF, PEER_DHALF)
        qhp = q_ref[:, pl.ds(off, PEER_DHALF)].astype(bf16)
        return lax.dot_general(keys_ref[0, p], qhp, (((1,), (1,)), ((), ())),
                               preferred_element_type=f32)

    def select_all(stable):
        s0_all = scores(0)
        s1_all = scores(1)
        tied = 0.0
        for c in range(s0_all.shape[1] // LANES):
            lanes = slice(c * LANES, (c + 1) * LANES)
            cnt, rank1, p0, p1, tied_c = _select_slab(s0_all[:, lanes], s1_all[:, lanes], stable=stable)
            cnt_ref[0, c] = _bf16_twice(cnt)
            p0_ref[0, c] = _bf16_twice(p0)
            rk1_ref[0, c] = pltpu.bitcast(rank1.astype(bf16), u32)
            p1_ref[0, c] = pltpu.bitcast(p1.astype(bf16), u32)
            if not stable:
                tied = jnp.maximum(tied, jnp.max(tied_c))
        return tied

    tied = select_all(stable=False)

    @pl.when(tied > 0.0)
    def _():
        select_all(stable=True)


def _top_rows_stable(s, k):
    rows = []
    idx = lax.broadcasted_iota(jnp.int32, s.shape, 0)
    rank = jnp.full(s.shape, float(k), f32)
    for j in range(k):
        m = jnp.max(s, axis=0, keepdims=True)
        rows.append(m)
        first = jnp.min(jnp.where(s == m, idx, s.shape[0]), axis=0, keepdims=True)
        pick = idx == first
        rank = jnp.where(pick, float(j), rank)
        s = jnp.where(pick, -jnp.inf, s)
    return rows, rank


def _select_slab(s0, s1, *, stable):
    if stable:
        a, rank0 = _top_rows_stable(s0, PEER_TOPK)
        b, rank1 = _top_rows_stable(s1, PEER_TOPK)
    else:
        a = _top_rows(s0, PEER_TOPK)
        b, rank1 = _top_rows(s1, PEER_TOPK, with_rank=True)
    row = lax.broadcasted_iota(jnp.int32, (PEER_TOPK, s0.shape[1]), 0)
    bmat = jnp.zeros((PEER_TOPK, s0.shape[1]), f32)
    for j in range(PEER_TOPK):
        bmat = jnp.where(row == j, b[j], bmat)
    row8 = lax.broadcasted_iota(jnp.int32, (SUBLANES, s0.shape[1]), 0)
    ahi = jnp.zeros((SUBLANES, s0.shape[1]), f32)
    for n in range(SUBLANES):
        ahi = jnp.where(row8 == n, a[SUBLANES + n], ahi)
    cand = jnp.concatenate([a[0] + bmat] + [a[i] + bmat[:SUBLANES] for i in range(1, SUBLANES)]
                           + [ahi + b[0]], axis=0)
    rows_of = [PEER_TOPK] + [SUBLANES] * (SUBLANES - 1) + [1] * SUBLANES
    if stable:
        picked = _top_rows_stable(cand, PEER_TOPK)[1] < float(PEER_TOPK)
    else:
        picked = cand >= _top_rows(cand, PEER_TOPK)[-1]
    picked = jnp.where(picked, 1.0, 0.0)
    cmax = a[0] + b[0]
    z = jnp.sum(picked * jnp.exp(cand - cmax), axis=0, keepdims=True)
    cnt = jnp.zeros_like(s0)
    total = jnp.zeros_like(z)
    lo = 0
    for i in range(PEER_TOPK):
        hi = lo + rows_of[i]
        cnt_i = jnp.sum(picked[lo:hi], axis=0, keepdims=True)
        lo = hi
        total = total + cnt_i
        cnt = jnp.where((rank0 == float(i)) if stable else (s0 == a[i]), cnt_i, cnt)
    p0 = jnp.exp(s0 - a[0]) / z
    p1 = jnp.exp(s1 - b[0])
    if stable:
        return cnt, rank1, p0, p1, None
    k = float(PEER_TOPK)
    n0 = jnp.sum(jnp.where(s0 >= a[-1], 1.0, 0.0), axis=0, keepdims=True)
    n1 = jnp.sum(jnp.where(rank1 < k, 1.0, 0.0), axis=0, keepdims=True)
    tied = jnp.abs(n0 - k) + jnp.abs(n1 - k) + jnp.abs(total - k)
    return cnt, rank1, p0, p1, tied


def _bf16_twice(t):
    bits = pltpu.bitcast(t.astype(bf16).astype(f32), u32) >> 16
    return bits | (bits << 16)


def peer_select(x, wq_b, keys_b, *, tq):
    n, d = x.shape
    nh = PEER_HEADS
    full_spec = pl.BlockSpec((1, tq // LANES, PEER_NKEYS, LANES), lambda i, h: (h, i, 0, 0))
    half_spec = pl.BlockSpec((1, tq // LANES, PEER_NKEYS // 2, LANES), lambda i, h: (h, i, 0, 0))
    full = jax.ShapeDtypeStruct((nh, n // LANES, PEER_NKEYS, LANES), u32)
    half = jax.ShapeDtypeStruct((nh, n // LANES, PEER_NKEYS // 2, LANES), u32)
    return pl.pallas_call(
        _peer_select_kernel,
        grid=(n // tq, nh),
        in_specs=[pl.BlockSpec((tq, d), lambda i, h: (i, 0)),
                  pl.BlockSpec((d, 2 * nh * PEER_DHALF), lambda i, h: (0, 0)),
                  pl.BlockSpec((1, 2, PEER_NKEYS, PEER_DHALF), lambda i, h: (h, 0, 0, 0))],
        out_specs=[full_spec, half_spec, full_spec, half_spec],
        out_shape=[full, half, full, half],
        scratch_shapes=[pltpu.VMEM((tq, 2 * nh * PEER_DHALF), f32)],
        compiler_params=pltpu.CompilerParams(
            dimension_semantics=("arbitrary", "arbitrary"), vmem_limit_bytes=VMEM_LIMIT),
        name="peer_select",
    )(x, wq_b, keys_b)


def _peer_main_kernel(xb_ref, x_ref, cnt_ref, rk1_ref, p0_ref, p1_ref, u_ref, vt_ref,
                      g_ref, b_ref, o_ref, acc_ref, *slab_refs, te, tt):
    j = pl.program_id(1)
    n_i0 = te // PEER_NKEYS
    n_slab = tt // LANES
    per_chunk = MXU_COLS // LANES
    at_refs, ga_refs = slab_refs[:n_slab], slab_refs[n_slab:]
    half = PEER_NKEYS // 2
    unpack = lambda words: pltpu.bitcast(words, bf16)

    @pl.when(j == 0)
    def _():
        acc_ref[...] = jnp.zeros_like(acc_ref)

    def act_chunk(c):
        at = lax.dot_general(u_ref[...], xb_ref[c * MXU_COLS:(c + 1) * MXU_COLS, :],
                             (((1,), (1,)), ((), ())), preferred_element_type=f32)
        for k in range(per_chunk):
            at_refs[c * per_chunk + k][...] = at[:, k * LANES:(k + 1) * LANES]

    def gate_slab(s):
        for il in range(n_i0):
            g = jnp.zeros((PEER_NKEYS, LANES), bf16)
            for h in range(PEER_HEADS):
                cntrow = jnp.broadcast_to(cnt_ref[h, s, il:il + 1, :], (half, LANES))
                p0row = jnp.broadcast_to(p0_ref[h, s, il:il + 1, :], (half, LANES))
                sel = unpack(rk1_ref[h, s]) < unpack(cntrow)
                g = g + jnp.where(sel, unpack(p1_ref[h, s]) * unpack(p0row), jnp.zeros((), bf16))
            ga = _gelu_tanh(at_refs[s][il * PEER_NKEYS:(il + 1) * PEER_NKEYS, :].astype(bf16)) * g
            ga_refs[s][il * half:(il + 1) * half, :] = pltpu.bitcast(ga, u32)

    def out_chunk(c):
        ga = jnp.concatenate([unpack(ga_refs[c * per_chunk + k][...]) for k in range(per_chunk)], axis=1)
        lanes = slice(c * MXU_COLS, (c + 1) * MXU_COLS)
        acc_ref[:, lanes] += jnp.dot(vt_ref[...], ga, preferred_element_type=f32)

    n_chunk = tt // MXU_COLS
    act_chunk(0)
    for c in range(n_chunk):
        gate_slab(c * per_chunk)
        if c + 1 < n_chunk:
            act_chunk(c + 1)
        for k in range(1, per_chunk):
            gate_slab(c * per_chunk + k)
        out_chunk(c)

    @pl.when(j == pl.num_programs(1) - 1)
    def _():
        z = ALPHA * x_ref[...] + acc_ref[...].T
        o_ref[...] = _layer_norm_rows(z, g_ref[...], b_ref[...])


def peer_main(x, xb, stats, u_b, vt_b, ln_g, ln_b, *, tt, te):
    n, d = x.shape
    e = u_b.shape[0]
    nh = PEER_HEADS
    cnt, rk1, p0, p1 = stats
    full_spec = pl.BlockSpec((nh, tt // LANES, te // PEER_NKEYS, LANES), lambda i, j: (0, i, j, 0))
    half_spec = pl.BlockSpec((nh, tt // LANES, PEER_NKEYS // 2, LANES), lambda i, j: (0, i, 0, 0))
    return pl.pallas_call(
        functools.partial(_peer_main_kernel, te=te, tt=tt),
        grid=(n // tt, e // te),
        in_specs=[pl.BlockSpec((tt, d), lambda i, j: (i, 0)),
                  pl.BlockSpec((tt, d), lambda i, j: (i, 0)),
                  full_spec, half_spec, full_spec, half_spec,
                  pl.BlockSpec((te, d), lambda i, j: (j, 0)),
                  pl.BlockSpec((d, te), lambda i, j: (0, j)),
                  pl.BlockSpec((1, d), lambda i, j: (0, 0)),
                  pl.BlockSpec((1, d), lambda i, j: (0, 0))],
        out_specs=pl.BlockSpec((tt, d), lambda i, j: (i, 0)),
        out_shape=jax.ShapeDtypeStruct((n, d), f32),
        scratch_shapes=([pltpu.VMEM((d, tt), f32)]
                        + [pltpu.VMEM((te, LANES), f32)] * (tt // LANES)
                        + [pltpu.VMEM((te // 2, LANES), u32)] * (tt // LANES)),
        compiler_params=pltpu.CompilerParams(
            dimension_semantics=("arbitrary", "arbitrary"), vmem_limit_bytes=VMEM_LIMIT),
        name="peer_main",
    )(xb, x, cnt, rk1, p0, p1, u_b, vt_b, ln_g.reshape(1, d), ln_b.reshape(1, d))


def peer_layer(x, w_q, sub_keys, u, v, ln_g, ln_b, *, tq=256, tt=1024, te=1024):
    stats = peer_select(x, w_q.astype(bf16), sub_keys.astype(bf16), tq=tq)
    return peer_main(x, x.astype(bf16), stats, u.astype(bf16), v.T.astype(bf16), ln_g, ln_b, tt=tt, te=te)


def _rglru_kernel(x_ref, win_ref, cw_ref, vec_ref, wg_ref, wout_ref, lng_ref, lnb_ref, o_ref,
                  tail_ref, h_ref, *, tiles_per_seq):
    i = pl.program_id(0)
    w = RG_WIDTH
    x = x_ref[...]
    ts = x.shape[0]

    @pl.when(i % tiles_per_seq == 0)
    def _():
        tail_ref[...] = jnp.zeros_like(tail_ref)
        h_ref[...] = jnp.zeros_like(h_ref)

    conv_b, b_a, b_x, lam = (vec_ref[n:n + 1, :] for n in range(4))
    hin = jnp.dot(x.astype(bf16), win_ref[...], preferred_element_type=f32)
    gate_branch = _gelu_tanh(hin[:, :w])
    hx = hin[:, w:]

    row = lax.broadcasted_iota(jnp.int32, (ts, w), 0)
    row8 = lax.broadcasted_iota(jnp.int32, (SUBLANES, w), 0)
    tail = tail_ref[...]
    xc = hx * cw_ref[RG_CONV - 1:RG_CONV, :] + conv_b
    for s in range(1, RG_CONV):
        rolled = pltpu.roll(hx, s, axis=0)
        head = jnp.where(row8 < s, pltpu.roll(tail, s, axis=0), rolled[:SUBLANES])
        shifted = jnp.concatenate([head, rolled[SUBLANES:]], axis=0)
        xc = xc + shifted * cw_ref[RG_CONV - 1 - s:RG_CONV - s, :]
    tail_ref[...] = hx[ts - SUBLANES:]

    gates = jnp.dot(xc.astype(bf16), wg_ref[...], preferred_element_type=f32)
    r = _sigmoid(gates[:, :w] + b_a)
    ig = _sigmoid(gates[:, w:] + b_x)
    log_a = (-RG_C * _softplus(-lam)) * r
    a = jnp.exp(log_a)
    u = jnp.sqrt(-jnp.tanh(log_a) * (a * a + 1.0)) * (ig * xc)

    s = 1
    while s < ts:
        keep = row >= s
        a_sh = jnp.where(keep, pltpu.roll(a, s, axis=0), 1.0)
        u_sh = jnp.where(keep, pltpu.roll(u, s, axis=0), 0.0)
        u = a * u_sh + u
        a = a * a_sh
        s *= 2
    hs = u + a * h_ref[SUBLANES - 1:SUBLANES, :]
    h_ref[...] = hs[ts - SUBLANES:]

    m = jnp.dot((hs * gate_branch).astype(bf16), wout_ref[...], preferred_element_type=f32)
    o_ref[...] = _layer_norm_rows(ALPHA * x + m, lng_ref[...], lnb_ref[...])


def rglru_layer(x, seq, w_in, conv_w, conv_b, w_a, b_a, w_x, b_x, lam, w_out, ln_g, ln_b, *, ts=256):
    n, d = x.shape
    w = RG_WIDTH
    blockdiag = lambda t: jax.scipy.linalg.block_diag(*t)
    w_gates = jnp.concatenate([blockdiag(w_a), blockdiag(w_x)], axis=1).astype(bf16)
    vecs = jnp.stack([conv_b, b_a.reshape(w), b_x.reshape(w), lam.reshape(w)])
    consts = (w_in.astype(bf16), conv_w, vecs, w_gates, w_out.astype(bf16), ln_g.reshape(1, d), ln_b.reshape(1, d))
    full = lambda arr: pl.BlockSpec(arr.shape, lambda i: (0,) * arr.ndim, pipeline_mode=pl.Buffered(1))
    tok = pl.BlockSpec((ts, d), lambda i: (i, 0))
    return pl.pallas_call(
        functools.partial(_rglru_kernel, tiles_per_seq=seq // ts),
        grid=(n // ts,),
        in_specs=[tok] + [full(c) for c in consts],
        out_specs=tok,
        out_shape=jax.ShapeDtypeStruct((n, d), f32),
        scratch_shapes=[pltpu.VMEM((SUBLANES, w), f32), pltpu.VMEM((SUBLANES, w), f32)],
        compiler_params=pltpu.CompilerParams(
            dimension_semantics=("arbitrary",), vmem_limit_bytes=VMEM_LIMIT),
        name="rglru",
    )(x, *consts)


def _sigmoid(x):
    return 1.0 / (1.0 + jnp.exp(-x))


def _softplus(x):
    return jnp.maximum(x, 0.0) + jnp.log1p(jnp.exp(-jnp.abs(x)))


def _head_sums(t, ones_ref):
    hi = t.astype(bf16)
    lo = (t - hi.astype(f32)).astype(bf16)
    return (jnp.dot(hi, ones_ref[...], preferred_element_type=f32)
            + jnp.dot(lo, ones_ref[...], preferred_element_type=f32))


def _rwkv_proj_kernel(x_ref, xp_ref, mix_ref, wr_ref, wk_ref, wv_ref, w1_ref, w2_ref, a1_ref, a2_ref,
                      g1_ref, g2_ref, vec_ref,
                      r_ref, w_ref, k_ref, v_ref, gate_ref, g_ref, *, tiles_per_seq):
    i = pl.program_id(0)
    x = x_ref[...]
    ts = x.shape[0]
    prev_last = jnp.where(i % tiles_per_seq == 0, 0.0, xp_ref[SUBLANES - 1:SUBLANES, :])
    row = lax.broadcasted_iota(jnp.int32, x.shape, 0)
    xprev = jnp.where(row == 0, prev_last, pltpu.roll(x, 1, axis=0))
    xx = xprev - x

    def mixed(m):
        return (x + xx * mix_ref[m:m + 1, :]).astype(bf16)

    def mm(a, w_ref_):
        return jnp.dot(a, w_ref_[...], preferred_element_type=f32)

    w0, a0 = (vec_ref[n:n + 1, :] for n in range(2))
    lw = mm(jnp.tanh(mm(mixed(1), w1_ref)).astype(bf16), w2_ref)
    w_log = -_softplus(-(w0 + lw)) - 0.5
    r_ref[...] = mm(mixed(0), wr_ref).astype(bf16)
    w_ref[...] = jnp.exp(w_log).astype(bf16)
    k_ref[...] = mm(mixed(2), wk_ref).astype(bf16)
    v_ref[...] = mm(mixed(3), wv_ref).astype(bf16)
    gate_ref[...] = _sigmoid(a0 + mm(mm(mixed(4), a1_ref).astype(bf16), a2_ref)).astype(bf16)
    g_ref[...] = mm(_sigmoid(mm(mixed(5), g1_ref)).astype(bf16), g2_ref)


def rwkv_proj(x, seq, mix, w_r, w_k, w_v, w1, w2, a1, a2, g1, g2, vecs, *, ts):
    n, d = x.shape
    full = lambda arr: pl.BlockSpec(arr.shape, lambda i: (0,) * arr.ndim)
    tok = pl.BlockSpec((ts, d), lambda i: (i, 0))
    prev = pl.BlockSpec((SUBLANES, d), lambda i: (jnp.maximum(i * (ts // SUBLANES) - 1, 0), 0))
    weights = (mix, w_r, w_k, w_v, w1, w2, a1, a2, g1, g2, vecs)
    out = lambda dt: jax.ShapeDtypeStruct((n, d), dt)
    return pl.pallas_call(
        functools.partial(_rwkv_proj_kernel, tiles_per_seq=seq // ts),
        grid=(n // ts,),
        in_specs=[tok, prev] + [full(w) for w in weights],
        out_specs=[tok] * 6,
        out_shape=[out(bf16), out(bf16), out(bf16), out(bf16), out(bf16), out(f32)],
        compiler_params=pltpu.CompilerParams(
            dimension_semantics=("arbitrary",), vmem_limit_bytes=VMEM_LIMIT),
        name="rwkv_proj",
    )(x, x, *weights)


def _rwkv_scan_kernel(rin_ref, win_ref, k_ref, vin_ref, gate_ref, knext_ref, kk_ref, ka_ref, o_ref,
                      s_ref, sa_ref, a_ref, b_ref, km_ref, r_ref, v_ref, w_ref):
    nk = s_ref.shape[0]
    tc = rin_ref.shape[0]
    r_ref[...] = rin_ref[...].astype(f32)
    v_ref[...] = vin_ref[...].astype(f32)
    w_ref[...] = jnp.exp(-win_ref[...].astype(f32))

    @pl.when(pl.program_id(0) == 0)
    def _():
        s_ref[...] = jnp.zeros_like(s_ref)
        sa_ref[...] = jnp.zeros_like(sa_ref)

    def unit_key(k):
        kk = k * kk_ref[...]
        norm = jnp.sqrt(jnp.sum(kk * kk, axis=-2, keepdims=True))
        return kk / jnp.maximum(norm, RWKV_L2_EPS)

    k_blk = k_ref[...].astype(f32)
    gate = gate_ref[...].astype(f32)
    kk_blk = unit_key(k_blk)
    a_ref[0:tc] = -kk_blk
    a_ref[tc:tc + 1] = -unit_key(knext_ref[...].astype(f32))
    b_ref[...] = kk_blk * gate
    km_ref[...] = k_blk * (1.0 + (gate - 1.0) * ka_ref[...])

    def step(t, sa):
        vt = v_ref[t]
        o = [jnp.zeros(s_ref.shape[1:], f32)] * 2
        nsa = [jnp.zeros(s_ref.shape[1:], f32)] * 2
        for kk in range(nk):
            new = (s_ref[kk] * w_ref[t, kk:kk + 1, :]
                   + (sa * b_ref[t, kk:kk + 1, :] + vt * km_ref[t, kk:kk + 1, :]))
            s_ref[kk] = new
            o[kk % 2] = o[kk % 2] + new * r_ref[t, kk:kk + 1, :]
            nsa[kk % 2] = nsa[kk % 2] + new * a_ref[t + 1, kk:kk + 1, :]
        ot = o[0] + o[1]
        mu = jnp.mean(ot, axis=0, keepdims=True)
        oc = ot - mu
        var = jnp.mean(oc * oc, axis=0, keepdims=True)
        o_ref[t] = (oc * lax.rsqrt(var + RWKV_GN_EPS)).astype(o_ref.dtype)
        return nsa[0] + nsa[1]

    sa_ref[...] = lax.fori_loop(0, tc, step, sa_ref[...])


def rwkv_scan(r, w, k, v, gate, kk_vec, ka_vec, *, tc):
    s, hd, chains = r.shape
    blk = pl.BlockSpec((tc, hd, chains), lambda i: (i, 0, 0))
    nxt = pl.BlockSpec((1, hd, chains), lambda i: (jnp.minimum((i + 1) * tc, s - 1), 0, 0))
    vec = pl.BlockSpec((hd, chains), lambda i: (0, 0))
    return pl.pallas_call(
        _rwkv_scan_kernel,
        grid=(s // tc,),
        in_specs=[blk] * 5 + [nxt, vec, vec],
        out_specs=blk,
        out_shape=jax.ShapeDtypeStruct((s, hd, chains), bf16),
        scratch_shapes=[pltpu.VMEM((hd, hd, chains), f32), pltpu.VMEM((hd, chains), f32),
                        pltpu.VMEM((tc + 1, hd, chains), f32)] + [pltpu.VMEM((tc, hd, chains), f32)] * 5,
        compiler_params=pltpu.CompilerParams(
            dimension_semantics=("arbitrary",), vmem_limit_bytes=VMEM_LIMIT),
        name="rwkv_scan",
    )(r, w, k, v, gate, k, kk_vec, ka_vec)


def _rwkv_out_kernel(x_ref, o_ref, r_ref, k_ref, gate_ref, v_ref, g_ref, vec_ref, ones_ref, wo_ref,
                     lng_ref, lnb_ref, out_ref):
    lnx_g, lnx_b, r_k, k_a = (vec_ref[n:n + 1, :] for n in range(4))
    widen = lambda ref: ref[...].astype(f32)
    k_mod = widen(k_ref) * (1.0 + (widen(gate_ref) - 1.0) * k_a)
    bonus = _head_sums(widen(r_ref) * k_mod * r_k, ones_ref) * widen(v_ref)
    y = (widen(o_ref) * lnx_g + lnx_b + bonus) * g_ref[...]
    m = jnp.dot(y.astype(bf16), wo_ref[...], preferred_element_type=f32)
    out_ref[...] = _layer_norm_rows(ALPHA * x_ref[...] + m, lng_ref[...], lnb_ref[...])


def rwkv_out(x, o, r, k, gate, v, g, vecs, ones, w_o, ln_g, ln_b, *, ts):
    n, d = x.shape
    full = lambda arr: pl.BlockSpec(arr.shape, lambda i: (0,) * arr.ndim)
    tok = pl.BlockSpec((ts, d), lambda i: (i, 0))
    consts = (vecs, ones, w_o, ln_g.reshape(1, d), ln_b.reshape(1, d))
    return pl.pallas_call(
        _rwkv_out_kernel,
        grid=(n // ts,),
        in_specs=[tok] * 7 + [full(c) for c in consts],
        out_specs=tok,
        out_shape=jax.ShapeDtypeStruct((n, d), f32),
        compiler_params=pltpu.CompilerParams(
            dimension_semantics=("arbitrary",), vmem_limit_bytes=VMEM_LIMIT),
        name="rwkv_out",
    )(x, o, r, k, gate, v, g, *consts)


def _head_ones(d, head):
    seg = jnp.arange(d) // head
    return (seg[:, None] == seg[None, :]).astype(bf16)


def rwkv_layer(x, seq, mix, w_r, w_k, w_v, w0, w1, w2, a0, a1, a2, g1, g2, k_k, k_a, r_k, lnx_g, lnx_b, w_o,
               ln_g, ln_b, *, ts=256, tc=16):
    n, d = x.shape
    bsz = n // seq
    nh, hd = RWKV_HEADS, RWKV_HEAD
    ones = _head_ones(d, hd)
    c = lambda w: w.astype(bf16)
    r, w, k, v, gate, g = rwkv_proj(x, seq, mix, c(w_r), c(w_k), c(w_v), c(w1), c(w2), c(a1), c(a2),
                                    c(g1), c(g2), jnp.stack([w0, a0]), ts=ts)
    to_chains = lambda t: t.reshape(bsz, seq, nh, hd).transpose(1, 3, 0, 2).reshape(seq, hd, bsz * nh)
    vec_chains = lambda t: jnp.tile(t.reshape(nh, hd).T, (1, bsz))
    o = rwkv_scan(*(to_chains(t) for t in (r, w, k, v, gate)), vec_chains(k_k), vec_chains(k_a), tc=tc)
    o = o.reshape(seq, hd, bsz, nh).transpose(2, 0, 3, 1).reshape(n, d)
    vecs_out = jnp.stack([lnx_g, lnx_b, r_k.reshape(d), k_a])
    return rwkv_out(x, o, r, k, gate, v, g, vecs_out, ones, c(w_o), ln_g, ln_b, ts=ts)


def kernel(x, rg_w_in, rg_conv_w, rg_conv_b, rg_w_a, rg_b_a, rg_w_x, rg_b_x, rg_lambda, rg_w_out, rw_mix, rw_w_r, rw_w_k, rw_w_v, rw_w0, rw_w1, rw_w2, rw_a0, rw_a1, rw_a2, rw_g1, rw_g2, rw_k_k, rw_k_a, rw_r_k, rw_lnx_g, rw_lnx_b, rw_w_o, peer_w_q, peer_sub_keys, peer_u, peer_v, ln_g, ln_b):
    bsz, s, d = x.shape
    x = x.reshape(bsz * s, d)
    for i in range(DEPTH):
        j = i // 2
        if i % 2 == 0:
            x1 = rglru_layer(x, s, rg_w_in[j], rg_conv_w[j], rg_conv_b[j], rg_w_a[j], rg_b_a[j],
                             rg_w_x[j], rg_b_x[j], rg_lambda[j], rg_w_out[j], ln_g[i, 0], ln_b[i, 0])
        else:
            x1 = rwkv_layer(x, s, rw_mix[j], rw_w_r[j], rw_w_k[j], rw_w_v[j], rw_w0[j], rw_w1[j],
                            rw_w2[j], rw_a0[j], rw_a1[j], rw_a2[j], rw_g1[j], rw_g2[j],
                            rw_k_k[j], rw_k_a[j], rw_r_k[j], rw_lnx_g[j], rw_lnx_b[j], rw_w_o[j],
                            ln_g[i, 0], ln_b[i, 0])
        x = peer_layer(x1, peer_w_q[i], peer_sub_keys[i], peer_u[i], peer_v[i], ln_g[i, 1], ln_b[i, 1])
    return x.reshape(bsz, s, d)
```

```python
import functools
import math
import jax, jax.numpy as jnp
from jax import lax
from jax.experimental import pallas as pl
from jax.experimental.pallas import tpu as pltpu

D_MODEL = 1024
DEPTH = 2
RG_WIDTH = 1408
RG_HEADS = 16
RG_BLOCK = 88
RG_CONV = 4
RG_C = 8.0
RWKV_HEAD = 64
RWKV_HEADS = 16
RWKV_GN_EPS = 64e-5
RWKV_L2_EPS = 1e-12
PEER_HEADS = 8
PEER_NKEYS = 128
PEER_DHALF = 128
PEER_TOPK = 16
PEER_BLOCK = 128
ALPHA = (2 * DEPTH) ** 0.25
LN_EPS = 1e-5

LANES = 128
SUBLANES = 8
MXU_COLS = 256
VMEM_LIMIT = 56 << 20

f32 = jnp.float32
bf16 = jnp.bfloat16
u32 = jnp.uint32


RG_BAND = 3 * LANES


def _rg_band_starts():
    n_tiles = RG_WIDTH // LANES
    starts = [LANES * min(max(j - 1, 0), n_tiles - RG_BAND // LANES) for j in range(n_tiles)]
    for j, lo in enumerate(starts):
        first_head, last_head = (j * LANES) // RG_BLOCK, ((j + 1) * LANES - 1) // RG_BLOCK
        assert lo <= first_head * RG_BLOCK and (last_head + 1) * RG_BLOCK <= lo + RG_BAND
    return starts


def _gelu_tanh(x):
    return 0.5 * x * (1.0 + jnp.tanh(math.sqrt(2.0 / math.pi) * (x + 0.044715 * (x * x * x))))


def _layer_norm_rows(z, g, b):
    mu = jnp.mean(z, axis=-1, keepdims=True)
    zc = z - mu
    var = jnp.mean(zc * zc, axis=-1, keepdims=True)
    return zc * lax.rsqrt(var + LN_EPS) * g + b


def _top_rows(s, k, with_rank=False):
    rows = []
    rank = jnp.full(s.shape, float(k), f32)
    for j in range(k):
        m = jnp.max(s, axis=0, keepdims=True)
        rows.append(m)
        hit = s == m
        if with_rank:
            rank = jnp.where(hit, float(j), rank)
        s = jnp.where(hit, -jnp.inf, s)
    return (rows, rank) if with_rank else rows


def _peer_select_kernel(x_ref, wq_ref, keys_ref, cnt_ref, rk1_ref, p0_ref, p1_ref, q_ref):
    h = pl.program_id(1)

    @pl.when(h == 0)
    def _():
        q_ref[...] = jnp.dot(x_ref[...].astype(bf16), wq_ref[...], preferred_element_type=f32)

    def scores(p):
        off = pl.multiple_of((2 * h + p) * PEER_DHALF, PEER_DHALF)
        qhp = q_ref[:, pl.ds(off, PEER_DHALF)].astype(bf16)
        return lax.dot_general(keys_ref[0, p], qhp, (((1,), (1,)), ((), ())),
                               preferred_element_type=f32)

    def select_all(stable):
        s0_all = scores(0)
        s1_all = scores(1)
        tied = 0.0
        for c in range(s0_all.shape[1] // LANES):
            lanes = slice(c * LANES, (c + 1) * LANES)
            cnt, rank1, p0, p1, tied_c = _select_slab(s0_all[:, lanes], s1_all[:, lanes], stable=stable)
            cnt_ref[0, c] = _bf16_twice(cnt)
            p0_ref[0, c] = _bf16_twice(p0)
            rk1_ref[0, c] = pltpu.bitcast(rank1.astype(bf16), u32)
            p1_ref[0, c] = pltpu.bitcast(p1.astype(bf16), u32)
            if not stable:
                tied = jnp.maximum(tied, jnp.max(tied_c))
        return tied

    tied = select_all(stable=False)

    @pl.when(tied > 0.0)
    def _():
        select_all(stable=True)


def _top_rows_stable(s, k):
    rows = []
    idx = lax.broadcasted_iota(jnp.int32, s.shape, 0)
    rank = jnp.full(s.shape, float(k), f32)
    for j in range(k):
        m = jnp.max(s, axis=0, keepdims=True)
        rows.append(m)
        first = jnp.min(jnp.where(s == m, idx, s.shape[0]), axis=0, keepdims=True)
        pick = idx == first
        rank = jnp.where(pick, float(j), rank)
        s = jnp.where(pick, -jnp.inf, s)
    return rows, rank


def _select_slab(s0, s1, *, stable):
    if stable:
        a, rank0 = _top_rows_stable(s0, PEER_TOPK)
        b, rank1 = _top_rows_stable(s1, PEER_TOPK)
    else:
        a = _top_rows(s0, PEER_TOPK)
        b, rank1 = _top_rows(s1, PEER_TOPK, with_rank=True)
    row = lax.broadcasted_iota(jnp.int32, (PEER_TOPK, s0.shape[1]), 0)
    bmat = jnp.zeros((PEER_TOPK, s0.shape[1]), f32)
    for j in range(PEER_TOPK):
        bmat = jnp.where(row == j, b[j], bmat)
    row8 = lax.broadcasted_iota(jnp.int32, (SUBLANES, s0.shape[1]), 0)
    ahi = jnp.zeros((SUBLANES, s0.shape[1]), f32)
    for n in range(SUBLANES):
        ahi = jnp.where(row8 == n, a[SUBLANES + n], ahi)
    cand = jnp.concatenate([a[0] + bmat] + [a[i] + bmat[:SUBLANES] for i in range(1, SUBLANES)]
                           + [ahi + b[0]], axis=0)
    rows_of = [PEER_TOPK] + [SUBLANES] * (SUBLANES - 1) + [1] * SUBLANES
    if stable:
        picked = _top_rows_stable(cand, PEER_TOPK)[1] < float(PEER_TOPK)
    else:
        picked = cand >= _top_rows(cand, PEER_TOPK)[-1]
    picked = jnp.where(picked, 1.0, 0.0)
    cmax = a[0] + b[0]
    z = jnp.sum(picked * jnp.exp(cand - cmax), axis=0, keepdims=True)
    cnt = jnp.zeros_like(s0)
    total = jnp.zeros_like(z)
    lo = 0
    for i in range(PEER_TOPK):
        hi = lo + rows_of[i]
        cnt_i = jnp.sum(picked[lo:hi], axis=0, keepdims=True)
        lo = hi
        total = total + cnt_i
        cnt = jnp.where((rank0 == float(i)) if stable else (s0 == a[i]), cnt_i, cnt)
    p0 = jnp.exp(s0 - a[0]) / z
    p1 = jnp.exp(s1 - b[0])
    if stable:
        return cnt, rank1, p0, p1, None
    k = float(PEER_TOPK)
    n0 = jnp.sum(jnp.where(s0 >= a[-1], 1.0, 0.0), axis=0, keepdims=True)
    n1 = jnp.sum(jnp.where(rank1 < k, 1.0, 0.0), axis=0, keepdims=True)
    tied = jnp.abs(n0 - k) + jnp.abs(n1 - k) + jnp.abs(total - k)
    return cnt, rank1, p0, p1, tied


def _bf16_twice(t):
    bits = pltpu.bitcast(t.astype(bf16).astype(f32), u32) >> 16
    return bits | (bits << 16)


def peer_select(x, wq_b, keys_b, *, tq):
    n, d = x.shape
    nh = PEER_HEADS
    full_spec = pl.BlockSpec((1, tq // LANES, PEER_NKEYS, LANES), lambda i, h: (h, i, 0, 0))
    half_spec = pl.BlockSpec((1, tq // LANES, PEER_NKEYS // 2, LANES), lambda i, h: (h, i, 0, 0))
    full = jax.ShapeDtypeStruct((nh, n // LANES, PEER_NKEYS, LANES), u32)
    half = jax.ShapeDtypeStruct((nh, n // LANES, PEER_NKEYS // 2, LANES), u32)
    return pl.pallas_call(
        _peer_select_kernel,
        grid=(n // tq, nh),
        in_specs=[pl.BlockSpec((tq, d), lambda i, h: (i, 0)),
                  pl.BlockSpec((d, 2 * nh * PEER_DHALF), lambda i, h: (0, 0)),
                  pl.BlockSpec((1, 2, PEER_NKEYS, PEER_DHALF), lambda i, h: (h, 0, 0, 0))],
        out_specs=[full_spec, half_spec, full_spec, half_spec],
        out_shape=[full, half, full, half],
        scratch_shapes=[pltpu.VMEM((tq, 2 * nh * PEER_DHALF), f32)],
        compiler_params=pltpu.CompilerParams(
            dimension_semantics=("arbitrary", "arbitrary"), vmem_limit_bytes=VMEM_LIMIT),
        name="peer_select",
    )(x, wq_b, keys_b)


def _peer_main_kernel(xb_ref, x_ref, cnt_ref, rk1_ref, p0_ref, p1_ref, u_ref, vt_ref,
                      g_ref, b_ref, o_ref, acc_ref, *slab_refs, te, tt):
    j = pl.program_id(1)
    n_i0 = te // PEER_NKEYS
    n_slab = tt // LANES
    per_chunk = MXU_COLS // LANES
    at_refs, ga_refs = slab_refs[:n_slab], slab_refs[n_slab:]
    half = PEER_NKEYS // 2
    unpack = lambda words: pltpu.bitcast(words, bf16)

    @pl.when(j == 0)
    def _():
        acc_ref[...] = jnp.zeros_like(acc_ref)

    def act_chunk(c):
        at = lax.dot_general(u_ref[...], xb_ref[c * MXU_COLS:(c + 1) * MXU_COLS, :],
                             (((1,), (1,)), ((), ())), preferred_element_type=f32)
        for k in range(per_chunk):
            at_refs[c * per_chunk + k][...] = at[:, k * LANES:(k + 1) * LANES]

    def gate_slab(s):
        for il in range(n_i0):
            g = jnp.zeros((PEER_NKEYS, LANES), bf16)
            for h in range(PEER_HEADS):
                cntrow = jnp.broadcast_to(cnt_ref[h, s, il:il + 1, :], (half, LANES))
                p0row = jnp.broadcast_to(p0_ref[h, s, il:il + 1, :], (half, LANES))
                sel = unpack(rk1_ref[h, s]) < unpack(cntrow)
                g = g + jnp.where(sel, unpack(p1_ref[h, s]) * unpack(p0row), jnp.zeros((), bf16))
            ga = _gelu_tanh(at_refs[s][il * PEER_NKEYS:(il + 1) * PEER_NKEYS, :].astype(bf16)) * g
            ga_refs[s][il * half:(il + 1) * half, :] = pltpu.bitcast(ga, u32)

    def out_chunk(c):
        ga = jnp.concatenate([unpack(ga_refs[c * per_chunk + k][...]) for k in range(per_chunk)], axis=1)
        lanes = slice(c * MXU_COLS, (c + 1) * MXU_COLS)
        acc_ref[:, lanes] += jnp.dot(vt_ref[...], ga, preferred_element_type=f32)

    n_chunk = tt // MXU_COLS
    act_chunk(0)
    for c in range(n_chunk):
        gate_slab(c * per_chunk)
        if c + 1 < n_chunk:
            act_chunk(c + 1)
        for k in range(1, per_chunk):
            gate_slab(c * per_chunk + k)
        out_chunk(c)

    @pl.when(j == pl.num_programs(1) - 1)
    def _():
        z = ALPHA * x_ref[...] + acc_ref[...].T
        o_ref[...] = _layer_norm_rows(z, g_ref[...], b_ref[...])


def peer_main(x, xb, stats, u_b, vt_b, ln_g, ln_b, *, tt, te):
    n, d = x.shape
    e = u_b.shape[0]
    nh = PEER_HEADS
    cnt, rk1, p0, p1 = stats
    full_spec = pl.BlockSpec((nh, tt // LANES, te // PEER_NKEYS, LANES), lambda i, j: (0, i, j, 0))
    half_spec = pl.BlockSpec((nh, tt // LANES, PEER_NKEYS // 2, LANES), lambda i, j: (0, i, 0, 0))
    return pl.pallas_call(
        functools.partial(_peer_main_kernel, te=te, tt=tt),
        grid=(n // tt, e // te),
        in_specs=[pl.BlockSpec((tt, d), lambda i, j: (i, 0)),
                  pl.BlockSpec((tt, d), lambda i, j: (i, 0)),
                  full_spec, half_spec, full_spec, half_spec,
                  pl.BlockSpec((te, d), lambda i, j: (j, 0)),
                  pl.BlockSpec((d, te), lambda i, j: (0, j)),
                  pl.BlockSpec((1, d), lambda i, j: (0, 0)),
                  pl.BlockSpec((1, d), lambda i, j: (0, 0))],
        out_specs=pl.BlockSpec((tt, d), lambda i, j: (i, 0)),
        out_shape=jax.ShapeDtypeStruct((n, d), f32),
        scratch_shapes=([pltpu.VMEM((d, tt), f32)]
                        + [pltpu.VMEM((te, LANES), f32)] * (tt // LANES)
                        + [pltpu.VMEM((te // 2, LANES), u32)] * (tt // LANES)),
        compiler_params=pltpu.CompilerParams(
            dimension_semantics=("arbitrary", "arbitrary"), vmem_limit_bytes=VMEM_LIMIT),
        name="peer_main",
    )(xb, x, cnt, rk1, p0, p1, u_b, vt_b, ln_g.reshape(1, d), ln_b.reshape(1, d))


def peer_layer(x, w_q, sub_keys, u, v, ln_g, ln_b, *, tq=256, tt=1024, te=1024):
    stats = peer_select(x, w_q.astype(bf16), sub_keys.astype(bf16), tq=tq)
    return peer_main(x, x.astype(bf16), stats, u.astype(bf16), v.T.astype(bf16), ln_g, ln_b, tt=tt, te=te)


def _rglru_kernel(x_ref, win_ref, cw_ref, vec_ref, wg_ref, wout_ref, lng_ref, lnb_ref, o_ref,
                  tail_ref, h_ref, *, tiles_per_seq):
    i = pl.program_id(0)
    w = RG_WIDTH
    x = x_ref[...]
    ts = x.shape[0]

    @pl.when(i % tiles_per_seq == 0)
    def _():
        tail_ref[...] = jnp.zeros_like(tail_ref)
        h_ref[...] = jnp.zeros_like(h_ref)

    conv_b, b_a, b_x, lam = (vec_ref[n:n + 1, :] for n in range(4))
    hin = jnp.dot(x.astype(bf16), win_ref[...], preferred_element_type=f32)
    gate_branch = _gelu_tanh(hin[:, :w])
    hx = hin[:, w:]

    row = lax.broadcasted_iota(jnp.int32, (ts, w), 0)
    row8 = lax.broadcasted_iota(jnp.int32, (SUBLANES, w), 0)
    tail = tail_ref[...]
    xc = hx * cw_ref[RG_CONV - 1:RG_CONV, :] + conv_b
    for s in range(1, RG_CONV):
        rolled = pltpu.roll(hx, s, axis=0)
        head = jnp.where(row8 < s, pltpu.roll(tail, s, axis=0), rolled[:SUBLANES])
        shifted = jnp.concatenate([head, rolled[SUBLANES:]], axis=0)
        xc = xc + shifted * cw_ref[RG_CONV - 1 - s:RG_CONV - s, :]
    tail_ref[...] = hx[ts - SUBLANES:]

    xcb = xc.astype(bf16)
    bands = [jnp.dot(xcb[:, lo:lo + RG_BAND], wg_ref[j], preferred_element_type=f32)
             for j, lo in enumerate(_rg_band_starts())]
    r = _sigmoid(jnp.concatenate([g[:, :LANES] for g in bands], axis=1) + b_a)
    ig = _sigmoid(jnp.concatenate([g[:, LANES:] for g in bands], axis=1) + b_x)
    log_a = (-RG_C * _softplus(-lam)) * r
    a = jnp.exp(log_a)
    u = jnp.sqrt(-jnp.tanh(log_a) * (a * a + 1.0)) * (ig * xc)

    sub = row & (SUBLANES - 1)
    s = 1
    while s < SUBLANES:
        keep = sub >= s
        a_sh = jnp.where(keep, pltpu.roll(a, s, axis=0), 1.0)
        u_sh = jnp.where(keep, pltpu.roll(u, s, axis=0), 0.0)
        u = a * u_sh + u
        a = a * a_sh
        s *= 2
    carry = h_ref[SUBLANES - 1:SUBLANES, :]
    groups = []
    for g in range(0, ts, SUBLANES):
        hg = u[g:g + SUBLANES] + a[g:g + SUBLANES] * carry
        groups.append(hg)
        carry = hg[SUBLANES - 1:]
    hs = jnp.concatenate(groups, axis=0)
    h_ref[...] = groups[-1]

    m = jnp.dot((hs * gate_branch).astype(bf16), wout_ref[...], preferred_element_type=f32)
    o_ref[...] = _layer_norm_rows(ALPHA * x + m, lng_ref[...], lnb_ref[...])


def rglru_layer(x, seq, w_in, conv_w, conv_b, w_a, b_a, w_x, b_x, lam, w_out, ln_g, ln_b, *, ts=256):
    n, d = x.shape
    w = RG_WIDTH
    blockdiag = lambda t: jax.scipy.linalg.block_diag(*t)
    full_a, full_x = blockdiag(w_a), blockdiag(w_x)
    w_gates = jnp.stack([
        jnp.concatenate([m[lo:lo + RG_BAND, j * LANES:(j + 1) * LANES] for m in (full_a, full_x)], axis=1)
        for j, lo in enumerate(_rg_band_starts())]).astype(bf16)
    vecs = jnp.stack([conv_b, b_a.reshape(w), b_x.reshape(w), lam.reshape(w)])
    consts = (w_in.astype(bf16), conv_w, vecs, w_gates, w_out.astype(bf16), ln_g.reshape(1, d), ln_b.reshape(1, d))
    full = lambda arr: pl.BlockSpec(arr.shape, lambda i: (0,) * arr.ndim, pipeline_mode=pl.Buffered(1))
    tok = pl.BlockSpec((ts, d), lambda i: (i, 0))
    return pl.pallas_call(
        functools.partial(_rglru_kernel, tiles_per_seq=seq // ts),
        grid=(n // ts,),
        in_specs=[tok] + [full(c) for c in consts],
        out_specs=tok,
        out_shape=jax.ShapeDtypeStruct((n, d), f32),
        scratch_shapes=[pltpu.VMEM((SUBLANES, w), f32), pltpu.VMEM((SUBLANES, w), f32)],
        compiler_params=pltpu.CompilerParams(
            dimension_semantics=("arbitrary",), vmem_limit_bytes=VMEM_LIMIT),
        name="rglru",
    )(x, *consts)


def _sigmoid(x):
    return 1.0 / (1.0 + jnp.exp(-x))


def _softplus(x):
    return jnp.maximum(x, 0.0) + jnp.log1p(jnp.exp(-jnp.abs(x)))


def _head_sums(t, ones_ref):
    hi = t.astype(bf16)
    lo = (t - hi.astype(f32)).astype(bf16)
    return (jnp.dot(hi, ones_ref[...], preferred_element_type=f32)
            + jnp.dot(lo, ones_ref[...], preferred_element_type=f32))


def _rwkv_proj_kernel(x_ref, xp_ref, mix_ref, wr_ref, wk_ref, wv_ref, w1_ref, w2_ref, a1_ref, a2_ref,
                      g1_ref, g2_ref, vec_ref,
                      r_ref, w_ref, k_ref, v_ref, gate_ref, g_ref, *, tiles_per_seq):
    i = pl.program_id(0)
    x = x_ref[...]
    ts = x.shape[0]
    prev_last = jnp.where(i % tiles_per_seq == 0, 0.0, xp_ref[SUBLANES - 1:SUBLANES, :])
    row = lax.broadcasted_iota(jnp.int32, x.shape, 0)
    xprev = jnp.where(row == 0, prev_last, pltpu.roll(x, 1, axis=0))
    xx = xprev - x

    def mixed(m):
        return (x + xx * mix_ref[m:m + 1, :]).astype(bf16)

    def mm(a, w_ref_):
        return jnp.dot(a, w_ref_[...], preferred_element_type=f32)

    w0, a0 = (vec_ref[n:n + 1, :] for n in range(2))
    lw = mm(jnp.tanh(mm(mixed(1), w1_ref)).astype(bf16), w2_ref)
    w_log = -_softplus(-(w0 + lw)) - 0.5
    r_ref[...] = mm(mixed(0), wr_ref).astype(bf16)
    w_ref[...] = jnp.exp(w_log).astype(bf16)
    k_ref[...] = mm(mixed(2), wk_ref).astype(bf16)
    v_ref[...] = mm(mixed(3), wv_ref).astype(bf16)
    gate_ref[...] = _sigmoid(a0 + mm(mm(mixed(4), a1_ref).astype(bf16), a2_ref)).astype(bf16)
    g_ref[...] = mm(_sigmoid(mm(mixed(5), g1_ref)).astype(bf16), g2_ref)


def rwkv_proj(x, seq, mix, w_r, w_k, w_v, w1, w2, a1, a2, g1, g2, vecs, *, ts):
    n, d = x.shape
    full = lambda arr: pl.BlockSpec(arr.shape, lambda i: (0,) * arr.ndim)
    tok = pl.BlockSpec((ts, d), lambda i: (i, 0))
    prev = pl.BlockSpec((SUBLANES, d), lambda i: (jnp.maximum(i * (ts // SUBLANES) - 1, 0), 0))
    weights = (mix, w_r, w_k, w_v, w1, w2, a1, a2, g1, g2, vecs)
    out = lambda dt: jax.ShapeDtypeStruct((n, d), dt)
    return pl.pallas_call(
        functools.partial(_rwkv_proj_kernel, tiles_per_seq=seq // ts),
        grid=(n // ts,),
        in_specs=[tok, prev] + [full(w) for w in weights],
        out_specs=[tok] * 6,
        out_shape=[out(bf16), out(bf16), out(bf16), out(bf16), out(bf16), out(f32)],
        compiler_params=pltpu.CompilerParams(
            dimension_semantics=("arbitrary",), vmem_limit_bytes=VMEM_LIMIT),
        name="rwkv_proj",
    )(x, x, *weights)


def _rwkv_scan_kernel(rin_ref, win_ref, k_ref, vin_ref, gate_ref, knext_ref, kk_ref, ka_ref, o_ref,
                      s_ref, sa_ref, a_ref, b_ref, km_ref, r_ref, v_ref, w_ref):
    nk = s_ref.shape[0]
    tc = rin_ref.shape[0]
    r_ref[...] = rin_ref[...].astype(f32)
    v_ref[...] = vin_ref[...].astype(f32)
    w_ref[...] = jnp.exp(-win_ref[...].astype(f32))

    @pl.when(pl.program_id(0) == 0)
    def _():
        s_ref[...] = jnp.zeros_like(s_ref)
        sa_ref[...] = jnp.zeros_like(sa_ref)

    def unit_key(k):
        kk = k * kk_ref[...]
        norm = jnp.sqrt(jnp.sum(kk * kk, axis=-2, keepdims=True))
        return kk / jnp.maximum(norm, RWKV_L2_EPS)

    k_blk = k_ref[...].astype(f32)
    gate = gate_ref[...].astype(f32)
    kk_blk = unit_key(k_blk)
    a_ref[0:tc] = -kk_blk
    a_ref[tc:tc + 1] = -unit_key(knext_ref[...].astype(f32))
    b_ref[...] = kk_blk * gate
    km_ref[...] = k_blk * (1.0 + (gate - 1.0) * ka_ref[...])

    def step(t, sa):
        vt = v_ref[t]
        o = [jnp.zeros(s_ref.shape[1:], f32)] * 2
        nsa = [jnp.zeros(s_ref.shape[1:], f32)] * 2
        for kk in range(nk):
            new = (s_ref[kk] * w_ref[t, kk:kk + 1, :]
                   + (sa * b_ref[t, kk:kk + 1, :] + vt * km_ref[t, kk:kk + 1, :]))
            s_ref[kk] = new
            o[kk % 2] = o[kk % 2] + new * r_ref[t, kk:kk + 1, :]
            nsa[kk % 2] = nsa[kk % 2] + new * a_ref[t + 1, kk:kk + 1, :]
        ot = o[0] + o[1]
        mu = jnp.mean(ot, axis=0, keepdims=True)
        oc = ot - mu
        var = jnp.mean(oc * oc, axis=0, keepdims=True)
        o_ref[t] = (oc * lax.rsqrt(var + RWKV_GN_EPS)).astype(o_ref.dtype)
        return nsa[0] + nsa[1]

    sa_ref[...] = lax.fori_loop(0, tc, step, sa_ref[...])


def rwkv_scan(r, w, k, v, gate, kk_vec, ka_vec, *, tc):
    s, hd, chains = r.shape
    blk = pl.BlockSpec((tc, hd, chains), lambda i: (i, 0, 0))
    nxt = pl.BlockSpec((1, hd, chains), lambda i: (jnp.minimum((i + 1) * tc, s - 1), 0, 0))
    vec = pl.BlockSpec((hd, chains), lambda i: (0, 0))
    return pl.pallas_call(
        _rwkv_scan_kernel,
        grid=(s // tc,),
        in_specs=[blk] * 5 + [nxt, vec, vec],
        out_specs=blk,
        out_shape=jax.ShapeDtypeStruct((s, hd, chains), bf16),
        scratch_shapes=[pltpu.VMEM((hd, hd, chains), f32), pltpu.VMEM((hd, chains), f32),
                        pltpu.VMEM((tc + 1, hd, chains), f32)] + [pltpu.VMEM((tc, hd, chains), f32)] * 5,
        compiler_params=pltpu.CompilerParams(
            dimension_semantics=("arbitrary",), vmem_limit_bytes=VMEM_LIMIT),
        name="rwkv_scan",
    )(r, w, k, v, gate, k, kk_vec, ka_vec)


def _rwkv_out_kernel(x_ref, o_ref, r_ref, k_ref, gate_ref, v_ref, g_ref, vec_ref, ones_ref, wo_ref,
                     lng_ref, lnb_ref, out_ref):
    lnx_g, lnx_b, r_k, k_a = (vec_ref[n:n + 1, :] for n in range(4))
    widen = lambda ref: ref[...].astype(f32)
    k_mod = widen(k_ref) * (1.0 + (widen(gate_ref) - 1.0) * k_a)
    bonus = _head_sums(widen(r_ref) * k_mod * r_k, ones_ref) * widen(v_ref)
    y = (widen(o_ref) * lnx_g + lnx_b + bonus) * g_ref[...]
    m = jnp.dot(y.astype(bf16), wo_ref[...], preferred_element_type=f32)
    out_ref[...] = _layer_norm_rows(ALPHA * x_ref[...] + m, lng_ref[...], lnb_ref[...])


def rwkv_out(x, o, r, k, gate, v, g, vecs, ones, w_o, ln_g, ln_b, *, ts):
    n, d = x.shape
    full = lambda arr: pl.BlockSpec(arr.shape, lambda i: (0,) * arr.ndim)
    tok = pl.BlockSpec((ts, d), lambda i: (i, 0))
    consts = (vecs, ones, w_o, ln_g.reshape(1, d), ln_b.reshape(1, d))
    return pl.pallas_call(
        _rwkv_out_kernel,
        grid=(n // ts,),
        in_specs=[tok] * 7 + [full(c) for c in consts],
        out_specs=tok,
        out_shape=jax.ShapeDtypeStruct((n, d), f32),
        compiler_params=pltpu.CompilerParams(
            dimension_semantics=("arbitrary",), vmem_limit_bytes=VMEM_LIMIT),
        name="rwkv_out",
    )(x, o, r, k, gate, v, g, *consts)


def _head_ones(d, head):
    seg = jnp.arange(d) // head
    return (seg[:, None] == seg[None, :]).astype(bf16)


def rwkv_layer(x, seq, mix, w_r, w_k, w_v, w0, w1, w2, a0, a1, a2, g1, g2, k_k, k_a, r_k, lnx_g, lnx_b, w_o,
               ln_g, ln_b, *, ts=256, tc=16):
    n, d = x.shape
    bsz = n // seq
    nh, hd = RWKV_HEADS, RWKV_HEAD
    ones = _head_ones(d, hd)
    c = lambda w: w.astype(bf16)
    r, w, k, v, gate, g = rwkv_proj(x, seq, mix, c(w_r), c(w_k), c(w_v), c(w1), c(w2), c(a1), c(a2),
                                    c(g1), c(g2), jnp.stack([w0, a0]), ts=ts)
    to_chains = lambda t: t.reshape(bsz, seq, nh, hd).transpose(1, 3, 0, 2).reshape(seq, hd, bsz * nh)
    vec_chains = lambda t: jnp.tile(t.reshape(nh, hd).T, (1, bsz))
    o = rwkv_scan(*(to_chains(t) for t in (r, w, k, v, gate)), vec_chains(k_k), vec_chains(k_a), tc=tc)
    o = o.reshape(seq, hd, bsz, nh).transpose(2, 0, 3, 1).reshape(n, d)
    vecs_out = jnp.stack([lnx_g, lnx_b, r_k.reshape(d), k_a])
    return rwkv_out(x, o, r, k, gate, v, g, vecs_out, ones, c(w_o), ln_g, ln_b, ts=ts)


def kernel(x, rg_w_in, rg_conv_w, rg_conv_b, rg_w_a, rg_b_a, rg_w_x, rg_b_x, rg_lambda, rg_w_out, rw_mix, rw_w_r, rw_w_k, rw_w_v, rw_w0, rw_w1, rw_w2, rw_a0, rw_a1, rw_a2, rw_g1, rw_g2, rw_k_k, rw_k_a, rw_r_k, rw_lnx_g, rw_lnx_b, rw_w_o, peer_w_q, peer_sub_keys, peer_u, peer_v, ln_g, ln_b):
    bsz, s, d = x.shape
    x = x.reshape(bsz * s, d)
    for i in range(DEPTH):
        j = i // 2
        if i % 2 == 0:
            x1 = rglru_layer(x, s, rg_w_in[j], rg_conv_w[j], rg_conv_b[j], rg_w_a[j], rg_b_a[j],
                             rg_w_x[j], rg_b_x[j], rg_lambda[j], rg_w_out[j], ln_g[i, 0], ln_b[i, 0])
        else:
            x1 = rwkv_layer(x, s, rw_mix[j], rw_w_r[j], rw_w_k[j], rw_w_v[j], rw_w0[j], rw_w1[j],
                            rw_w2[j], rw_a0[j], rw_a1[j], rw_a2[j], rw_g1[j], rw_g2[j],
                            rw_k_k[j], rw_k_a[j], rw_r_k[j], rw_lnx_g[j], rw_lnx_b[j], rw_w_o[j],
                            ln_g[i, 0], ln_b[i, 0])
        x = peer_layer(x1, peer_w_q[i], peer_sub_keys[i], peer_u[i], peer_v[i], ln_g[i, 1], ln_b[i, 1])
    return x.reshape(bsz, s, d)
```

```python
import functools
import math
import jax, jax.numpy as jnp
from jax import lax
from jax.experimental import pallas as pl
from jax.experimental.pallas import tpu as pltpu

D_MODEL = 1024
DEPTH = 2
RG_WIDTH = 1408
RG_HEADS = 16
RG_BLOCK = 88
RG_CONV = 4
RG_C = 8.0
RWKV_HEAD = 64
RWKV_HEADS = 16
RWKV_GN_EPS = 64e-5
RWKV_L2_EPS = 1e-12
PEER_HEADS = 8
PEER_NKEYS = 128
PEER_DHALF = 128
PEER_TOPK = 16
PEER_BLOCK = 128
ALPHA = (2 * DEPTH) ** 0.25
LN_EPS = 1e-5

LANES = 128
SUBLANES = 8
MXU_COLS = 256
VMEM_LIMIT = 56 << 20

f32 = jnp.float32
bf16 = jnp.bfloat16
u32 = jnp.uint32


RG_BAND = 3 * LANES


def _rg_band_starts():
    n_tiles = RG_WIDTH // LANES
    starts = [LANES * min(max(j - 1, 0), n_tiles - RG_BAND // LANES) for j in range(n_tiles)]
    for j, lo in enumerate(starts):
        first_head, last_head = (j * LANES) // RG_BLOCK, ((j + 1) * LANES - 1) // RG_BLOCK
        assert lo <= first_head * RG_BLOCK and (last_head + 1) * RG_BLOCK <= lo + RG_BAND
    return starts


def _gelu_tanh(x):
    return 0.5 * x * (1.0 + jnp.tanh(math.sqrt(2.0 / math.pi) * (x + 0.044715 * (x * x * x))))


def _layer_norm_rows(z, g, b):
    mu = jnp.mean(z, axis=-1, keepdims=True)
    zc = z - mu
    var = jnp.mean(zc * zc, axis=-1, keepdims=True)
    return zc * lax.rsqrt(var + LN_EPS) * g + b


def _top_rows(s, k, with_rank=False):
    rows = []
    rank = jnp.full(s.shape, float(k), f32)
    for j in range(k):
        m = jnp.max(s, axis=0, keepdims=True)
        rows.append(m)
        hit = s == m
        if with_rank:
            rank = jnp.where(hit, float(j), rank)
        s = jnp.where(hit, -jnp.inf, s)
    return (rows, rank) if with_rank else rows


def _peer_select_kernel(x_ref, wq_ref, keys_ref, cnt_ref, rk1_ref, p0_ref, p1_ref, q_ref):
    h = pl.program_id(1)

    @pl.when(h == 0)
    def _():
        q_ref[...] = jnp.dot(x_ref[...], wq_ref[...], preferred_element_type=f32)

    def scores(p):
        off = pl.multiple_of((2 * h + p) * PEER_DHALF, PEER_DHALF)
        qhp = q_ref[:, pl.ds(off, PEER_DHALF)].astype(bf16)
        return lax.dot_general(keys_ref[0, p], qhp, (((1,), (1,)), ((), ())),
                               preferred_element_type=f32)

    def select_all(stable):
        s0_all = scores(0)
        s1_all = scores(1)
        tied = jnp.zeros((1, LANES), f32)
        for c in range(s0_all.shape[1] // LANES):
            lanes = slice(c * LANES, (c + 1) * LANES)
            cnt, rank1, p0, p1, tied_c = _select_slab(s0_all[:, lanes], s1_all[:, lanes], stable=stable)
            cnt_ref[0, c] = _bf16_twice(cnt)
            p0_ref[0, c] = _bf16_twice(p0)
            rk1_ref[0, c] = pltpu.bitcast(rank1.astype(bf16), u32)
            p1_ref[0, c] = pltpu.bitcast(p1.astype(bf16), u32)
            if not stable:
                tied = jnp.maximum(tied, tied_c)
        return tied

    tied = select_all(stable=False)

    @pl.when(jnp.max(tied) > 0.0)
    def _():
        select_all(stable=True)


def _top_rows_stable(s, k):
    rows = []
    idx = lax.broadcasted_iota(jnp.int32, s.shape, 0)
    rank = jnp.full(s.shape, float(k), f32)
    for j in range(k):
        m = jnp.max(s, axis=0, keepdims=True)
        rows.append(m)
        first = jnp.min(jnp.where(s == m, idx, s.shape[0]), axis=0, keepdims=True)
        pick = idx == first
        rank = jnp.where(pick, float(j), rank)
        s = jnp.where(pick, -jnp.inf, s)
    return rows, rank


def _select_slab(s0, s1, *, stable):
    if stable:
        a, rank0 = _top_rows_stable(s0, PEER_TOPK)
        b, rank1 = _top_rows_stable(s1, PEER_TOPK)
    else:
        a = _top_rows(s0, PEER_TOPK)
        b, rank1 = _top_rows(s1, PEER_TOPK, with_rank=True)
    row = lax.broadcasted_iota(jnp.int32, (PEER_TOPK, s0.shape[1]), 0)
    bmat = jnp.zeros((PEER_TOPK, s0.shape[1]), f32)
    for j in range(PEER_TOPK):
        bmat = jnp.where(row == j, b[j], bmat)
    row8 = lax.broadcasted_iota(jnp.int32, (SUBLANES, s0.shape[1]), 0)
    ahi = jnp.zeros((SUBLANES, s0.shape[1]), f32)
    for n in range(SUBLANES):
        ahi = jnp.where(row8 == n, a[SUBLANES + n], ahi)
    cand = jnp.concatenate([a[0] + bmat] + [a[i] + bmat[:SUBLANES] for i in range(1, SUBLANES)]
                           + [ahi + b[0]], axis=0)
    rows_of = [PEER_TOPK] + [SUBLANES] * (SUBLANES - 1) + [1] * SUBLANES
    if stable:
        picked = _top_rows_stable(cand, PEER_TOPK)[1] < float(PEER_TOPK)
    else:
        picked = cand >= _top_rows(cand, PEER_TOPK)[-1]
    picked = jnp.where(picked, 1.0, 0.0)
    cmax = a[0] + b[0]
    z = jnp.sum(picked * jnp.exp(cand - cmax), axis=0, keepdims=True)
    cnt = jnp.zeros_like(s0)
    total = jnp.zeros_like(z)
    lo = 0
    for i in range(PEER_TOPK):
        hi = lo + rows_of[i]
        cnt_i = jnp.sum(picked[lo:hi], axis=0, keepdims=True)
        lo = hi
        total = total + cnt_i
        cnt = jnp.where((rank0 == float(i)) if stable else (s0 == a[i]), cnt_i, cnt)
    p0 = jnp.exp(s0 - a[0]) / z
    p1 = jnp.exp(s1 - b[0])
    if stable:
        return cnt, rank1, p0, p1, None
    k = float(PEER_TOPK)
    n0 = jnp.sum(jnp.where(s0 >= a[-1], 1.0, 0.0), axis=0, keepdims=True)
    n1 = jnp.sum(jnp.where(rank1 < k, 1.0, 0.0), axis=0, keepdims=True)
    tied = jnp.abs(n0 - k) + jnp.abs(n1 - k) + jnp.abs(total - k)
    return cnt, rank1, p0, p1, tied


def _bf16_twice(t):
    bits = pltpu.bitcast(t.astype(bf16).astype(f32), u32) >> 16
    return bits | (bits << 16)


def peer_select(x, wq_b, keys_b, *, tq):
    n, d = x.shape
    nh = PEER_HEADS
    full_spec = pl.BlockSpec((1, tq // LANES, PEER_NKEYS, LANES), lambda i, h: (h, i, 0, 0))
    half_spec = pl.BlockSpec((1, tq // LANES, PEER_NKEYS // 2, LANES), lambda i, h: (h, i, 0, 0))
    full = jax.ShapeDtypeStruct((nh, n // LANES, PEER_NKEYS, LANES), u32)
    half = jax.ShapeDtypeStruct((nh, n // LANES, PEER_NKEYS // 2, LANES), u32)
    return pl.pallas_call(
        _peer_select_kernel,
        grid=(n // tq, nh),
        in_specs=[pl.BlockSpec((tq, d), lambda i, h: (i, 0)),
                  pl.BlockSpec((d, 2 * nh * PEER_DHALF), lambda i, h: (0, 0)),
                  pl.BlockSpec((1, 2, PEER_NKEYS, PEER_DHALF), lambda i, h: (h, 0, 0, 0))],
        out_specs=[full_spec, half_spec, full_spec, half_spec],
        out_shape=[full, half, full, half],
        scratch_shapes=[pltpu.VMEM((tq, 2 * nh * PEER_DHALF), f32)],
        compiler_params=pltpu.CompilerParams(
            dimension_semantics=("arbitrary", "arbitrary"), vmem_limit_bytes=VMEM_LIMIT),
        name="peer_select",
    )(x, wq_b, keys_b)


def _peer_main_kernel(xb_ref, x_ref, cnt_ref, rk1_ref, p0_ref, p1_ref, u_ref, vt_ref,
                      g_ref, b_ref, o_ref, acc_ref, *slab_refs, te, tt):
    j = pl.program_id(1)
    n_i0 = te // PEER_NKEYS
    n_slab = tt // LANES
    per_chunk = MXU_COLS // LANES
    at_refs, ga_refs = slab_refs[:n_slab], slab_refs[n_slab:]
    half = PEER_NKEYS // 2
    unpack = lambda words: pltpu.bitcast(words, bf16)

    @pl.when(j == 0)
    def _():
        acc_ref[...] = jnp.zeros_like(acc_ref)

    def act_chunk(c):
        at = lax.dot_general(u_ref[...], xb_ref[c * MXU_COLS:(c + 1) * MXU_COLS, :],
                             (((1,), (1,)), ((), ())), preferred_element_type=f32)
        for k in range(per_chunk):
            at_refs[c * per_chunk + k][...] = at[:, k * LANES:(k + 1) * LANES]

    def gate_slab(s):
        for il in range(n_i0):
            g = jnp.zeros((PEER_NKEYS, LANES), bf16)
            for h in range(PEER_HEADS):
                cntrow = jnp.broadcast_to(cnt_ref[h, s, il:il + 1, :], (half, LANES))
                p0row = jnp.broadcast_to(p0_ref[h, s, il:il + 1, :], (half, LANES))
                sel = unpack(rk1_ref[h, s]) < unpack(cntrow)
                g = g + jnp.where(sel, unpack(p1_ref[h, s]) * unpack(p0row), jnp.zeros((), bf16))
            ga = _gelu_tanh(at_refs[s][il * PEER_NKEYS:(il + 1) * PEER_NKEYS, :].astype(bf16)) * g
            ga_refs[s][il * half:(il + 1) * half, :] = pltpu.bitcast(ga, u32)

    def out_chunk(c):
        ga = jnp.concatenate([unpack(ga_refs[c * per_chunk + k][...]) for k in range(per_chunk)], axis=1)
        lanes = slice(c * MXU_COLS, (c + 1) * MXU_COLS)
        acc_ref[:, lanes] += jnp.dot(vt_ref[...], ga, preferred_element_type=f32)

    n_chunk = tt // MXU_COLS
    act_chunk(0)
    for c in range(n_chunk):
        gate_slab(c * per_chunk)
        if c + 1 < n_chunk:
            act_chunk(c + 1)
        for k in range(1, per_chunk):
            gate_slab(c * per_chunk + k)
        out_chunk(c)

    @pl.when(j == pl.num_programs(1) - 1)
    def _():
        z = ALPHA * x_ref[...] + acc_ref[...].T
        o_ref[...] = _layer_norm_rows(z, g_ref[...], b_ref[...])


def peer_main(x, xb, stats, u_b, vt_b, ln_g, ln_b, *, tt, te):
    n, d = x.shape
    e = u_b.shape[0]
    nh = PEER_HEADS
    cnt, rk1, p0, p1 = stats
    full_spec = pl.BlockSpec((nh, tt // LANES, te // PEER_NKEYS, LANES), lambda i, j: (0, i, j, 0))
    half_spec = pl.BlockSpec((nh, tt // LANES, PEER_NKEYS // 2, LANES), lambda i, j: (0, i, 0, 0))
    return pl.pallas_call(
        functools.partial(_peer_main_kernel, te=te, tt=tt),
        grid=(n // tt, e // te),
        in_specs=[pl.BlockSpec((tt, d), lambda i, j: (i, 0)),
                  pl.BlockSpec((tt, d), lambda i, j: (i, 0)),
                  full_spec, half_spec, full_spec, half_spec,
                  pl.BlockSpec((te, d), lambda i, j: (j, 0)),
                  pl.BlockSpec((d, te), lambda i, j: (0, j)),
                  pl.BlockSpec((1, d), lambda i, j: (0, 0)),
                  pl.BlockSpec((1, d), lambda i, j: (0, 0))],
        out_specs=pl.BlockSpec((tt, d), lambda i, j: (i, 0)),
        out_shape=jax.ShapeDtypeStruct((n, d), f32),
        scratch_shapes=([pltpu.VMEM((d, tt), f32)]
                        + [pltpu.VMEM((te, LANES), f32)] * (tt // LANES)
                        + [pltpu.VMEM((te // 2, LANES), u32)] * (tt // LANES)),
        compiler_params=pltpu.CompilerParams(
            dimension_semantics=("arbitrary", "arbitrary"), vmem_limit_bytes=VMEM_LIMIT),
        name="peer_main",
    )(xb, x, cnt, rk1, p0, p1, u_b, vt_b, ln_g.reshape(1, d), ln_b.reshape(1, d))


def peer_layer(x, xb, w_q, sub_keys, u, v, ln_g, ln_b, *, tq=256, tt=1024, te=1024):
    stats = peer_select(xb, w_q.astype(bf16), sub_keys.astype(bf16), tq=tq)
    return peer_main(x, xb, stats, u.astype(bf16), v.T.astype(bf16), ln_g, ln_b, tt=tt, te=te)


def _rglru_kernel(x_ref, win_ref, cw_ref, vec_ref, wg_ref, wout_ref, lng_ref, lnb_ref, o_ref, ob_ref,
                  tail_ref, h_ref, *, tiles_per_seq):
    i = pl.program_id(0)
    w = RG_WIDTH
    x = x_ref[...]
    ts = x.shape[0]

    @pl.when(i % tiles_per_seq == 0)
    def _():
        tail_ref[...] = jnp.zeros_like(tail_ref)
        h_ref[...] = jnp.zeros_like(h_ref)

    conv_b, b_a, b_x, lam = (vec_ref[n:n + 1, :] for n in range(4))
    hin = jnp.dot(x.astype(bf16), win_ref[...], preferred_element_type=f32)
    gate_branch = _gelu_tanh(hin[:, :w])
    hx = hin[:, w:]

    row = lax.broadcasted_iota(jnp.int32, (ts, w), 0)
    row8 = lax.broadcasted_iota(jnp.int32, (SUBLANES, w), 0)
    tail = tail_ref[...]
    xc = hx * cw_ref[RG_CONV - 1:RG_CONV, :] + conv_b
    for s in range(1, RG_CONV):
        rolled = pltpu.roll(hx, s, axis=0)
        head = jnp.where(row8 < s, pltpu.roll(tail, s, axis=0), rolled[:SUBLANES])
        shifted = jnp.concatenate([head, rolled[SUBLANES:]], axis=0)
        xc = xc + shifted * cw_ref[RG_CONV - 1 - s:RG_CONV - s, :]
    tail_ref[...] = hx[ts - SUBLANES:]

    xcb = xc.astype(bf16)
    bands = [jnp.dot(xcb[:, lo:lo + RG_BAND], wg_ref[j], preferred_element_type=f32)
             for j, lo in enumerate(_rg_band_starts())]
    r = _sigmoid(jnp.concatenate([g[:, :LANES] for g in bands], axis=1) + b_a)
    ig = _sigmoid(jnp.concatenate([g[:, LANES:] for g in bands], axis=1) + b_x)
    log_a = (-RG_C * _softplus(-lam)) * r
    a = jnp.exp(log_a)
    u = jnp.sqrt(-jnp.tanh(log_a) * (a * a + 1.0)) * (ig * xc)

    sub = row & (SUBLANES - 1)
    s = 1
    while s < SUBLANES:
        keep = sub >= s
        a_sh = jnp.where(keep, pltpu.roll(a, s, axis=0), 1.0)
        u_sh = jnp.where(keep, pltpu.roll(u, s, axis=0), 0.0)
        u = a * u_sh + u
        a = a * a_sh
        s *= 2
    carry = h_ref[SUBLANES - 1:SUBLANES, :]
    groups = []
    for g in range(0, ts, SUBLANES):
        hg = u[g:g + SUBLANES] + a[g:g + SUBLANES] * carry
        groups.append(hg)
        carry = hg[SUBLANES - 1:]
    hs = jnp.concatenate(groups, axis=0)
    h_ref[...] = groups[-1]

    m = jnp.dot((hs * gate_branch).astype(bf16), wout_ref[...], preferred_element_type=f32)
    out = _layer_norm_rows(ALPHA * x + m, lng_ref[...], lnb_ref[...])
    o_ref[...] = out
    ob_ref[...] = out.astype(bf16)


def rglru_layer(x, seq, w_in, conv_w, conv_b, w_a, b_a, w_x, b_x, lam, w_out, ln_g, ln_b, *, ts=256):
    n, d = x.shape
    w = RG_WIDTH
    blockdiag = lambda t: jax.scipy.linalg.block_diag(*t)
    full_a, full_x = blockdiag(w_a), blockdiag(w_x)
    w_gates = jnp.stack([
        jnp.concatenate([m[lo:lo + RG_BAND, j * LANES:(j + 1) * LANES] for m in (full_a, full_x)], axis=1)
        for j, lo in enumerate(_rg_band_starts())]).astype(bf16)
    vecs = jnp.stack([conv_b, b_a.reshape(w), b_x.reshape(w), lam.reshape(w)])
    consts = (w_in.astype(bf16), conv_w, vecs, w_gates, w_out.astype(bf16), ln_g.reshape(1, d), ln_b.reshape(1, d))
    full = lambda arr: pl.BlockSpec(arr.shape, lambda i: (0,) * arr.ndim, pipeline_mode=pl.Buffered(1))
    tok = pl.BlockSpec((ts, d), lambda i: (i, 0))
    return pl.pallas_call(
        functools.partial(_rglru_kernel, tiles_per_seq=seq // ts),
        grid=(n // ts,),
        in_specs=[tok] + [full(c) for c in consts],
        out_specs=[tok, tok],
        out_shape=[jax.ShapeDtypeStruct((n, d), f32), jax.ShapeDtypeStruct((n, d), bf16)],
        scratch_shapes=[pltpu.VMEM((SUBLANES, w), f32), pltpu.VMEM((SUBLANES, w), f32)],
        compiler_params=pltpu.CompilerParams(
            dimension_semantics=("arbitrary",), vmem_limit_bytes=VMEM_LIMIT),
        name="rglru",
    )(x, *consts)


def _sigmoid(x):
    return 1.0 / (1.0 + jnp.exp(-x))


def _softplus(x):
    return jnp.maximum(x, 0.0) + jnp.log1p(jnp.exp(-jnp.abs(x)))


def _head_sums(t, ones_ref):
    hi = t.astype(bf16)
    lo = (t - hi.astype(f32)).astype(bf16)
    return (jnp.dot(hi, ones_ref[...], preferred_element_type=f32)
            + jnp.dot(lo, ones_ref[...], preferred_element_type=f32))


def _rwkv_proj_kernel(x_ref, xp_ref, mix_ref, wr_ref, wk_ref, wv_ref, w1_ref, w2_ref, a1_ref, a2_ref,
                      g1_ref, g2_ref, vec_ref,
                      r_ref, w_ref, k_ref, v_ref, gate_ref, g_ref, *, tiles_per_seq):
    i = pl.program_id(0)
    x = x_ref[...]
    ts = x.shape[0]
    prev_last = jnp.where(i % tiles_per_seq == 0, 0.0, xp_ref[SUBLANES - 1:SUBLANES, :])
    row = lax.broadcasted_iota(jnp.int32, x.shape, 0)
    xprev = jnp.where(row == 0, prev_last, pltpu.roll(x, 1, axis=0))
    xx = xprev - x

    def mixed(m):
        return (x + xx * mix_ref[m:m + 1, :]).astype(bf16)

    def mm(a, w_ref_):
        return jnp.dot(a, w_ref_[...], preferred_element_type=f32)

    w0, a0 = (vec_ref[n:n + 1, :] for n in range(2))
    lw = mm(jnp.tanh(mm(mixed(1), w1_ref)).astype(bf16), w2_ref)
    w_log = -_softplus(-(w0 + lw)) - 0.5
    r_ref[...] = mm(mixed(0), wr_ref).astype(bf16)
    w_ref[...] = jnp.exp(w_log).astype(bf16)
    k_ref[...] = mm(mixed(2), wk_ref).astype(bf16)
    v_ref[...] = mm(mixed(3), wv_ref).astype(bf16)
    gate_ref[...] = _sigmoid(a0 + mm(mm(mixed(4), a1_ref).astype(bf16), a2_ref)).astype(bf16)
    g_ref[...] = mm(_sigmoid(mm(mixed(5), g1_ref)).astype(bf16), g2_ref)


def rwkv_proj(x, seq, mix, w_r, w_k, w_v, w1, w2, a1, a2, g1, g2, vecs, *, ts):
    n, d = x.shape
    full = lambda arr: pl.BlockSpec(arr.shape, lambda i: (0,) * arr.ndim)
    tok = pl.BlockSpec((ts, d), lambda i: (i, 0))
    prev = pl.BlockSpec((SUBLANES, d), lambda i: (jnp.maximum(i * (ts // SUBLANES) - 1, 0), 0))
    weights = (mix, w_r, w_k, w_v, w1, w2, a1, a2, g1, g2, vecs)
    out = lambda dt: jax.ShapeDtypeStruct((n, d), dt)
    return pl.pallas_call(
        functools.partial(_rwkv_proj_kernel, tiles_per_seq=seq // ts),
        grid=(n // ts,),
        in_specs=[tok, prev] + [full(w) for w in weights],
        out_specs=[tok] * 6,
        out_shape=[out(bf16), out(bf16), out(bf16), out(bf16), out(bf16), out(f32)],
        compiler_params=pltpu.CompilerParams(
            dimension_semantics=("arbitrary",), vmem_limit_bytes=VMEM_LIMIT),
        name="rwkv_proj",
    )(x, x, *weights)


def _rwkv_scan_kernel(rin_ref, win_ref, k_ref, vin_ref, gate_ref, knext_ref, kk_ref, ka_ref, o_ref,
                      s_ref, sa_ref, a_ref, b_ref, km_ref, r_ref, v_ref, w_ref):
    nk = s_ref.shape[0]
    tc = rin_ref.shape[0]
    r_ref[...] = rin_ref[...].astype(f32)
    v_ref[...] = vin_ref[...].astype(f32)
    w_ref[...] = jnp.exp(-win_ref[...].astype(f32))

    @pl.when(pl.program_id(0) == 0)
    def _():
        s_ref[...] = jnp.zeros_like(s_ref)
        sa_ref[...] = jnp.zeros_like(sa_ref)

    def unit_key(k):
        kk = k * kk_ref[...]
        norm = jnp.sqrt(jnp.sum(kk * kk, axis=-2, keepdims=True))
        return kk / jnp.maximum(norm, RWKV_L2_EPS)

    k_blk = k_ref[...].astype(f32)
    gate = gate_ref[...].astype(f32)
    kk_blk = unit_key(k_blk)
    a_ref[0:tc] = -kk_blk
    a_ref[tc:tc + 1] = -unit_key(knext_ref[...].astype(f32))
    b_ref[...] = kk_blk * gate
    km_ref[...] = k_blk * (1.0 + (gate - 1.0) * ka_ref[...])

    def step(t, sa):
        vt = v_ref[t]
        o = [jnp.zeros(s_ref.shape[1:], f32)] * 2
        nsa = [jnp.zeros(s_ref.shape[1:], f32)] * 2
        for kk in range(nk):
            new = (s_ref[kk] * w_ref[t, kk:kk + 1, :]
                   + (sa * b_ref[t, kk:kk + 1, :] + vt * km_ref[t, kk:kk + 1, :]))
            s_ref[kk] = new
            o[kk % 2] = o[kk % 2] + new * r_ref[t, kk:kk + 1, :]
            nsa[kk % 2] = nsa[kk % 2] + new * a_ref[t + 1, kk:kk + 1, :]
        ot = o[0] + o[1]
        mu = jnp.mean(ot, axis=0, keepdims=True)
        oc = ot - mu
        var = jnp.mean(oc * oc, axis=0, keepdims=True)
        o_ref[t] = (oc * lax.rsqrt(var + RWKV_GN_EPS)).astype(o_ref.dtype)
        return nsa[0] + nsa[1]

    sa_ref[...] = lax.fori_loop(0, tc, step, sa_ref[...])


def rwkv_scan(r, w, k, v, gate, kk_vec, ka_vec, *, tc):
    s, hd, chains = r.shape
    blk = pl.BlockSpec((tc, hd, chains), lambda i: (i, 0, 0))
    nxt = pl.BlockSpec((1, hd, chains), lambda i: (jnp.minimum((i + 1) * tc, s - 1), 0, 0))
    vec = pl.BlockSpec((hd, chains), lambda i: (0, 0))
    return pl.pallas_call(
        _rwkv_scan_kernel,
        grid=(s // tc,),
        in_specs=[blk] * 5 + [nxt, vec, vec],
        out_specs=blk,
        out_shape=jax.ShapeDtypeStruct((s, hd, chains), bf16),
        scratch_shapes=[pltpu.VMEM((hd, hd, chains), f32), pltpu.VMEM((hd, chains), f32),
                        pltpu.VMEM((tc + 1, hd, chains), f32)] + [pltpu.VMEM((tc, hd, chains), f32)] * 5,
        compiler_params=pltpu.CompilerParams(
            dimension_semantics=("arbitrary",), vmem_limit_bytes=VMEM_LIMIT),
        name="rwkv_scan",
    )(r, w, k, v, gate, k, kk_vec, ka_vec)


def _rwkv_out_kernel(x_ref, o_ref, r_ref, k_ref, gate_ref, v_ref, g_ref, vec_ref, ones_ref, wo_ref,
                     lng_ref, lnb_ref, out_ref, outb_ref):
    lnx_g, lnx_b, r_k, k_a = (vec_ref[n:n + 1, :] for n in range(4))
    widen = lambda ref: ref[...].astype(f32)
    k_mod = widen(k_ref) * (1.0 + (widen(gate_ref) - 1.0) * k_a)
    bonus = _head_sums(widen(r_ref) * k_mod * r_k, ones_ref) * widen(v_ref)
    y = (widen(o_ref) * lnx_g + lnx_b + bonus) * g_ref[...]
    m = jnp.dot(y.astype(bf16), wo_ref[...], preferred_element_type=f32)
    out = _layer_norm_rows(ALPHA * x_ref[...] + m, lng_ref[...], lnb_ref[...])
    out_ref[...] = out
    outb_ref[...] = out.astype(bf16)


def rwkv_out(x, o, r, k, gate, v, g, vecs, ones, w_o, ln_g, ln_b, *, ts):
    n, d = x.shape
    full = lambda arr: pl.BlockSpec(arr.shape, lambda i: (0,) * arr.ndim)
    tok = pl.BlockSpec((ts, d), lambda i: (i, 0))
    consts = (vecs, ones, w_o, ln_g.reshape(1, d), ln_b.reshape(1, d))
    return pl.pallas_call(
        _rwkv_out_kernel,
        grid=(n // ts,),
        in_specs=[tok] * 7 + [full(c) for c in consts],
        out_specs=[tok, tok],
        out_shape=[jax.ShapeDtypeStruct((n, d), f32), jax.ShapeDtypeStruct((n, d), bf16)],
        compiler_params=pltpu.CompilerParams(
            dimension_semantics=("arbitrary",), vmem_limit_bytes=VMEM_LIMIT),
        name="rwkv_out",
    )(x, o, r, k, gate, v, g, *consts)


def _head_ones(d, head):
    seg = jnp.arange(d) // head
    return (seg[:, None] == seg[None, :]).astype(bf16)


def rwkv_layer(x, seq, mix, w_r, w_k, w_v, w0, w1, w2, a0, a1, a2, g1, g2, k_k, k_a, r_k, lnx_g, lnx_b, w_o,
               ln_g, ln_b, *, ts=256, tc=16):
    n, d = x.shape
    bsz = n // seq
    nh, hd = RWKV_HEADS, RWKV_HEAD
    ones = _head_ones(d, hd)
    c = lambda w: w.astype(bf16)
    r, w, k, v, gate, g = rwkv_proj(x, seq, mix, c(w_r), c(w_k), c(w_v), c(w1), c(w2), c(a1), c(a2),
                                    c(g1), c(g2), jnp.stack([w0, a0]), ts=ts)
    to_chains = lambda t: t.reshape(bsz, seq, nh, hd).transpose(1, 3, 0, 2).reshape(seq, hd, bsz * nh)
    vec_chains = lambda t: jnp.tile(t.reshape(nh, hd).T, (1, bsz))
    o = rwkv_scan(*(to_chains(t) for t in (r, w, k, v, gate)), vec_chains(k_k), vec_chains(k_a), tc=tc)
    o = o.reshape(seq, hd, bsz, nh).transpose(2, 0, 3, 1).reshape(n, d)
    vecs_out = jnp.stack([lnx_g, lnx_b, r_k.reshape(d), k_a])
    return rwkv_out(x, o, r, k, gate, v, g, vecs_out, ones, c(w_o), ln_g, ln_b, ts=ts)


def kernel(x, rg_w_in, rg_conv_w, rg_conv_b, rg_w_a, rg_b_a, rg_w_x, rg_b_x, rg_lambda, rg_w_out, rw_mix, rw_w_r, rw_w_k, rw_w_v, rw_w0, rw_w1, rw_w2, rw_a0, rw_a1, rw_a2, rw_g1, rw_g2, rw_k_k, rw_k_a, rw_r_k, rw_lnx_g, rw_lnx_b, rw_w_o, peer_w_q, peer_sub_keys, peer_u, peer_v, ln_g, ln_b):
    bsz, s, d = x.shape
    x = x.reshape(bsz * s, d)
    for i in range(DEPTH):
        j = i // 2
        if i % 2 == 0:
            x1, x1b = rglru_layer(x, s, rg_w_in[j], rg_conv_w[j], rg_conv_b[j], rg_w_a[j], rg_b_a[j],
                                  rg_w_x[j], rg_b_x[j], rg_lambda[j], rg_w_out[j], ln_g[i, 0], ln_b[i, 0])
        else:
            x1, x1b = rwkv_layer(x, s, rw_mix[j], rw_w_r[j], rw_w_k[j], rw_w_v[j], rw_w0[j], rw_w1[j],
                                 rw_w2[j], rw_a0[j], rw_a1[j], rw_a2[j], rw_g1[j], rw_g2[j],
                                 rw_k_k[j], rw_k_a[j], rw_r_k[j], rw_lnx_g[j], rw_lnx_b[j], rw_w_o[j],
                                 ln_g[i, 0], ln_b[i, 0])
        x = peer_layer(x1, x1b, peer_w_q[i], peer_sub_keys[i], peer_u[i], peer_v[i], ln_g[i, 1], ln_b[i, 1])
    return x.reshape(bsz, s, d)
```

```python
import functools
import math
import jax, jax.numpy as jnp
from jax import lax
from jax.experimental import pallas as pl
from jax.experimental.pallas import tpu as pltpu

DEPTH = 2
RG_WIDTH = 1408
RG_BLOCK = 88
RG_CONV = 4
RG_C = 8.0
RWKV_HEAD = 64
RWKV_HEADS = 16
RWKV_GN_EPS = 64e-5
RWKV_L2_EPS = 1e-12
PEER_HEADS = 8
PEER_NKEYS = 128
PEER_DHALF = 128
PEER_TOPK = 16
ALPHA = (2 * DEPTH) ** 0.25
LN_EPS = 1e-5

LANES = 128
SUBLANES = 8
MXU_COLS = 256
VMEM_LIMIT = 56 << 20

f32 = jnp.float32
bf16 = jnp.bfloat16
u32 = jnp.uint32

RG_BAND = 3 * LANES


def _rg_band_starts():
    n_tiles = RG_WIDTH // LANES
    starts = [LANES * min(max(j - 1, 0), n_tiles - RG_BAND // LANES) for j in range(n_tiles)]
    for j, lo in enumerate(starts):
        first_head, last_head = (j * LANES) // RG_BLOCK, ((j + 1) * LANES - 1) // RG_BLOCK
        assert lo <= first_head * RG_BLOCK and (last_head + 1) * RG_BLOCK <= lo + RG_BAND
    return starts


def _gelu_tanh(x):
    return 0.5 * x * (1.0 + jnp.tanh(math.sqrt(2.0 / math.pi) * (x + 0.044715 * (x * x * x))))


def _layer_norm_rows(z, g, b):
    mu = jnp.mean(z, axis=-1, keepdims=True)
    zc = z - mu
    var = jnp.mean(zc * zc, axis=-1, keepdims=True)
    return zc * lax.rsqrt(var + LN_EPS) * g + b


def _top_rows(s, k, with_rank=False):
    rows = []
    rank = jnp.full(s.shape, float(k), f32)
    for j in range(k):
        m = jnp.max(s, axis=0, keepdims=True)
        rows.append(m)
        hit = s == m
        if with_rank:
            rank = jnp.where(hit, float(j), rank)
        s = jnp.where(hit, -jnp.inf, s)
    return (rows, rank) if with_rank else rows


def _peer_select_kernel(x_ref, wq_ref, keys_ref, cnt_ref, rk1_ref, p0_ref, p1_ref, q_ref):
    h = pl.program_id(1)

    @pl.when(h == 0)
    def _():
        q_ref[...] = jnp.dot(x_ref[...], wq_ref[...], preferred_element_type=f32)

    def scores(p):
        off = pl.multiple_of((2 * h + p) * PEER_DHALF, PEER_DHALF)
        qhp = q_ref[:, pl.ds(off, PEER_DHALF)].astype(bf16)
        return lax.dot_general(keys_ref[0, p], qhp, (((1,), (1,)), ((), ())),
                               preferred_element_type=f32)

    def select_all(stable):
        s0_all = scores(0)
        s1_all = scores(1)
        tied = 0.0
        for c in range(s0_all.shape[1] // LANES):
            lanes = slice(c * LANES, (c + 1) * LANES)
            cnt, rank1, p0, p1, tied_c = _select_slab(s0_all[:, lanes], s1_all[:, lanes], stable=stable)
            cnt_ref[0, c] = _bf16_twice(cnt)
            p0_ref[0, c] = _bf16_twice(p0)
            rk1_ref[0, c] = pltpu.bitcast(rank1.astype(bf16), u32)
            p1_ref[0, c] = pltpu.bitcast(p1.astype(bf16), u32)
            if not stable:
                tied = jnp.maximum(tied, jnp.max(tied_c))
        return tied

    tied = select_all(stable=False)

    @pl.when(tied > 0.0)
    def _():
        select_all(stable=True)


def _top_rows_stable(s, k):
    rows = []
    idx = lax.broadcasted_iota(jnp.int32, s.shape, 0)
    rank = jnp.full(s.shape, float(k), f32)
    for j in range(k):
        m = jnp.max(s, axis=0, keepdims=True)
        rows.append(m)
        first = jnp.min(jnp.where(s == m, idx, s.shape[0]), axis=0, keepdims=True)
        pick = idx == first
        rank = jnp.where(pick, float(j), rank)
        s = jnp.where(pick, -jnp.inf, s)
    return rows, rank


def _select_slab(s0, s1, *, stable):
    if stable:
        a, rank0 = _top_rows_stable(s0, PEER_TOPK)
        b, rank1 = _top_rows_stable(s1, PEER_TOPK)
    else:
        a = _top_rows(s0, PEER_TOPK)
        b, rank1 = _top_rows(s1, PEER_TOPK, with_rank=True)
    row = lax.broadcasted_iota(jnp.int32, (PEER_TOPK, s0.shape[1]), 0)
    bmat = jnp.zeros((PEER_TOPK, s0.shape[1]), f32)
    for j in range(PEER_TOPK):
        bmat = jnp.where(row == j, b[j], bmat)
    row8 = lax.broadcasted_iota(jnp.int32, (SUBLANES, s0.shape[1]), 0)
    ahi = jnp.zeros((SUBLANES, s0.shape[1]), f32)
    for n in range(SUBLANES):
        ahi = jnp.where(row8 == n, a[SUBLANES + n], ahi)
    cand = jnp.concatenate([a[0] + bmat] + [a[i] + bmat[:SUBLANES] for i in range(1, SUBLANES)]
                           + [ahi + b[0]], axis=0)
    rows_of = [PEER_TOPK] + [SUBLANES] * (SUBLANES - 1) + [1] * SUBLANES
    if stable:
        picked = _top_rows_stable(cand, PEER_TOPK)[1] < float(PEER_TOPK)
    else:
        picked = cand >= _top_rows(cand, PEER_TOPK)[-1]
    picked = jnp.where(picked, 1.0, 0.0)
    cmax = a[0] + b[0]
    z = jnp.sum(picked * jnp.exp(cand - cmax), axis=0, keepdims=True)
    cnt = jnp.zeros_like(s0)
    total = jnp.zeros_like(z)
    lo = 0
    for i in range(PEER_TOPK):
        hi = lo + rows_of[i]
        cnt_i = jnp.sum(picked[lo:hi], axis=0, keepdims=True)
        lo = hi
        total = total + cnt_i
        cnt = jnp.where((rank0 == float(i)) if stable else (s0 == a[i]), cnt_i, cnt)
    p0 = jnp.exp(s0 - a[0]) / z
    p1 = jnp.exp(s1 - b[0])
    if stable:
        return cnt, rank1, p0, p1, None
    k = float(PEER_TOPK)
    n0 = jnp.sum(jnp.where(s0 >= a[-1], 1.0, 0.0), axis=0, keepdims=True)
    n1 = jnp.sum(jnp.where(rank1 < k, 1.0, 0.0), axis=0, keepdims=True)
    tied = jnp.abs(n0 - k) + jnp.abs(n1 - k) + jnp.abs(total - k)
    return cnt, rank1, p0, p1, tied


def _bf16_twice(t):
    bits = pltpu.bitcast(t.astype(bf16).astype(f32), u32) >> 16
    return bits | (bits << 16)


def peer_select(x, wq_b, keys_b, *, tq):
    n, d = x.shape
    nh = PEER_HEADS
    full_spec = pl.BlockSpec((1, tq // LANES, PEER_NKEYS, LANES), lambda i, h: (h, i, 0, 0))
    half_spec = pl.BlockSpec((1, tq // LANES, PEER_NKEYS // 2, LANES), lambda i, h: (h, i, 0, 0))
    full = jax.ShapeDtypeStruct((nh, n // LANES, PEER_NKEYS, LANES), u32)
    half = jax.ShapeDtypeStruct((nh, n // LANES, PEER_NKEYS // 2, LANES), u32)
    return pl.pallas_call(
        _peer_select_kernel,
        grid=(n // tq, nh),
        in_specs=[pl.BlockSpec((tq, d), lambda i, h: (i, 0)),
                  pl.BlockSpec((d, 2 * nh * PEER_DHALF), lambda i, h: (0, 0)),
                  pl.BlockSpec((1, 2, PEER_NKEYS, PEER_DHALF), lambda i, h: (h, 0, 0, 0))],
        out_specs=[full_spec, half_spec, full_spec, half_spec],
        out_shape=[full, half, full, half],
        scratch_shapes=[pltpu.VMEM((tq, 2 * nh * PEER_DHALF), f32)],
        compiler_params=pltpu.CompilerParams(
            dimension_semantics=("arbitrary", "arbitrary"), vmem_limit_bytes=VMEM_LIMIT),
        name="peer_select",
    )(x, wq_b, keys_b)


def _peer_main_kernel(xb_ref, x_ref, cnt_ref, rk1_ref, p0_ref, p1_ref, u_ref, vt_ref,
                      g_ref, b_ref, o_ref, acc_ref, *slab_refs, te, tt):
    j = pl.program_id(1)
    n_i0 = te // PEER_NKEYS
    n_slab = tt // LANES
    per_chunk = MXU_COLS // LANES
    at_refs, ga_refs = slab_refs[:n_slab], slab_refs[n_slab:]
    half = PEER_NKEYS // 2
    unpack = lambda words: pltpu.bitcast(words, bf16)

    @pl.when(j == 0)
    def _():
        acc_ref[...] = jnp.zeros_like(acc_ref)

    def act_chunk(c):
        at = lax.dot_general(u_ref[...], xb_ref[c * MXU_COLS:(c + 1) * MXU_COLS, :],
                             (((1,), (1,)), ((), ())), preferred_element_type=f32)
        for k in range(per_chunk):
            at_refs[c * per_chunk + k][...] = at[:, k * LANES:(k + 1) * LANES]

    def gate_slab(s):
        for il in range(n_i0):
            g = jnp.zeros((PEER_NKEYS, LANES), bf16)
            for h in range(PEER_HEADS):
                cntrow = jnp.broadcast_to(cnt_ref[h, s, il:il + 1, :], (half, LANES))
                p0row = jnp.broadcast_to(p0_ref[h, s, il:il + 1, :], (half, LANES))
                sel = unpack(rk1_ref[h, s]) < unpack(cntrow)
                g = g + jnp.where(sel, unpack(p1_ref[h, s]) * unpack(p0row), jnp.zeros((), bf16))
            ga = _gelu_tanh(at_refs[s][il * PEER_NKEYS:(il + 1) * PEER_NKEYS, :].astype(bf16)) * g
            ga_refs[s][il * half:(il + 1) * half, :] = pltpu.bitcast(ga, u32)

    def out_chunk(c):
        ga = jnp.concatenate([unpack(ga_refs[c * per_chunk + k][...]) for k in range(per_chunk)], axis=1)
        lanes = slice(c * MXU_COLS, (c + 1) * MXU_COLS)
        acc_ref[:, lanes] += jnp.dot(vt_ref[...], ga, preferred_element_type=f32)

    n_chunk = tt // MXU_COLS
    act_chunk(0)
    for c in range(n_chunk):
        gate_slab(c * per_chunk)
        if c + 1 < n_chunk:
            act_chunk(c + 1)
        for k in range(1, per_chunk):
            gate_slab(c * per_chunk + k)
        out_chunk(c)

    @pl.when(j == pl.num_programs(1) - 1)
    def _():
        z = ALPHA * x_ref[...] + acc_ref[...].T
        o_ref[...] = _layer_norm_rows(z, g_ref[...], b_ref[...])


def peer_main(x, xb, stats, u_b, vt_b, ln_g, ln_b, *, tt, te):
    n, d = x.shape
    e = u_b.shape[0]
    nh = PEER_HEADS
    cnt, rk1, p0, p1 = stats
    full_spec = pl.BlockSpec((nh, tt // LANES, te // PEER_NKEYS, LANES), lambda i, j: (0, i, j, 0))
    half_spec = pl.BlockSpec((nh, tt // LANES, PEER_NKEYS // 2, LANES), lambda i, j: (0, i, 0, 0))
    return pl.pallas_call(
        functools.partial(_peer_main_kernel, te=te, tt=tt),
        grid=(n // tt, e // te),
        in_specs=[pl.BlockSpec((tt, d), lambda i, j: (i, 0)),
                  pl.BlockSpec((tt, d), lambda i, j: (i, 0)),
                  full_spec, half_spec, full_spec, half_spec,
                  pl.BlockSpec((te, d), lambda i, j: (j, 0)),
                  pl.BlockSpec((d, te), lambda i, j: (0, j)),
                  pl.BlockSpec((1, d), lambda i, j: (0, 0)),
                  pl.BlockSpec((1, d), lambda i, j: (0, 0))],
        out_specs=pl.BlockSpec((tt, d), lambda i, j: (i, 0)),
        out_shape=jax.ShapeDtypeStruct((n, d), f32),
        scratch_shapes=([pltpu.VMEM((d, tt), f32)]
                        + [pltpu.VMEM((te, LANES), f32)] * (tt // LANES)
                        + [pltpu.VMEM((te // 2, LANES), u32)] * (tt // LANES)),
        compiler_params=pltpu.CompilerParams(
            dimension_semantics=("arbitrary", "arbitrary"), vmem_limit_bytes=VMEM_LIMIT),
        name="peer_main",
    )(xb, x, cnt, rk1, p0, p1, u_b, vt_b, ln_g.reshape(1, d), ln_b.reshape(1, d))


def peer_layer(x, xb, w_q, sub_keys, u, v, ln_g, ln_b, *, tq=256, tt=1024, te=1024):
    stats = peer_select(xb, w_q.astype(bf16), sub_keys.astype(bf16), tq=tq)
    return peer_main(x, xb, stats, u.astype(bf16), v.T.astype(bf16), ln_g, ln_b, tt=tt, te=te)


def _rglru_kernel(x_ref, win_ref, cw_ref, vec_ref, wg_ref, wout_ref, lng_ref, lnb_ref, o_ref, ob_ref,
                  tail_ref, h_ref, *, tiles_per_seq):
    i = pl.program_id(0)
    w = RG_WIDTH
    x = x_ref[...]
    ts = x.shape[0]

    @pl.when(i % tiles_per_seq == 0)
    def _():
        tail_ref[...] = jnp.zeros_like(tail_ref)
        h_ref[...] = jnp.zeros_like(h_ref)

    conv_b, b_a, b_x, lam = (vec_ref[n:n + 1, :] for n in range(4))
    hin = jnp.dot(x.astype(bf16), win_ref[...], preferred_element_type=f32)
    gate_branch = _gelu_tanh(hin[:, :w])
    hx = hin[:, w:]

    row = lax.broadcasted_iota(jnp.int32, (ts, w), 0)
    row8 = lax.broadcasted_iota(jnp.int32, (SUBLANES, w), 0)
    tail = tail_ref[...]
    xc = hx * cw_ref[RG_CONV - 1:RG_CONV, :] + conv_b
    for s in range(1, RG_CONV):
        rolled = pltpu.roll(hx, s, axis=0)
        head = jnp.where(row8 < s, pltpu.roll(tail, s, axis=0), rolled[:SUBLANES])
        shifted = jnp.concatenate([head, rolled[SUBLANES:]], axis=0)
        xc = xc + shifted * cw_ref[RG_CONV - 1 - s:RG_CONV - s, :]
    tail_ref[...] = hx[ts - SUBLANES:]

    xcb = xc.astype(bf16)
    bands = [jnp.dot(xcb[:, lo:lo + RG_BAND], wg_ref[j], preferred_element_type=f32)
             for j, lo in enumerate(_rg_band_starts())]
    r = _sigmoid(jnp.concatenate([g[:, :LANES] for g in bands], axis=1) + b_a)
    ig = _sigmoid(jnp.concatenate([g[:, LANES:] for g in bands], axis=1) + b_x)
    log_a = (-RG_C * _softplus(-lam)) * r
    a = jnp.exp(log_a)
    u = jnp.sqrt(-jnp.tanh(log_a) * (a * a + 1.0)) * (ig * xc)

    sub = row & (SUBLANES - 1)
    s = 1
    while s < SUBLANES:
        keep = sub >= s
        a_sh = jnp.where(keep, pltpu.roll(a, s, axis=0), 1.0)
        u_sh = jnp.where(keep, pltpu.roll(u, s, axis=0), 0.0)
        u = a * u_sh + u
        a = a * a_sh
        s *= 2
    carry = h_ref[SUBLANES - 1:SUBLANES, :]
    groups = []
    for g in range(0, ts, SUBLANES):
        hg = u[g:g + SUBLANES] + a[g:g + SUBLANES] * carry
        groups.append(hg)
        carry = hg[SUBLANES - 1:]
    hs = jnp.concatenate(groups, axis=0)
    h_ref[...] = groups[-1]

    m = jnp.dot((hs * gate_branch).astype(bf16), wout_ref[...], preferred_element_type=f32)
    out = _layer_norm_rows(ALPHA * x + m, lng_ref[...], lnb_ref[...])
    o_ref[...] = out
    ob_ref[...] = out.astype(bf16)


def rglru_layer(x, seq, w_in, conv_w, conv_b, w_a, b_a, w_x, b_x, lam, w_out, ln_g, ln_b, *, ts=256):
    n, d = x.shape
    w = RG_WIDTH
    blockdiag = lambda t: jax.scipy.linalg.block_diag(*t)
    full_a, full_x = blockdiag(w_a), blockdiag(w_x)
    w_gates = jnp.stack([
        jnp.concatenate([m[lo:lo + RG_BAND, j * LANES:(j + 1) * LANES] for m in (full_a, full_x)], axis=1)
        for j, lo in enumerate(_rg_band_starts())]).astype(bf16)
    vecs = jnp.stack([conv_b, b_a.reshape(w), b_x.reshape(w), lam.reshape(w)])
    consts = (w_in.astype(bf16), conv_w, vecs, w_gates, w_out.astype(bf16), ln_g.reshape(1, d), ln_b.reshape(1, d))
    full = lambda arr: pl.BlockSpec(arr.shape, lambda i: (0,) * arr.ndim, pipeline_mode=pl.Buffered(1))
    tok = pl.BlockSpec((ts, d), lambda i: (i, 0))
    return pl.pallas_call(
        functools.partial(_rglru_kernel, tiles_per_seq=seq // ts),
        grid=(n // ts,),
        in_specs=[tok] + [full(c) for c in consts],
        out_specs=[tok, tok],
        out_shape=[jax.ShapeDtypeStruct((n, d), f32), jax.ShapeDtypeStruct((n, d), bf16)],
        scratch_shapes=[pltpu.VMEM((SUBLANES, w), f32), pltpu.VMEM((SUBLANES, w), f32)],
        compiler_params=pltpu.CompilerParams(
            dimension_semantics=("arbitrary",), vmem_limit_bytes=VMEM_LIMIT),
        name="rglru",
    )(x, *consts)


def _sigmoid(x):
    return 1.0 / (1.0 + jnp.exp(-x))


def _softplus(x):
    return jnp.maximum(x, 0.0) + jnp.log1p(jnp.exp(-jnp.abs(x)))


def _head_sums(t, ones_ref):
    hi = t.astype(bf16)
    lo = (t - hi.astype(f32)).astype(bf16)
    return (jnp.dot(hi, ones_ref[...], preferred_element_type=f32)
            + jnp.dot(lo, ones_ref[...], preferred_element_type=f32))


def _rwkv_proj_kernel(x_ref, xp_ref, mix_ref, wr_ref, wk_ref, wv_ref, w1_ref, w2_ref, a1_ref, a2_ref,
                      g1_ref, g2_ref, vec_ref,
                      r_ref, w_ref, k_ref, v_ref, gate_ref, g_ref, *, tiles_per_seq):
    i = pl.program_id(0)
    x = x_ref[...]
    ts = x.shape[0]
    prev_last = jnp.where(i % tiles_per_seq == 0, 0.0, xp_ref[SUBLANES - 1:SUBLANES, :])
    row = lax.broadcasted_iota(jnp.int32, x.shape, 0)
    xprev = jnp.where(row == 0, prev_last, pltpu.roll(x, 1, axis=0))
    xx = xprev - x

    def mixed(m):
        return (x + xx * mix_ref[m:m + 1, :]).astype(bf16)

    def mm(a, w_ref_):
        return jnp.dot(a, w_ref_[...], preferred_element_type=f32)

    w0, a0 = (vec_ref[n:n + 1, :] for n in range(2))
    lw = mm(jnp.tanh(mm(mixed(1), w1_ref)).astype(bf16), w2_ref)
    w_log = -_softplus(-(w0 + lw)) - 0.5
    r_ref[...] = mm(mixed(0), wr_ref).astype(bf16)
    w_ref[...] = jnp.exp(w_log).astype(bf16)
    k_ref[...] = mm(mixed(2), wk_ref).astype(bf16)
    v_ref[...] = mm(mixed(3), wv_ref).astype(bf16)
    gate_ref[...] = _sigmoid(a0 + mm(mm(mixed(4), a1_ref).astype(bf16), a2_ref)).astype(bf16)
    g_ref[...] = mm(_sigmoid(mm(mixed(5), g1_ref)).astype(bf16), g2_ref)


def rwkv_proj(x, seq, mix, w_r, w_k, w_v, w1, w2, a1, a2, g1, g2, vecs, *, ts):
    n, d = x.shape
    full = lambda arr: pl.BlockSpec(arr.shape, lambda i: (0,) * arr.ndim)
    tok = pl.BlockSpec((ts, d), lambda i: (i, 0))
    prev = pl.BlockSpec((SUBLANES, d), lambda i: (jnp.maximum(i * (ts // SUBLANES) - 1, 0), 0))
    weights = (mix, w_r, w_k, w_v, w1, w2, a1, a2, g1, g2, vecs)
    out = lambda dt: jax.ShapeDtypeStruct((n, d), dt)
    return pl.pallas_call(
        functools.partial(_rwkv_proj_kernel, tiles_per_seq=seq // ts),
        grid=(n // ts,),
        in_specs=[tok, prev] + [full(w) for w in weights],
        out_specs=[tok] * 6,
        out_shape=[out(bf16), out(bf16), out(bf16), out(bf16), out(bf16), out(f32)],
        compiler_params=pltpu.CompilerParams(
            dimension_semantics=("arbitrary",), vmem_limit_bytes=VMEM_LIMIT),
        name="rwkv_proj",
    )(x, x, *weights)


def _rwkv_scan_kernel(rin_ref, win_ref, k_ref, vin_ref, gate_ref, knext_ref, kk_ref, ka_ref, o_ref,
                      s_ref, sa_ref, a_ref, b_ref, km_ref, r_ref, v_ref, w_ref):
    nk = s_ref.shape[0]
    tc = rin_ref.shape[0]
    r_ref[...] = rin_ref[...].astype(f32)
    v_ref[...] = vin_ref[...].astype(f32)
    w_ref[...] = jnp.exp(-win_ref[...].astype(f32))

    @pl.when(pl.program_id(0) == 0)
    def _():
        s_ref[...] = jnp.zeros_like(s_ref)
        sa_ref[...] = jnp.zeros_like(sa_ref)

    def unit_key(k):
        kk = k * kk_ref[...]
        norm = jnp.sqrt(jnp.sum(kk * kk, axis=-2, keepdims=True))
        return kk / jnp.maximum(norm, RWKV_L2_EPS)

    k_blk = k_ref[...].astype(f32)
    gate = gate_ref[...].astype(f32)
    kk_blk = unit_key(k_blk)
    a_ref[0:tc] = -kk_blk
    a_ref[tc:tc + 1] = -unit_key(knext_ref[...].astype(f32))
    b_ref[...] = kk_blk * gate
    km_ref[...] = k_blk * (1.0 + (gate - 1.0) * ka_ref[...])

    def step(t, sa):
        vt = v_ref[t]
        o = [jnp.zeros(s_ref.shape[1:], f32)] * 2
        nsa = [jnp.zeros(s_ref.shape[1:], f32)] * 2
        for kk in range(nk):
            new = (s_ref[kk] * w_ref[t, kk:kk + 1, :]
                   + (sa * b_ref[t, kk:kk + 1, :] + vt * km_ref[t, kk:kk + 1, :]))
            s_ref[kk] = new
            o[kk % 2] = o[kk % 2] + new * r_ref[t, kk:kk + 1, :]
            nsa[kk % 2] = nsa[kk % 2] + new * a_ref[t + 1, kk:kk + 1, :]
        ot = o[0] + o[1]
        mu = jnp.mean(ot, axis=0, keepdims=True)
        oc = ot - mu
        var = jnp.mean(oc * oc, axis=0, keepdims=True)
        o_ref[t] = (oc * lax.rsqrt(var + RWKV_GN_EPS)).astype(o_ref.dtype)
        return nsa[0] + nsa[1]

    sa_ref[...] = lax.fori_loop(0, tc, step, sa_ref[...])


def rwkv_scan(r, w, k, v, gate, kk_vec, ka_vec, *, tc):
    s, hd, chains = r.shape
    blk = pl.BlockSpec((tc, hd, chains), lambda i: (i, 0, 0))
    nxt = pl.BlockSpec((1, hd, chains), lambda i: (jnp.minimum((i + 1) * tc, s - 1), 0, 0))
    vec = pl.BlockSpec((hd, chains), lambda i: (0, 0))
    return pl.pallas_call(
        _rwkv_scan_kernel,
        grid=(s // tc,),
        in_specs=[blk] * 5 + [nxt, vec, vec],
        out_specs=blk,
        out_shape=jax.ShapeDtypeStruct((s, hd, chains), bf16),
        scratch_shapes=[pltpu.VMEM((hd, hd, chains), f32), pltpu.VMEM((hd, chains), f32),
                        pltpu.VMEM((tc + 1, hd, chains), f32)] + [pltpu.VMEM((tc, hd, chains), f32)] * 5,
        compiler_params=pltpu.CompilerParams(
            dimension_semantics=("arbitrary",), vmem_limit_bytes=VMEM_LIMIT),
        name="rwkv_scan",
    )(r, w, k, v, gate, k, kk_vec, ka_vec)


def _rwkv_out_kernel(x_ref, o_ref, r_ref, k_ref, gate_ref, v_ref, g_ref, vec_ref, ones_ref, wo_ref,
                     lng_ref, lnb_ref, out_ref, outb_ref):
    lnx_g, lnx_b, r_k, k_a = (vec_ref[n:n + 1, :] for n in range(4))
    widen = lambda ref: ref[...].astype(f32)
    k_mod = widen(k_ref) * (1.0 + (widen(gate_ref) - 1.0) * k_a)
    bonus = _head_sums(widen(r_ref) * k_mod * r_k, ones_ref) * widen(v_ref)
    y = (widen(o_ref) * lnx_g + lnx_b + bonus) * g_ref[...]
    m = jnp.dot(y.astype(bf16), wo_ref[...], preferred_element_type=f32)
    out = _layer_norm_rows(ALPHA * x_ref[...] + m, lng_ref[...], lnb_ref[...])
    out_ref[...] = out
    outb_ref[...] = out.astype(bf16)


def rwkv_out(x, o, r, k, gate, v, g, vecs, ones, w_o, ln_g, ln_b, *, ts):
    n, d = x.shape
    full = lambda arr: pl.BlockSpec(arr.shape, lambda i: (0,) * arr.ndim)
    tok = pl.BlockSpec((ts, d), lambda i: (i, 0))
    consts = (vecs, ones, w_o, ln_g.reshape(1, d), ln_b.reshape(1, d))
    return pl.pallas_call(
        _rwkv_out_kernel,
        grid=(n // ts,),
        in_specs=[tok] * 7 + [full(c) for c in consts],
        out_specs=[tok, tok],
        out_shape=[jax.ShapeDtypeStruct((n, d), f32), jax.ShapeDtypeStruct((n, d), bf16)],
        compiler_params=pltpu.CompilerParams(
            dimension_semantics=("arbitrary",), vmem_limit_bytes=VMEM_LIMIT),
        name="rwkv_out",
    )(x, o, r, k, gate, v, g, *consts)


def _head_ones(d, head):
    seg = jnp.arange(d) // head
    return (seg[:, None] == seg[None, :]).astype(bf16)


def rwkv_layer(x, seq, mix, w_r, w_k, w_v, w0, w1, w2, a0, a1, a2, g1, g2, k_k, k_a, r_k, lnx_g, lnx_b, w_o,
               ln_g, ln_b, *, ts=256, tc=16):
    n, d = x.shape
    bsz = n // seq
    nh, hd = RWKV_HEADS, RWKV_HEAD
    ones = _head_ones(d, hd)
    c = lambda w: w.astype(bf16)
    r, w, k, v, gate, g = rwkv_proj(x, seq, mix, c(w_r), c(w_k), c(w_v), c(w1), c(w2), c(a1), c(a2),
                                    c(g1), c(g2), jnp.stack([w0, a0]), ts=ts)
    to_chains = lambda t: t.reshape(bsz, seq, nh, hd).transpose(1, 3, 0, 2).reshape(seq, hd, bsz * nh)
    vec_chains = lambda t: jnp.tile(t.reshape(nh, hd).T, (1, bsz))
    o = rwkv_scan(*(to_chains(t) for t in (r, w, k, v, gate)), vec_chains(k_k), vec_chains(k_a), tc=tc)
    o = o.reshape(seq, hd, bsz, nh).transpose(2, 0, 3, 1).reshape(n, d)
    vecs_out = jnp.stack([lnx_g, lnx_b, r_k.reshape(d), k_a])
    return rwkv_out(x, o, r, k, gate, v, g, vecs_out, ones, c(w_o), ln_g, ln_b, ts=ts)


def kernel(x, rg_w_in, rg_conv_w, rg_conv_b, rg_w_a, rg_b_a, rg_w_x, rg_b_x, rg_lambda, rg_w_out, rw_mix, rw_w_r, rw_w_k, rw_w_v, rw_w0, rw_w1, rw_w2, rw_a0, rw_a1, rw_a2, rw_g1, rw_g2, rw_k_k, rw_k_a, rw_r_k, rw_lnx_g, rw_lnx_b, rw_w_o, peer_w_q, peer_sub_keys, peer_u, peer_v, ln_g, ln_b):
    bsz, s, d = x.shape
    x = x.reshape(bsz * s, d)
    for i in range(DEPTH):
        j = i // 2
        if i % 2 == 0:
            x1, x1b = rglru_layer(x, s, rg_w_in[j], rg_conv_w[j], rg_conv_b[j], rg_w_a[j], rg_b_a[j],
                                  rg_w_x[j], rg_b_x[j], rg_lambda[j], rg_w_out[j], ln_g[i, 0], ln_b[i, 0])
        else:
            x1, x1b = rwkv_layer(x, s, rw_mix[j], rw_w_r[j], rw_w_k[j], rw_w_v[j], rw_w0[j], rw_w1[j],
                                 rw_w2[j], rw_a0[j], rw_a1[j], rw_a2[j], rw_g1[j], rw_g2[j],
                                 rw_k_k[j], rw_k_a[j], rw_r_k[j], rw_lnx_g[j], rw_lnx_b[j], rw_w_o[j],
                                 ln_g[i, 0], ln_b[i, 0])
        x = peer_layer(x1, x1b, peer_w_q[i], peer_sub_keys[i], peer_u[i], peer_v[i], ln_g[i, 1], ln_b[i, 1])
    return x.reshape(bsz, s, d)
```

```python
import functools
import math
import jax, jax.numpy as jnp
from jax import lax
from jax.experimental import pallas as pl
from jax.experimental.pallas import tpu as pltpu

DEPTH = 2
RG_WIDTH = 1408
RG_BLOCK = 88
RG_CONV = 4
RG_C = 8.0
RWKV_HEAD = 64
RWKV_HEADS = 16
RWKV_GN_EPS = 64e-5
RWKV_L2_EPS = 1e-12
PEER_HEADS = 8
PEER_NKEYS = 128
PEER_DHALF = 128
PEER_TOPK = 16
ALPHA = (2 * DEPTH) ** 0.25
LN_EPS = 1e-5

LANES = 128
SUBLANES = 8
MXU_COLS = 256
VMEM_LIMIT = 56 << 20

f32 = jnp.float32
bf16 = jnp.bfloat16
u32 = jnp.uint32

RG_BAND = 3 * LANES


def _rg_band_starts():
    n_tiles = RG_WIDTH // LANES
    starts = [LANES * min(max(j - 1, 0), n_tiles - RG_BAND // LANES) for j in range(n_tiles)]
    for j, lo in enumerate(starts):
        first_head, last_head = (j * LANES) // RG_BLOCK, ((j + 1) * LANES - 1) // RG_BLOCK
        assert lo <= first_head * RG_BLOCK and (last_head + 1) * RG_BLOCK <= lo + RG_BAND
    return starts


def _gelu_tanh(x):
    return 0.5 * x * (1.0 + jnp.tanh(math.sqrt(2.0 / math.pi) * (x + 0.044715 * (x * x * x))))


def _layer_norm_rows(z, g, b):
    mu = jnp.mean(z, axis=-1, keepdims=True)
    zc = z - mu
    var = jnp.mean(zc * zc, axis=-1, keepdims=True)
    return zc * lax.rsqrt(var + LN_EPS) * g + b


def _top_rows(s, k, with_rank=False):
    rows = []
    rank = jnp.full(s.shape, float(k), f32)
    for j in range(k):
        m = jnp.max(s, axis=0, keepdims=True)
        rows.append(m)
        hit = s == m
        if with_rank:
            rank = jnp.where(hit, float(j), rank)
        s = jnp.where(hit, -jnp.inf, s)
    return (rows, rank) if with_rank else rows


def _peer_select_kernel(x_ref, wq_ref, keys_ref, cnt_ref, rk1_ref, p0_ref, p1_ref, q_ref):
    h = pl.program_id(1)

    @pl.when(h == 0)
    def _():
        q_ref[...] = jnp.dot(x_ref[...], wq_ref[...], preferred_element_type=f32)

    def scores(p):
        off = pl.multiple_of((2 * h + p) * PEER_DHALF, PEER_DHALF)
        qhp = q_ref[:, pl.ds(off, PEER_DHALF)].astype(bf16)
        return lax.dot_general(keys_ref[0, p], qhp, (((1,), (1,)), ((), ())),
                               preferred_element_type=f32)

    def select_all(stable):
        s0_all = scores(0)
        s1_all = scores(1)
        tied = 0.0
        for c in range(s0_all.shape[1] // LANES):
            lanes = slice(c * LANES, (c + 1) * LANES)
            cnt, rank1, p0, p1, tied_c = _select_slab(s0_all[:, lanes], s1_all[:, lanes], stable=stable)
            cnt_ref[0, c] = _bf16_twice(cnt)
            p0_ref[0, c] = _bf16_twice(p0)
            rk1_ref[0, c] = pltpu.bitcast(rank1.astype(bf16), u32)
            p1_ref[0, c] = pltpu.bitcast(p1.astype(bf16), u32)
            if not stable:
                tied = jnp.maximum(tied, jnp.max(tied_c))
        return tied

    tied = select_all(stable=False)

    @pl.when(tied > 0.0)
    def _():
        select_all(stable=True)


def _top_rows_stable(s, k):
    rows = []
    idx = lax.broadcasted_iota(jnp.int32, s.shape, 0)
    rank = jnp.full(s.shape, float(k), f32)
    for j in range(k):
        m = jnp.max(s, axis=0, keepdims=True)
        rows.append(m)
        first = jnp.min(jnp.where(s == m, idx, s.shape[0]), axis=0, keepdims=True)
        pick = idx == first
        rank = jnp.where(pick, float(j), rank)
        s = jnp.where(pick, -jnp.inf, s)
    return rows, rank


def _select_slab(s0, s1, *, stable):
    if stable:
        a, rank0 = _top_rows_stable(s0, PEER_TOPK)
        b, rank1 = _top_rows_stable(s1, PEER_TOPK)
    else:
        a = _top_rows(s0, PEER_TOPK)
        b, rank1 = _top_rows(s1, PEER_TOPK, with_rank=True)
    row = lax.broadcasted_iota(jnp.int32, (PEER_TOPK, s0.shape[1]), 0)
    bmat = jnp.zeros((PEER_TOPK, s0.shape[1]), f32)
    for j in range(PEER_TOPK):
        bmat = jnp.where(row == j, b[j], bmat)
    row8 = lax.broadcasted_iota(jnp.int32, (SUBLANES, s0.shape[1]), 0)
    ahi = jnp.zeros((SUBLANES, s0.shape[1]), f32)
    for n in range(SUBLANES):
        ahi = jnp.where(row8 == n, a[SUBLANES + n], ahi)
    cand = jnp.concatenate([a[0] + bmat] + [a[i] + bmat[:SUBLANES] for i in range(1, SUBLANES)]
                           + [ahi + b[0]], axis=0)
    rows_of = [PEER_TOPK] + [SUBLANES] * (SUBLANES - 1) + [1] * SUBLANES
    if stable:
        picked = _top_rows_stable(cand, PEER_TOPK)[1] < float(PEER_TOPK)
    else:
        picked = cand >= _top_rows(cand, PEER_TOPK)[-1]
    picked = jnp.where(picked, 1.0, 0.0)
    cmax = a[0] + b[0]
    z = jnp.sum(picked * jnp.exp(cand - cmax), axis=0, keepdims=True)
    cnt = jnp.zeros_like(s0)
    total = jnp.zeros_like(z)
    lo = 0
    for i in range(PEER_TOPK):
        hi = lo + rows_of[i]
        cnt_i = jnp.sum(picked[lo:hi], axis=0, keepdims=True)
        lo = hi
        total = total + cnt_i
        cnt = jnp.where((rank0 == float(i)) if stable else (s0 == a[i]), cnt_i, cnt)
    p0 = jnp.exp(s0 - a[0]) / z
    p1 = jnp.exp(s1 - b[0])
    if stable:
        return cnt, rank1, p0, p1, None
    k = float(PEER_TOPK)
    n0 = jnp.sum(jnp.where(s0 >= a[-1], 1.0, 0.0), axis=0, keepdims=True)
    n1 = jnp.sum(jnp.where(rank1 < k, 1.0, 0.0), axis=0, keepdims=True)
    tied = jnp.abs(n0 - k) + jnp.abs(n1 - k) + jnp.abs(total - k)
    return cnt, rank1, p0, p1, tied


def _bf16_twice(t):
    bits = pltpu.bitcast(t.astype(bf16).astype(f32), u32) >> 16
    return bits | (bits << 16)


def peer_select(x, wq_b, keys_b, *, tq):
    n, d = x.shape
    nh = PEER_HEADS
    full_spec = pl.BlockSpec((1, tq // LANES, PEER_NKEYS, LANES), lambda i, h: (h, i, 0, 0))
    half_spec = pl.BlockSpec((1, tq // LANES, PEER_NKEYS // 2, LANES), lambda i, h: (h, i, 0, 0))
    full = jax.ShapeDtypeStruct((nh, n // LANES, PEER_NKEYS, LANES), u32)
    half = jax.ShapeDtypeStruct((nh, n // LANES, PEER_NKEYS // 2, LANES), u32)
    return pl.pallas_call(
        _peer_select_kernel,
        grid=(n // tq, nh),
        in_specs=[pl.BlockSpec((tq, d), lambda i, h: (i, 0)),
                  pl.BlockSpec((d, 2 * nh * PEER_DHALF), lambda i, h: (0, 0)),
                  pl.BlockSpec((1, 2, PEER_NKEYS, PEER_DHALF), lambda i, h: (h, 0, 0, 0))],
        out_specs=[full_spec, half_spec, full_spec, half_spec],
        out_shape=[full, half, full, half],
        scratch_shapes=[pltpu.VMEM((tq, 2 * nh * PEER_DHALF), f32)],
        compiler_params=pltpu.CompilerParams(
            dimension_semantics=("arbitrary", "arbitrary"), vmem_limit_bytes=VMEM_LIMIT),
        name="peer_select",
    )(x, wq_b, keys_b)


def _peer_main_kernel(xb_ref, x_ref, cnt_ref, rk1_ref, p0_ref, p1_ref, u_ref, vt_ref,
                      g_ref, b_ref, o_ref, acc_ref, *slab_refs, te, tt):
    j = pl.program_id(1)
    n_i0 = te // PEER_NKEYS
    n_slab = tt // LANES
    per_chunk = MXU_COLS // LANES
    at_refs, ga_refs = slab_refs[:n_slab], slab_refs[n_slab:]
    half = PEER_NKEYS // 2
    unpack = lambda words: pltpu.bitcast(words, bf16)

    @pl.when(j == 0)
    def _():
        acc_ref[...] = jnp.zeros_like(acc_ref)

    def act_chunk(c):
        at = lax.dot_general(u_ref[...], xb_ref[c * MXU_COLS:(c + 1) * MXU_COLS, :],
                             (((1,), (1,)), ((), ())), preferred_element_type=f32)
        for k in range(per_chunk):
            at_refs[c * per_chunk + k][...] = at[:, k * LANES:(k + 1) * LANES]

    def gate_slab(s):
        for il in range(n_i0):
            g = jnp.zeros((PEER_NKEYS, LANES), bf16)
            for h in range(PEER_HEADS):
                cntrow = jnp.broadcast_to(cnt_ref[h, s, il:il + 1, :], (half, LANES))
                p0row = jnp.broadcast_to(p0_ref[h, s, il:il + 1, :], (half, LANES))
                sel = unpack(rk1_ref[h, s]) < unpack(cntrow)
                g = g + jnp.where(sel, unpack(p1_ref[h, s]) * unpack(p0row), jnp.zeros((), bf16))
            ga = _gelu_tanh(at_refs[s][il * PEER_NKEYS:(il + 1) * PEER_NKEYS, :].astype(bf16)) * g
            ga_refs[s][il * half:(il + 1) * half, :] = pltpu.bitcast(ga, u32)

    def out_chunk(c):
        ga = jnp.concatenate([unpack(ga_refs[c * per_chunk + k][...]) for k in range(per_chunk)], axis=1)
        lanes = slice(c * MXU_COLS, (c + 1) * MXU_COLS)
        acc_ref[:, lanes] += jnp.dot(vt_ref[...], ga, preferred_element_type=f32)

    n_chunk = tt // MXU_COLS
    act_chunk(0)
    for c in range(n_chunk):
        gate_slab(c * per_chunk)
        if c + 1 < n_chunk:
            act_chunk(c + 1)
        for k in range(1, per_chunk):
            gate_slab(c * per_chunk + k)
        out_chunk(c)

    @pl.when(j == pl.num_programs(1) - 1)
    def _():
        z = ALPHA * x_ref[...] + acc_ref[...].T
        o_ref[...] = _layer_norm_rows(z, g_ref[...], b_ref[...])


def peer_main(x, xb, stats, u_b, vt_b, ln_g, ln_b, *, tt, te):
    n, d = x.shape
    e = u_b.shape[0]
    nh = PEER_HEADS
    cnt, rk1, p0, p1 = stats
    full_spec = pl.BlockSpec((nh, tt // LANES, te // PEER_NKEYS, LANES), lambda i, j: (0, i, j, 0))
    half_spec = pl.BlockSpec((nh, tt // LANES, PEER_NKEYS // 2, LANES), lambda i, j: (0, i, 0, 0))
    return pl.pallas_call(
        functools.partial(_peer_main_kernel, te=te, tt=tt),
        grid=(n // tt, e // te),
        in_specs=[pl.BlockSpec((tt, d), lambda i, j: (i, 0)),
                  pl.BlockSpec((tt, d), lambda i, j: (i, 0)),
                  full_spec, half_spec, full_spec, half_spec,
                  pl.BlockSpec((te, d), lambda i, j: (j, 0)),
                  pl.BlockSpec((d, te), lambda i, j: (0, j)),
                  pl.BlockSpec((1, d), lambda i, j: (0, 0)),
                  pl.BlockSpec((1, d), lambda i, j: (0, 0))],
        out_specs=pl.BlockSpec((tt, d), lambda i, j: (i, 0)),
        out_shape=jax.ShapeDtypeStruct((n, d), f32),
        scratch_shapes=([pltpu.VMEM((d, tt), f32)]
                        + [pltpu.VMEM((te, LANES), f32)] * (tt // LANES)
                        + [pltpu.VMEM((te // 2, LANES), u32)] * (tt // LANES)),
        compiler_params=pltpu.CompilerParams(
            dimension_semantics=("arbitrary", "arbitrary"), vmem_limit_bytes=VMEM_LIMIT),
        name="peer_main",
    )(xb, x, cnt, rk1, p0, p1, u_b, vt_b, ln_g.reshape(1, d), ln_b.reshape(1, d))


def peer_layer(x, xb, w_q, sub_keys, u, v, ln_g, ln_b, *, tq=256, tt=1024, te=1024):
    stats = peer_select(xb, w_q.astype(bf16), sub_keys.astype(bf16), tq=tq)
    return peer_main(x, xb, stats, u.astype(bf16), v.T.astype(bf16), ln_g, ln_b, tt=tt, te=te)


def _rglru_kernel(x_ref, win_ref, cw_ref, vec_ref, wg_ref, wout_ref, lng_ref, lnb_ref, o_ref, ob_ref,
                  tail_ref, h_ref, *, tiles_per_seq):
    i = pl.program_id(0)
    w = RG_WIDTH
    x = x_ref[...]
    ts = x.shape[0]

    @pl.when(i % tiles_per_seq == 0)
    def _():
        tail_ref[...] = jnp.zeros_like(tail_ref)
        h_ref[...] = jnp.zeros_like(h_ref)

    conv_b, b_a, b_x, lam = (vec_ref[n:n + 1, :] for n in range(4))
    hin = jnp.dot(x.astype(bf16), win_ref[...], preferred_element_type=f32)
    gate_branch = _gelu_tanh(hin[:, :w])
    hx = hin[:, w:]

    row = lax.broadcasted_iota(jnp.int32, (ts, w), 0)
    row8 = lax.broadcasted_iota(jnp.int32, (SUBLANES, w), 0)
    tail = tail_ref[...]
    xc = hx * cw_ref[RG_CONV - 1:RG_CONV, :] + conv_b
    for s in range(1, RG_CONV):
        rolled = pltpu.roll(hx, s, axis=0)
        head = jnp.where(row8 < s, pltpu.roll(tail, s, axis=0), rolled[:SUBLANES])
        shifted = jnp.concatenate([head, rolled[SUBLANES:]], axis=0)
        xc = xc + shifted * cw_ref[RG_CONV - 1 - s:RG_CONV - s, :]
    tail_ref[...] = hx[ts - SUBLANES:]

    xcb = xc.astype(bf16)
    bands = [jnp.dot(xcb[:, lo:lo + RG_BAND], wg_ref[j], preferred_element_type=f32)
             for j, lo in enumerate(_rg_band_starts())]
    r = _sigmoid(jnp.concatenate([g[:, :LANES] for g in bands], axis=1) + b_a)
    ig = _sigmoid(jnp.concatenate([g[:, LANES:] for g in bands], axis=1) + b_x)
    log_a = (-RG_C * _softplus(-lam)) * r
    a = jnp.exp(log_a)
    u = jnp.sqrt(-jnp.tanh(log_a) * (a * a + 1.0)) * (ig * xc)

    sub = row & (SUBLANES - 1)
    s = 1
    while s < SUBLANES:
        keep = sub >= s
        a_sh = jnp.where(keep, pltpu.roll(a, s, axis=0), 1.0)
        u_sh = jnp.where(keep, pltpu.roll(u, s, axis=0), 0.0)
        u = a * u_sh + u
        a = a * a_sh
        s *= 2
    carry = h_ref[SUBLANES - 1:SUBLANES, :]
    groups = []
    for g in range(0, ts, SUBLANES):
        hg = u[g:g + SUBLANES] + a[g:g + SUBLANES] * carry
        groups.append(hg)
        carry = hg[SUBLANES - 1:]
    hs = jnp.concatenate(groups, axis=0)
    h_ref[...] = groups[-1]

    m = jnp.dot((hs * gate_branch).astype(bf16), wout_ref[...], preferred_element_type=f32)
    out = _layer_norm_rows(ALPHA * x + m, lng_ref[...], lnb_ref[...])
    o_ref[...] = out
    ob_ref[...] = out.astype(bf16)


def rglru_layer(x, seq, w_in, conv_w, conv_b, w_a, b_a, w_x, b_x, lam, w_out, ln_g, ln_b, *, ts=256):
    n, d = x.shape
    w = RG_WIDTH
    blockdiag = lambda t: jax.scipy.linalg.block_diag(*t)
    full_a, full_x = blockdiag(w_a), blockdiag(w_x)
    w_gates = jnp.stack([
        jnp.concatenate([m[lo:lo + RG_BAND, j * LANES:(j + 1) * LANES] for m in (full_a, full_x)], axis=1)
        for j, lo in enumerate(_rg_band_starts())]).astype(bf16)
    vecs = jnp.stack([conv_b, b_a.reshape(w), b_x.reshape(w), lam.reshape(w)])
    consts = (w_in.astype(bf16), conv_w, vecs, w_gates, w_out.astype(bf16), ln_g.reshape(1, d), ln_b.reshape(1, d))
    full = lambda arr: pl.BlockSpec(arr.shape, lambda i: (0,) * arr.ndim, pipeline_mode=pl.Buffered(1))
    tok = pl.BlockSpec((ts, d), lambda i: (i, 0))
    return pl.pallas_call(
        functools.partial(_rglru_kernel, tiles_per_seq=seq // ts),
        grid=(n // ts,),
        in_specs=[tok] + [full(c) for c in consts],
        out_specs=[tok, tok],
        out_shape=[jax.ShapeDtypeStruct((n, d), f32), jax.ShapeDtypeStruct((n, d), bf16)],
        scratch_shapes=[pltpu.VMEM((SUBLANES, w), f32), pltpu.VMEM((SUBLANES, w), f32)],
        compiler_params=pltpu.CompilerParams(
            dimension_semantics=("arbitrary",), vmem_limit_bytes=VMEM_LIMIT),
        name="rglru",
    )(x, *consts)


def _sigmoid(x):
    return 1.0 / (1.0 + jnp.exp(-x))


def _softplus(x):
    return jnp.maximum(x, 0.0) + jnp.log1p(jnp.exp(-jnp.abs(x)))


def _head_sums(t, ones_ref):
    hi = t.astype(bf16)
    lo = (t - hi.astype(f32)).astype(bf16)
    return (jnp.dot(hi, ones_ref[...], preferred_element_type=f32)
            + jnp.dot(lo, ones_ref[...], preferred_element_type=f32))


def _rwkv_proj_kernel(x_ref, xp_ref, mix_ref, wr_ref, wk_ref, wv_ref, w1_ref, w2_ref, a1_ref, a2_ref,
                      g1_ref, g2_ref, vec_ref,
                      r_ref, w_ref, k_ref, v_ref, gate_ref, g_ref, *, tiles_per_seq):
    i = pl.program_id(0)
    x = x_ref[...]
    ts = x.shape[0]
    prev_last = jnp.where(i % tiles_per_seq == 0, 0.0, xp_ref[SUBLANES - 1:SUBLANES, :])
    row = lax.broadcasted_iota(jnp.int32, x.shape, 0)
    xprev = jnp.where(row == 0, prev_last, pltpu.roll(x, 1, axis=0))
    xx = xprev - x

    def mixed(m):
        return (x + xx * mix_ref[m:m + 1, :]).astype(bf16)

    def mm(a, w_ref_):
        return jnp.dot(a, w_ref_[...], preferred_element_type=f32)

    w0, a0 = (vec_ref[n:n + 1, :] for n in range(2))
    lw = mm(jnp.tanh(mm(mixed(1), w1_ref)).astype(bf16), w2_ref)
    w_log = -_softplus(-(w0 + lw)) - 0.5
    r_ref[...] = mm(mixed(0), wr_ref).astype(bf16)
    w_ref[...] = jnp.exp(w_log).astype(bf16)
    k_ref[...] = mm(mixed(2), wk_ref).astype(bf16)
    v_ref[...] = mm(mixed(3), wv_ref).astype(bf16)
    gate_ref[...] = _sigmoid(a0 + mm(mm(mixed(4), a1_ref).astype(bf16), a2_ref)).astype(bf16)
    g_ref[...] = mm(_sigmoid(mm(mixed(5), g1_ref)).astype(bf16), g2_ref)


def rwkv_proj(x, seq, mix, w_r, w_k, w_v, w1, w2, a1, a2, g1, g2, vecs, *, ts):
    n, d = x.shape
    full = lambda arr: pl.BlockSpec(arr.shape, lambda i: (0,) * arr.ndim)
    tok = pl.BlockSpec((ts, d), lambda i: (i, 0))
    prev = pl.BlockSpec((SUBLANES, d), lambda i: (jnp.maximum(i * (ts // SUBLANES) - 1, 0), 0))
    weights = (mix, w_r, w_k, w_v, w1, w2, a1, a2, g1, g2, vecs)
    out = lambda dt: jax.ShapeDtypeStruct((n, d), dt)
    return pl.pallas_call(
        functools.partial(_rwkv_proj_kernel, tiles_per_seq=seq // ts),
        grid=(n // ts,),
        in_specs=[tok, prev] + [full(w) for w in weights],
        out_specs=[tok] * 6,
        out_shape=[out(bf16), out(bf16), out(bf16), out(bf16), out(bf16), out(f32)],
        compiler_params=pltpu.CompilerParams(
            dimension_semantics=("arbitrary",), vmem_limit_bytes=VMEM_LIMIT),
        name="rwkv_proj",
    )(x, x, *weights)


def _rwkv_scan_kernel(rin_ref, win_ref, k_ref, vin_ref, gate_ref, knext_ref, kk_ref, ka_ref, o_ref,
                      s_ref, sa_ref, a_ref, b_ref, km_ref, r_ref, v_ref, w_ref):
    nk = s_ref.shape[0]
    tc = rin_ref.shape[0]
    r_ref[...] = rin_ref[...].astype(f32)
    v_ref[...] = vin_ref[...].astype(f32)
    w_ref[...] = jnp.exp(-win_ref[...].astype(f32))

    @pl.when(pl.program_id(0) == 0)
    def _():
        s_ref[...] = jnp.zeros_like(s_ref)
        sa_ref[...] = jnp.zeros_like(sa_ref)

    def unit_key(k):
        kk = k * kk_ref[...]
        norm = jnp.sqrt(jnp.sum(kk * kk, axis=-2, keepdims=True))
        return kk / jnp.maximum(norm, RWKV_L2_EPS)

    k_blk = k_ref[...].astype(f32)
    gate = gate_ref[...].astype(f32)
    kk_blk = unit_key(k_blk)
    a_ref[0:tc] = -kk_blk
    a_ref[tc:tc + 1] = -unit_key(knext_ref[...].astype(f32))
    b_ref[...] = kk_blk * gate
    km_ref[...] = k_blk * (1.0 + (gate - 1.0) * ka_ref[...])

    def step(t, sa):
        vt = v_ref[t]
        o = [jnp.zeros(s_ref.shape[1:], f32)] * 2
        nsa = [jnp.zeros(s_ref.shape[1:], f32)] * 2
        for kk in range(nk):
            new = (s_ref[kk] * w_ref[t, kk:kk + 1, :]
                   + (sa * b_ref[t, kk:kk + 1, :] + vt * km_ref[t, kk:kk + 1, :]))
            s_ref[kk] = new
            o[kk % 2] = o[kk % 2] + new * r_ref[t, kk:kk + 1, :]
            nsa[kk % 2] = nsa[kk % 2] + new * a_ref[t + 1, kk:kk + 1, :]
        ot = o[0] + o[1]
        mu = jnp.mean(ot, axis=0, keepdims=True)
        oc = ot - mu
        var = jnp.mean(oc * oc, axis=0, keepdims=True)
        o_ref[t] = (oc * lax.rsqrt(var + RWKV_GN_EPS)).astype(o_ref.dtype)
        return nsa[0] + nsa[1]

    sa_ref[...] = lax.fori_loop(0, tc, step, sa_ref[...])


def rwkv_scan(r, w, k, v, gate, kk_vec, ka_vec, *, tc):
    s, hd, chains = r.shape
    blk = pl.BlockSpec((tc, hd, chains), lambda i: (i, 0, 0))
    nxt = pl.BlockSpec((1, hd, chains), lambda i: (jnp.minimum((i + 1) * tc, s - 1), 0, 0))
    vec = pl.BlockSpec((hd, chains), lambda i: (0, 0))
    return pl.pallas_call(
        _rwkv_scan_kernel,
        grid=(s // tc,),
        in_specs=[blk] * 5 + [nxt, vec, vec],
        out_specs=blk,
        out_shape=jax.ShapeDtypeStruct((s, hd, chains), bf16),
        scratch_shapes=[pltpu.VMEM((hd, hd, chains), f32), pltpu.VMEM((hd, chains), f32),
                        pltpu.VMEM((tc + 1, hd, chains), f32)] + [pltpu.VMEM((tc, hd, chains), f32)] * 5,
        compiler_params=pltpu.CompilerParams(
            dimension_semantics=("arbitrary",), vmem_limit_bytes=VMEM_LIMIT),
        name="rwkv_scan",
    )(r, w, k, v, gate, k, kk_vec, ka_vec)


def _rwkv_out_kernel(x_ref, o_ref, r_ref, k_ref, gate_ref, v_ref, g_ref, vec_ref, ones_ref, wo_ref,
                     lng_ref, lnb_ref, out_ref, outb_ref):
    lnx_g, lnx_b, r_k, k_a = (vec_ref[n:n + 1, :] for n in range(4))
    widen = lambda ref: ref[...].astype(f32)
    k_mod = widen(k_ref) * (1.0 + (widen(gate_ref) - 1.0) * k_a)
    bonus = _head_sums(widen(r_ref) * k_mod * r_k, ones_ref) * widen(v_ref)
    y = (widen(o_ref) * lnx_g + lnx_b + bonus) * g_ref[...]
    m = jnp.dot(y.astype(bf16), wo_ref[...], preferred_element_type=f32)
    out = _layer_norm_rows(ALPHA * x_ref[...] + m, lng_ref[...], lnb_ref[...])
    out_ref[...] = out
    outb_ref[...] = out.astype(bf16)


def rwkv_out(x, o, r, k, gate, v, g, vecs, ones, w_o, ln_g, ln_b, *, ts):
    n, d = x.shape
    full = lambda arr: pl.BlockSpec(arr.shape, lambda i: (0,) * arr.ndim)
    tok = pl.BlockSpec((ts, d), lambda i: (i, 0))
    consts = (vecs, ones, w_o, ln_g.reshape(1, d), ln_b.reshape(1, d))
    return pl.pallas_call(
        _rwkv_out_kernel,
        grid=(n // ts,),
        in_specs=[tok] * 7 + [full(c) for c in consts],
        out_specs=[tok, tok],
        out_shape=[jax.ShapeDtypeStruct((n, d), f32), jax.ShapeDtypeStruct((n, d), bf16)],
        compiler_params=pltpu.CompilerParams(
            dimension_semantics=("arbitrary",), vmem_limit_bytes=VMEM_LIMIT),
        name="rwkv_out",
    )(x, o, r, k, gate, v, g, *consts)


def _head_ones(d, head):
    seg = jnp.arange(d) // head
    return (seg[:, None] == seg[None, :]).astype(bf16)


def rwkv_layer(x, seq, mix, w_r, w_k, w_v, w0, w1, w2, a0, a1, a2, g1, g2, k_k, k_a, r_k, lnx_g, lnx_b, w_o,
               ln_g, ln_b, *, ts=256, tc=32):
    n, d = x.shape
    bsz = n // seq
    nh, hd = RWKV_HEADS, RWKV_HEAD
    ones = _head_ones(d, hd)
    c = lambda w: w.astype(bf16)
    r, w, k, v, gate, g = rwkv_proj(x, seq, mix, c(w_r), c(w_k), c(w_v), c(w1), c(w2), c(a1), c(a2),
                                    c(g1), c(g2), jnp.stack([w0, a0]), ts=ts)
    to_chains = lambda t: t.reshape(bsz, seq, nh, hd).transpose(1, 3, 0, 2).reshape(seq, hd, bsz * nh)
    vec_chains = lambda t: jnp.tile(t.reshape(nh, hd).T, (1, bsz))
    o = rwkv_scan(*(to_chains(t) for t in (r, w, k, v, gate)), vec_chains(k_k), vec_chains(k_a), tc=tc)
    o = o.reshape(seq, hd, bsz, nh).transpose(2, 0, 3, 1).reshape(n, d)
    vecs_out = jnp.stack([lnx_g, lnx_b, r_k.reshape(d), k_a])
    return rwkv_out(x, o, r, k, gate, v, g, vecs_out, ones, c(w_o), ln_g, ln_b, ts=ts)


def kernel(x, rg_w_in, rg_conv_w, rg_conv_b, rg_w_a, rg_b_a, rg_w_x, rg_b_x, rg_lambda, rg_w_out, rw_mix, rw_w_r, rw_w_k, rw_w_v, rw_w0, rw_w1, rw_w2, rw_a0, rw_a1, rw_a2, rw_g1, rw_g2, rw_k_k, rw_k_a, rw_r_k, rw_lnx_g, rw_lnx_b, rw_w_o, peer_w_q, peer_sub_keys, peer_u, peer_v, ln_g, ln_b):
    bsz, s, d = x.shape
    x = x.reshape(bsz * s, d)
    for i in range(DEPTH):
        j = i // 2
        if i % 2 == 0:
            x1, x1b = rglru_layer(x, s, rg_w_in[j], rg_conv_w[j], rg_conv_b[j], rg_w_a[j], rg_b_a[j],
                                  rg_w_x[j], rg_b_x[j], rg_lambda[j], rg_w_out[j], ln_g[i, 0], ln_b[i, 0])
        else:
            x1, x1b = rwkv_layer(x, s, rw_mix[j], rw_w_r[j], rw_w_k[j], rw_w_v[j], rw_w0[j], rw_w1[j],
                                 rw_w2[j], rw_a0[j], rw_a1[j], rw_a2[j], rw_g1[j], rw_g2[j],
                                 rw_k_k[j], rw_k_a[j], rw_r_k[j], rw_lnx_g[j], rw_lnx_b[j], rw_w_o[j],
                                 ln_g[i, 0], ln_b[i, 0])
        x = peer_layer(x1, x1b, peer_w_q[i], peer_sub_keys[i], peer_u[i], peer_v[i], ln_g[i, 1], ln_b[i, 1])
    return x.reshape(bsz, s, d)
```

```python
import functools
import math
import jax, jax.numpy as jnp
from jax import lax
from jax.experimental import pallas as pl
from jax.experimental.pallas import tpu as pltpu

DEPTH = 2
RG_WIDTH = 1408
RG_BLOCK = 88
RG_CONV = 4
RG_C = 8.0
RWKV_HEAD = 64
RWKV_HEADS = 16
RWKV_GN_EPS = 64e-5
RWKV_L2_EPS = 1e-12
PEER_HEADS = 8
PEER_NKEYS = 128
PEER_DHALF = 128
PEER_TOPK = 16
ALPHA = (2 * DEPTH) ** 0.25
LN_EPS = 1e-5

LANES = 128
SUBLANES = 8
MXU_COLS = 256
VMEM_LIMIT = 56 << 20

f32 = jnp.float32
bf16 = jnp.bfloat16
u32 = jnp.uint32

RG_BAND = 3 * LANES


def _rg_band_starts():
    n_tiles = RG_WIDTH // LANES
    starts = [LANES * min(max(j - 1, 0), n_tiles - RG_BAND // LANES) for j in range(n_tiles)]
    for j, lo in enumerate(starts):
        first_head, last_head = (j * LANES) // RG_BLOCK, ((j + 1) * LANES - 1) // RG_BLOCK
        assert lo <= first_head * RG_BLOCK and (last_head + 1) * RG_BLOCK <= lo + RG_BAND
    return starts


def _gelu_tanh(x):
    return 0.5 * x * (1.0 + jnp.tanh(math.sqrt(2.0 / math.pi) * (x + 0.044715 * (x * x * x))))


def _layer_norm_rows(z, g, b):
    mu = jnp.mean(z, axis=-1, keepdims=True)
    zc = z - mu
    var = jnp.mean(zc * zc, axis=-1, keepdims=True)
    return zc * lax.rsqrt(var + LN_EPS) * g + b


def _top_rows(s, k, with_rank=False):
    rows = []
    rank = jnp.full(s.shape, float(k), f32)
    for j in range(k):
        m = jnp.max(s, axis=0, keepdims=True)
        rows.append(m)
        hit = s == m
        if with_rank:
            rank = jnp.where(hit, float(j), rank)
        s = jnp.where(hit, -jnp.inf, s)
    return (rows, rank) if with_rank else rows


def _peer_select_kernel(x_ref, wq_ref, keys_ref, cnt_ref, rk1_ref, p0_ref, p1_ref, q_ref):
    h = pl.program_id(1)

    @pl.when(h == 0)
    def _():
        q_ref[...] = jnp.dot(x_ref[...], wq_ref[...], preferred_element_type=f32)

    def scores(p):
        off = pl.multiple_of((2 * h + p) * PEER_DHALF, PEER_DHALF)
        qhp = q_ref[:, pl.ds(off, PEER_DHALF)].astype(bf16)
        return lax.dot_general(keys_ref[0, p], qhp, (((1,), (1,)), ((), ())),
                               preferred_element_type=f32)

    def select_all(stable):
        s0_all = scores(0)
        s1_all = scores(1)
        tied = 0.0
        for c in range(s0_all.shape[1] // LANES):
            lanes = slice(c * LANES, (c + 1) * LANES)
            cnt, rank1, p0, p1, tied_c = _select_slab(s0_all[:, lanes], s1_all[:, lanes], stable=stable)
            cnt_ref[0, c] = _bf16_twice(cnt)
            p0_ref[0, c] = _bf16_twice(p0)
            rk1_ref[0, c] = pltpu.bitcast(rank1.astype(bf16), u32)
            p1_ref[0, c] = pltpu.bitcast(p1.astype(bf16), u32)
            if not stable:
                tied = jnp.maximum(tied, jnp.max(tied_c))
        return tied

    tied = select_all(stable=False)

    @pl.when(tied > 0.0)
    def _():
        select_all(stable=True)


def _top_rows_stable(s, k):
    rows = []
    idx = lax.broadcasted_iota(jnp.int32, s.shape, 0)
    rank = jnp.full(s.shape, float(k), f32)
    for j in range(k):
        m = jnp.max(s, axis=0, keepdims=True)
        rows.append(m)
        first = jnp.min(jnp.where(s == m, idx, s.shape[0]), axis=0, keepdims=True)
        pick = idx == first
        rank = jnp.where(pick, float(j), rank)
        s = jnp.where(pick, -jnp.inf, s)
    return rows, rank


def _select_slab(s0, s1, *, stable):
    if stable:
        a, rank0 = _top_rows_stable(s0, PEER_TOPK)
        b, rank1 = _top_rows_stable(s1, PEER_TOPK)
    else:
        a = _top_rows(s0, PEER_TOPK)
        b, rank1 = _top_rows(s1, PEER_TOPK, with_rank=True)
    row = lax.broadcasted_iota(jnp.int32, (PEER_TOPK, s0.shape[1]), 0)
    bmat = jnp.zeros((PEER_TOPK, s0.shape[1]), f32)
    for j in range(PEER_TOPK):
        bmat = jnp.where(row == j, b[j], bmat)
    row8 = lax.broadcasted_iota(jnp.int32, (SUBLANES, s0.shape[1]), 0)
    ahi = jnp.zeros((SUBLANES, s0.shape[1]), f32)
    for n in range(SUBLANES):
        ahi = jnp.where(row8 == n, a[SUBLANES + n], ahi)
    cand = jnp.concatenate([a[0] + bmat] + [a[i] + bmat[:SUBLANES] for i in range(1, SUBLANES)]
                           + [ahi + b[0]], axis=0)
    rows_of = [PEER_TOPK] + [SUBLANES] * (SUBLANES - 1) + [1] * SUBLANES
    if stable:
        picked = _top_rows_stable(cand, PEER_TOPK)[1] < float(PEER_TOPK)
    else:
        picked = cand >= _top_rows(cand, PEER_TOPK)[-1]
    picked = jnp.where(picked, 1.0, 0.0)
    cmax = a[0] + b[0]
    z = jnp.sum(picked * jnp.exp(cand - cmax), axis=0, keepdims=True)
    cnt = jnp.zeros_like(s0)
    total = jnp.zeros_like(z)
    lo = 0
    for i in range(PEER_TOPK):
        hi = lo + rows_of[i]
        cnt_i = jnp.sum(picked[lo:hi], axis=0, keepdims=True)
        lo = hi
        total = total + cnt_i
        cnt = jnp.where((rank0 == float(i)) if stable else (s0 == a[i]), cnt_i, cnt)
    p0 = jnp.exp(s0 - a[0]) / z
    p1 = jnp.exp(s1 - b[0])
    if stable:
        return cnt, rank1, p0, p1, None
    k = float(PEER_TOPK)
    n0 = jnp.sum(jnp.where(s0 >= a[-1], 1.0, 0.0), axis=0, keepdims=True)
    n1 = jnp.sum(jnp.where(rank1 < k, 1.0, 0.0), axis=0, keepdims=True)
    tied = jnp.abs(n0 - k) + jnp.abs(n1 - k) + jnp.abs(total - k)
    return cnt, rank1, p0, p1, tied


def _bf16_twice(t):
    bits = pltpu.bitcast(t.astype(bf16).astype(f32), u32) >> 16
    return bits | (bits << 16)


def peer_select(x, wq_b, keys_b, *, tq):
    n, d = x.shape
    nh = PEER_HEADS
    full_spec = pl.BlockSpec((1, tq // LANES, PEER_NKEYS, LANES), lambda i, h: (h, i, 0, 0))
    half_spec = pl.BlockSpec((1, tq // LANES, PEER_NKEYS // 2, LANES), lambda i, h: (h, i, 0, 0))
    full = jax.ShapeDtypeStruct((nh, n // LANES, PEER_NKEYS, LANES), u32)
    half = jax.ShapeDtypeStruct((nh, n // LANES, PEER_NKEYS // 2, LANES), u32)
    return pl.pallas_call(
        _peer_select_kernel,
        grid=(n // tq, nh),
        in_specs=[pl.BlockSpec((tq, d), lambda i, h: (i, 0)),
                  pl.BlockSpec((d, 2 * nh * PEER_DHALF), lambda i, h: (0, 0)),
                  pl.BlockSpec((1, 2, PEER_NKEYS, PEER_DHALF), lambda i, h: (h, 0, 0, 0))],
        out_specs=[full_spec, half_spec, full_spec, half_spec],
        out_shape=[full, half, full, half],
        scratch_shapes=[pltpu.VMEM((tq, 2 * nh * PEER_DHALF), f32)],
        compiler_params=pltpu.CompilerParams(
            dimension_semantics=("arbitrary", "arbitrary"), vmem_limit_bytes=VMEM_LIMIT),
        name="peer_select",
    )(x, wq_b, keys_b)


def _peer_main_kernel(xb_ref, x_ref, cnt_ref, rk1_ref, p0_ref, p1_ref, u_ref, vt_ref,
                      g_ref, b_ref, o_ref, acc_ref, *slab_refs, te, tt):
    j = pl.program_id(1)
    n_i0 = te // PEER_NKEYS
    n_slab = tt // LANES
    per_chunk = MXU_COLS // LANES
    at_refs, ga_refs = slab_refs[:n_slab], slab_refs[n_slab:]
    half = PEER_NKEYS // 2
    unpack = lambda words: pltpu.bitcast(words, bf16)

    @pl.when(j == 0)
    def _():
        acc_ref[...] = jnp.zeros_like(acc_ref)

    def act_chunk(c):
        at = lax.dot_general(u_ref[...], xb_ref[c * MXU_COLS:(c + 1) * MXU_COLS, :],
                             (((1,), (1,)), ((), ())), preferred_element_type=f32)
        for k in range(per_chunk):
            at_refs[c * per_chunk + k][...] = at[:, k * LANES:(k + 1) * LANES]

    def gate_slab(s):
        for il in range(n_i0):
            g = jnp.zeros((PEER_NKEYS, LANES), bf16)
            for h in range(PEER_HEADS):
                cntrow = jnp.broadcast_to(cnt_ref[h, s, il:il + 1, :], (half, LANES))
                p0row = jnp.broadcast_to(p0_ref[h, s, il:il + 1, :], (half, LANES))
                sel = unpack(rk1_ref[h, s]) < unpack(cntrow)
                g = g + jnp.where(sel, unpack(p1_ref[h, s]) * unpack(p0row), jnp.zeros((), bf16))
            ga = _gelu_tanh(at_refs[s][il * PEER_NKEYS:(il + 1) * PEER_NKEYS, :].astype(bf16)) * g
            ga_refs[s][il * half:(il + 1) * half, :] = pltpu.bitcast(ga, u32)

    def out_chunk(c):
        ga = jnp.concatenate([unpack(ga_refs[c * per_chunk + k][...]) for k in range(per_chunk)], axis=1)
        lanes = slice(c * MXU_COLS, (c + 1) * MXU_COLS)
        acc_ref[:, lanes] += jnp.dot(vt_ref[...], ga, preferred_element_type=f32)

    n_chunk = tt // MXU_COLS
    act_chunk(0)
    for c in range(n_chunk):
        gate_slab(c * per_chunk)
        if c + 1 < n_chunk:
            act_chunk(c + 1)
        for k in range(1, per_chunk):
            gate_slab(c * per_chunk + k)
        out_chunk(c)

    @pl.when(j == pl.num_programs(1) - 1)
    def _():
        z = ALPHA * x_ref[...] + acc_ref[...].T
        o_ref[...] = _layer_norm_rows(z, g_ref[...], b_ref[...])


def peer_main(x, xb, stats, u_b, vt_b, ln_g, ln_b, *, tt, te):
    n, d = x.shape
    e = u_b.shape[0]
    nh = PEER_HEADS
    cnt, rk1, p0, p1 = stats
    full_spec = pl.BlockSpec((nh, tt // LANES, te // PEER_NKEYS, LANES), lambda i, j: (0, i, j, 0))
    half_spec = pl.BlockSpec((nh, tt // LANES, PEER_NKEYS // 2, LANES), lambda i, j: (0, i, 0, 0))
    return pl.pallas_call(
        functools.partial(_peer_main_kernel, te=te, tt=tt),
        grid=(n // tt, e // te),
        in_specs=[pl.BlockSpec((tt, d), lambda i, j: (i, 0)),
                  pl.BlockSpec((tt, d), lambda i, j: (i, 0)),
                  full_spec, half_spec, full_spec, half_spec,
                  pl.BlockSpec((te, d), lambda i, j: (j, 0)),
                  pl.BlockSpec((d, te), lambda i, j: (0, j)),
                  pl.BlockSpec((1, d), lambda i, j: (0, 0)),
                  pl.BlockSpec((1, d), lambda i, j: (0, 0))],
        out_specs=pl.BlockSpec((tt, d), lambda i, j: (i, 0)),
        out_shape=jax.ShapeDtypeStruct((n, d), f32),
        scratch_shapes=([pltpu.VMEM((d, tt), f32)]
                        + [pltpu.VMEM((te, LANES), f32)] * (tt // LANES)
                        + [pltpu.VMEM((te // 2, LANES), u32)] * (tt // LANES)),
        compiler_params=pltpu.CompilerParams(
            dimension_semantics=("arbitrary", "arbitrary"), vmem_limit_bytes=VMEM_LIMIT),
        name="peer_main",
    )(xb, x, cnt, rk1, p0, p1, u_b, vt_b, ln_g.reshape(1, d), ln_b.reshape(1, d))


def peer_layer(x, xb, w_q, sub_keys, u, v, ln_g, ln_b, *, tq=512, tt=1024, te=1024):
    stats = peer_select(xb, w_q.astype(bf16), sub_keys.astype(bf16), tq=tq)
    return peer_main(x, xb, stats, u.astype(bf16), v.T.astype(bf16), ln_g, ln_b, tt=tt, te=te)


def _rglru_kernel(x_ref, win_ref, cw_ref, vec_ref, wg_ref, wout_ref, lng_ref, lnb_ref, o_ref, ob_ref,
                  tail_ref, h_ref, *, tiles_per_seq):
    i = pl.program_id(0)
    w = RG_WIDTH
    x = x_ref[...]
    ts = x.shape[0]

    @pl.when(i % tiles_per_seq == 0)
    def _():
        tail_ref[...] = jnp.zeros_like(tail_ref)
        h_ref[...] = jnp.zeros_like(h_ref)

    conv_b, b_a, b_x, lam = (vec_ref[n:n + 1, :] for n in range(4))
    hin = jnp.dot(x.astype(bf16), win_ref[...], preferred_element_type=f32)
    gate_branch = _gelu_tanh(hin[:, :w])
    hx = hin[:, w:]

    row = lax.broadcasted_iota(jnp.int32, (ts, w), 0)
    row8 = lax.broadcasted_iota(jnp.int32, (SUBLANES, w), 0)
    tail = tail_ref[...]
    xc = hx * cw_ref[RG_CONV - 1:RG_CONV, :] + conv_b
    for s in range(1, RG_CONV):
        rolled = pltpu.roll(hx, s, axis=0)
        head = jnp.where(row8 < s, pltpu.roll(tail, s, axis=0), rolled[:SUBLANES])
        shifted = jnp.concatenate([head, rolled[SUBLANES:]], axis=0)
        xc = xc + shifted * cw_ref[RG_CONV - 1 - s:RG_CONV - s, :]
    tail_ref[...] = hx[ts - SUBLANES:]

    xcb = xc.astype(bf16)
    bands = [jnp.dot(xcb[:, lo:lo + RG_BAND], wg_ref[j], preferred_element_type=f32)
             for j, lo in enumerate(_rg_band_starts())]
    r = _sigmoid(jnp.concatenate([g[:, :LANES] for g in bands], axis=1) + b_a)
    ig = _sigmoid(jnp.concatenate([g[:, LANES:] for g in bands], axis=1) + b_x)
    log_a = (-RG_C * _softplus(-lam)) * r
    a = jnp.exp(log_a)
    u = jnp.sqrt(-jnp.tanh(log_a) * (a * a + 1.0)) * (ig * xc)

    sub = row & (SUBLANES - 1)
    s = 1
    while s < SUBLANES:
        keep = sub >= s
        a_sh = jnp.where(keep, pltpu.roll(a, s, axis=0), 1.0)
        u_sh = jnp.where(keep, pltpu.roll(u, s, axis=0), 0.0)
        u = a * u_sh + u
        a = a * a_sh
        s *= 2
    carry = h_ref[SUBLANES - 1:SUBLANES, :]
    groups = []
    for g in range(0, ts, SUBLANES):
        hg = u[g:g + SUBLANES] + a[g:g + SUBLANES] * carry
        groups.append(hg)
        carry = hg[SUBLANES - 1:]
    hs = jnp.concatenate(groups, axis=0)
    h_ref[...] = groups[-1]

    m = jnp.dot((hs * gate_branch).astype(bf16), wout_ref[...], preferred_element_type=f32)
    out = _layer_norm_rows(ALPHA * x + m, lng_ref[...], lnb_ref[...])
    o_ref[...] = out
    ob_ref[...] = out.astype(bf16)


def rglru_layer(x, seq, w_in, conv_w, conv_b, w_a, b_a, w_x, b_x, lam, w_out, ln_g, ln_b, *, ts=256):
    n, d = x.shape
    w = RG_WIDTH
    blockdiag = lambda t: jax.scipy.linalg.block_diag(*t)
    full_a, full_x = blockdiag(w_a), blockdiag(w_x)
    w_gates = jnp.stack([
        jnp.concatenate([m[lo:lo + RG_BAND, j * LANES:(j + 1) * LANES] for m in (full_a, full_x)], axis=1)
        for j, lo in enumerate(_rg_band_starts())]).astype(bf16)
    vecs = jnp.stack([conv_b, b_a.reshape(w), b_x.reshape(w), lam.reshape(w)])
    consts = (w_in.astype(bf16), conv_w, vecs, w_gates, w_out.astype(bf16), ln_g.reshape(1, d), ln_b.reshape(1, d))
    full = lambda arr: pl.BlockSpec(arr.shape, lambda i: (0,) * arr.ndim, pipeline_mode=pl.Buffered(1))
    tok = pl.BlockSpec((ts, d), lambda i: (i, 0))
    return pl.pallas_call(
        functools.partial(_rglru_kernel, tiles_per_seq=seq // ts),
        grid=(n // ts,),
        in_specs=[tok] + [full(c) for c in consts],
        out_specs=[tok, tok],
        out_shape=[jax.ShapeDtypeStruct((n, d), f32), jax.ShapeDtypeStruct((n, d), bf16)],
        scratch_shapes=[pltpu.VMEM((SUBLANES, w), f32), pltpu.VMEM((SUBLANES, w), f32)],
        compiler_params=pltpu.CompilerParams(
            dimension_semantics=("arbitrary",), vmem_limit_bytes=VMEM_LIMIT),
        name="rglru",
    )(x, *consts)


def _sigmoid(x):
    return 1.0 / (1.0 + jnp.exp(-x))


def _softplus(x):
    return jnp.maximum(x, 0.0) + jnp.log1p(jnp.exp(-jnp.abs(x)))


def _head_sums(t, ones_ref):
    hi = t.astype(bf16)
    lo = (t - hi.astype(f32)).astype(bf16)
    return (jnp.dot(hi, ones_ref[...], preferred_element_type=f32)
            + jnp.dot(lo, ones_ref[...], preferred_element_type=f32))


def _rwkv_proj_kernel(x_ref, xp_ref, mix_ref, wr_ref, wk_ref, wv_ref, w1_ref, w2_ref, a1_ref, a2_ref,
                      g1_ref, g2_ref, vec_ref,
                      r_ref, w_ref, k_ref, v_ref, gate_ref, g_ref, *, tiles_per_seq):
    i = pl.program_id(0)
    x = x_ref[...]
    ts = x.shape[0]
    prev_last = jnp.where(i % tiles_per_seq == 0, 0.0, xp_ref[SUBLANES - 1:SUBLANES, :])
    row = lax.broadcasted_iota(jnp.int32, x.shape, 0)
    xprev = jnp.where(row == 0, prev_last, pltpu.roll(x, 1, axis=0))
    xx = xprev - x

    def mixed(m):
        return (x + xx * mix_ref[m:m + 1, :]).astype(bf16)

    def mm(a, w_ref_):
        return jnp.dot(a, w_ref_[...], preferred_element_type=f32)

    w0, a0 = (vec_ref[n:n + 1, :] for n in range(2))
    lw = mm(jnp.tanh(mm(mixed(1), w1_ref)).astype(bf16), w2_ref)
    w_log = -_softplus(-(w0 + lw)) - 0.5
    r_ref[...] = mm(mixed(0), wr_ref).astype(bf16)
    w_ref[...] = jnp.exp(w_log).astype(bf16)
    k_ref[...] = mm(mixed(2), wk_ref).astype(bf16)
    v_ref[...] = mm(mixed(3), wv_ref).astype(bf16)
    gate_ref[...] = _sigmoid(a0 + mm(mm(mixed(4), a1_ref).astype(bf16), a2_ref)).astype(bf16)
    g_ref[...] = mm(_sigmoid(mm(mixed(5), g1_ref)).astype(bf16), g2_ref)


def rwkv_proj(x, seq, mix, w_r, w_k, w_v, w1, w2, a1, a2, g1, g2, vecs, *, ts):
    n, d = x.shape
    full = lambda arr: pl.BlockSpec(arr.shape, lambda i: (0,) * arr.ndim)
    tok = pl.BlockSpec((ts, d), lambda i: (i, 0))
    prev = pl.BlockSpec((SUBLANES, d), lambda i: (jnp.maximum(i * (ts // SUBLANES) - 1, 0), 0))
    weights = (mix, w_r, w_k, w_v, w1, w2, a1, a2, g1, g2, vecs)
    out = lambda dt: jax.ShapeDtypeStruct((n, d), dt)
    return pl.pallas_call(
        functools.partial(_rwkv_proj_kernel, tiles_per_seq=seq // ts),
        grid=(n // ts,),
        in_specs=[tok, prev] + [full(w) for w in weights],
        out_specs=[tok] * 6,
        out_shape=[out(bf16), out(bf16), out(bf16), out(bf16), out(bf16), out(f32)],
        compiler_params=pltpu.CompilerParams(
            dimension_semantics=("arbitrary",), vmem_limit_bytes=VMEM_LIMIT),
        name="rwkv_proj",
    )(x, x, *weights)


def _rwkv_scan_kernel(rin_ref, win_ref, k_ref, vin_ref, gate_ref, knext_ref, kk_ref, ka_ref, o_ref,
                      s_ref, sa_ref, a_ref, b_ref, km_ref, r_ref, v_ref, w_ref):
    nk = s_ref.shape[0]
    tc = rin_ref.shape[0]
    r_ref[...] = rin_ref[...].astype(f32)
    v_ref[...] = vin_ref[...].astype(f32)
    w_ref[...] = jnp.exp(-win_ref[...].astype(f32))

    @pl.when(pl.program_id(0) == 0)
    def _():
        s_ref[...] = jnp.zeros_like(s_ref)
        sa_ref[...] = jnp.zeros_like(sa_ref)

    def unit_key(k):
        kk = k * kk_ref[...]
        norm = jnp.sqrt(jnp.sum(kk * kk, axis=-2, keepdims=True))
        return kk / jnp.maximum(norm, RWKV_L2_EPS)

    k_blk = k_ref[...].astype(f32)
    gate = gate_ref[...].astype(f32)
    kk_blk = unit_key(k_blk)
    a_ref[0:tc] = -kk_blk
    a_ref[tc:tc + 1] = -unit_key(knext_ref[...].astype(f32))
    b_ref[...] = kk_blk * gate
    km_ref[...] = k_blk * (1.0 + (gate - 1.0) * ka_ref[...])

    def step(t, sa):
        vt = v_ref[t]
        o = [jnp.zeros(s_ref.shape[1:], f32)] * 2
        nsa = [jnp.zeros(s_ref.shape[1:], f32)] * 2
        for kk in range(nk):
            new = (s_ref[kk] * w_ref[t, kk:kk + 1, :]
                   + (sa * b_ref[t, kk:kk + 1, :] + vt * km_ref[t, kk:kk + 1, :]))
            s_ref[kk] = new
            o[kk % 2] = o[kk % 2] + new * r_ref[t, kk:kk + 1, :]
            nsa[kk % 2] = nsa[kk % 2] + new * a_ref[t + 1, kk:kk + 1, :]
        ot = o[0] + o[1]
        mu = jnp.mean(ot, axis=0, keepdims=True)
        oc = ot - mu
        var = jnp.mean(oc * oc, axis=0, keepdims=True)
        o_ref[t] = (oc * lax.rsqrt(var + RWKV_GN_EPS)).astype(o_ref.dtype)
        return nsa[0] + nsa[1]

    sa_ref[...] = lax.fori_loop(0, tc, step, sa_ref[...])


def rwkv_scan(r, w, k, v, gate, kk_vec, ka_vec, *, tc):
    s, hd, chains = r.shape
    blk = pl.BlockSpec((tc, hd, chains), lambda i: (i, 0, 0))
    nxt = pl.BlockSpec((1, hd, chains), lambda i: (jnp.minimum((i + 1) * tc, s - 1), 0, 0))
    vec = pl.BlockSpec((hd, chains), lambda i: (0, 0))
    return pl.pallas_call(
        _rwkv_scan_kernel,
        grid=(s // tc,),
        in_specs=[blk] * 5 + [nxt, vec, vec],
        out_specs=blk,
        out_shape=jax.ShapeDtypeStruct((s, hd, chains), bf16),
        scratch_shapes=[pltpu.VMEM((hd, hd, chains), f32), pltpu.VMEM((hd, chains), f32),
                        pltpu.VMEM((tc + 1, hd, chains), f32)] + [pltpu.VMEM((tc, hd, chains), f32)] * 5,
        compiler_params=pltpu.CompilerParams(
            dimension_semantics=("arbitrary",), vmem_limit_bytes=VMEM_LIMIT),
        name="rwkv_scan",
    )(r, w, k, v, gate, k, kk_vec, ka_vec)


def _rwkv_out_kernel(x_ref, o_ref, r_ref, k_ref, gate_ref, v_ref, g_ref, vec_ref, ones_ref, wo_ref,
                     lng_ref, lnb_ref, out_ref, outb_ref):
    lnx_g, lnx_b, r_k, k_a = (vec_ref[n:n + 1, :] for n in range(4))
    widen = lambda ref: ref[...].astype(f32)
    k_mod = widen(k_ref) * (1.0 + (widen(gate_ref) - 1.0) * k_a)
    bonus = _head_sums(widen(r_ref) * k_mod * r_k, ones_ref) * widen(v_ref)
    y = (widen(o_ref) * lnx_g + lnx_b + bonus) * g_ref[...]
    m = jnp.dot(y.astype(bf16), wo_ref[...], preferred_element_type=f32)
    out = _layer_norm_rows(ALPHA * x_ref[...] + m, lng_ref[...], lnb_ref[...])
    out_ref[...] = out
    outb_ref[...] = out.astype(bf16)


def rwkv_out(x, o, r, k, gate, v, g, vecs, ones, w_o, ln_g, ln_b, *, ts):
    n, d = x.shape
    full = lambda arr: pl.BlockSpec(arr.shape, lambda i: (0,) * arr.ndim)
    tok = pl.BlockSpec((ts, d), lambda i: (i, 0))
    consts = (vecs, ones, w_o, ln_g.reshape(1, d), ln_b.reshape(1, d))
    return pl.pallas_call(
        _rwkv_out_kernel,
        grid=(n // ts,),
        in_specs=[tok] * 7 + [full(c) for c in consts],
        out_specs=[tok, tok],
        out_shape=[jax.ShapeDtypeStruct((n, d), f32), jax.ShapeDtypeStruct((n, d), bf16)],
        compiler_params=pltpu.CompilerParams(
            dimension_semantics=("arbitrary",), vmem_limit_bytes=VMEM_LIMIT),
        name="rwkv_out",
    )(x, o, r, k, gate, v, g, *consts)


def _head_ones(d, head):
    seg = jnp.arange(d) // head
    return (seg[:, None] == seg[None, :]).astype(bf16)


def rwkv_layer(x, seq, mix, w_r, w_k, w_v, w0, w1, w2, a0, a1, a2, g1, g2, k_k, k_a, r_k, lnx_g, lnx_b, w_o,
               ln_g, ln_b, *, ts=256, tc=16):
    n, d = x.shape
    bsz = n // seq
    nh, hd = RWKV_HEADS, RWKV_HEAD
    ones = _head_ones(d, hd)
    c = lambda w: w.astype(bf16)
    r, w, k, v, gate, g = rwkv_proj(x, seq, mix, c(w_r), c(w_k), c(w_v), c(w1), c(w2), c(a1), c(a2),
                                    c(g1), c(g2), jnp.stack([w0, a0]), ts=ts)
    to_chains = lambda t: t.reshape(bsz, seq, nh, hd).transpose(1, 3, 0, 2).reshape(seq, hd, bsz * nh)
    vec_chains = lambda t: jnp.tile(t.reshape(nh, hd).T, (1, bsz))
    o = rwkv_scan(*(to_chains(t) for t in (r, w, k, v, gate)), vec_chains(k_k), vec_chains(k_a), tc=tc)
    o = o.reshape(seq, hd, bsz, nh).transpose(2, 0, 3, 1).reshape(n, d)
    vecs_out = jnp.stack([lnx_g, lnx_b, r_k.reshape(d), k_a])
    return rwkv_out(x, o, r, k, gate, v, g, vecs_out, ones, c(w_o), ln_g, ln_b, ts=ts)


def kernel(x, rg_w_in, rg_conv_w, rg_conv_b, rg_w_a, rg_b_a, rg_w_x, rg_b_x, rg_lambda, rg_w_out, rw_mix, rw_w_r, rw_w_k, rw_w_v, rw_w0, rw_w1, rw_w2, rw_a0, rw_a1, rw_a2, rw_g1, rw_g2, rw_k_k, rw_k_a, rw_r_k, rw_lnx_g, rw_lnx_b, rw_w_o, peer_w_q, peer_sub_keys, peer_u, peer_v, ln_g, ln_b):
    bsz, s, d = x.shape
    x = x.reshape(bsz * s, d)
    for i in range(DEPTH):
        j = i // 2
        if i % 2 == 0:
            x1, x1b = rglru_layer(x, s, rg_w_in[j], rg_conv_w[j], rg_conv_b[j], rg_w_a[j], rg_b_a[j],
                                  rg_w_x[j], rg_b_x[j], rg_lambda[j], rg_w_out[j], ln_g[i, 0], ln_b[i, 0])
        else:
            x1, x1b = rwkv_layer(x, s, rw_mix[j], rw_w_r[j], rw_w_k[j], rw_w_v[j], rw_w0[j], rw_w1[j],
                                 rw_w2[j], rw_a0[j], rw_a1[j], rw_a2[j], rw_g1[j], rw_g2[j],
                                 rw_k_k[j], rw_k_a[j], rw_r_k[j], rw_lnx_g[j], rw_lnx_b[j], rw_w_o[j],
                                 ln_g[i, 0], ln_b[i, 0])
        x = peer_layer(x1, x1b, peer_w_q[i], peer_sub_keys[i], peer_u[i], peer_v[i], ln_g[i, 1], ln_b[i, 1])
    return x.reshape(bsz, s, d)
```

```python
import functools
import math
import jax, jax.numpy as jnp
from jax import lax
from jax.experimental import pallas as pl
from jax.experimental.pallas import tpu as pltpu

DEPTH = 2
RG_WIDTH = 1408
RG_BLOCK = 88
RG_CONV = 4
RG_C = 8.0
RWKV_HEAD = 64
RWKV_HEADS = 16
RWKV_GN_EPS = 64e-5
RWKV_L2_EPS = 1e-12
PEER_HEADS = 8
PEER_NKEYS = 128
PEER_DHALF = 128
PEER_TOPK = 16
ALPHA = (2 * DEPTH) ** 0.25
LN_EPS = 1e-5

LANES = 128
SUBLANES = 8
MXU_COLS = 256
VMEM_LIMIT = 56 << 20

f32 = jnp.float32
bf16 = jnp.bfloat16
u32 = jnp.uint32

RG_BAND = 3 * LANES


def _rg_band_starts():
    n_tiles = RG_WIDTH // LANES
    starts = [LANES * min(max(j - 1, 0), n_tiles - RG_BAND // LANES) for j in range(n_tiles)]
    for j, lo in enumerate(starts):
        first_head, last_head = (j * LANES) // RG_BLOCK, ((j + 1) * LANES - 1) // RG_BLOCK
        assert lo <= first_head * RG_BLOCK and (last_head + 1) * RG_BLOCK <= lo + RG_BAND
    return starts


def _gelu_tanh(x):
    return 0.5 * x * (1.0 + jnp.tanh(math.sqrt(2.0 / math.pi) * (x + 0.044715 * (x * x * x))))


def _layer_norm_rows(z, g, b):
    mu = jnp.mean(z, axis=-1, keepdims=True)
    zc = z - mu
    var = jnp.mean(zc * zc, axis=-1, keepdims=True)
    return zc * lax.rsqrt(var + LN_EPS) * g + b


def _top_rows(s, k, with_rank=False):
    rows = []
    rank = jnp.full(s.shape, float(k), f32)
    for j in range(k):
        m = jnp.max(s, axis=0, keepdims=True)
        rows.append(m)
        hit = s == m
        if with_rank:
            rank = jnp.where(hit, float(j), rank)
        s = jnp.where(hit, -jnp.inf, s)
    return (rows, rank) if with_rank else rows


def _peer_select_kernel(x_ref, wq_ref, keys_ref, cnt_ref, rk1_ref, p0_ref, p1_ref, q_ref):
    h = pl.program_id(1)

    @pl.when(h == 0)
    def _():
        q_ref[...] = jnp.dot(x_ref[...], wq_ref[...], preferred_element_type=f32)

    def scores(p):
        off = pl.multiple_of((2 * h + p) * PEER_DHALF, PEER_DHALF)
        qhp = q_ref[:, pl.ds(off, PEER_DHALF)].astype(bf16)
        return lax.dot_general(keys_ref[0, p], qhp, (((1,), (1,)), ((), ())),
                               preferred_element_type=f32)

    def select_all(stable):
        s0_all = scores(0)
        s1_all = scores(1)
        tied = 0.0
        for c in range(s0_all.shape[1] // LANES):
            lanes = slice(c * LANES, (c + 1) * LANES)
            cnt, rank1, p0, p1, tied_c = _select_slab(s0_all[:, lanes], s1_all[:, lanes], stable=stable)
            cnt_ref[0, c] = _bf16_twice(cnt)
            p0_ref[0, c] = _bf16_twice(p0)
            rk1_ref[0, c] = pltpu.bitcast(rank1.astype(bf16), u32)
            p1_ref[0, c] = pltpu.bitcast(p1.astype(bf16), u32)
            if not stable:
                tied = jnp.maximum(tied, jnp.max(tied_c))
        return tied

    tied = select_all(stable=False)

    @pl.when(tied > 0.0)
    def _():
        select_all(stable=True)


def _top_rows_stable(s, k):
    rows = []
    idx = lax.broadcasted_iota(jnp.int32, s.shape, 0)
    rank = jnp.full(s.shape, float(k), f32)
    for j in range(k):
        m = jnp.max(s, axis=0, keepdims=True)
        rows.append(m)
        first = jnp.min(jnp.where(s == m, idx, s.shape[0]), axis=0, keepdims=True)
        pick = idx == first
        rank = jnp.where(pick, float(j), rank)
        s = jnp.where(pick, -jnp.inf, s)
    return rows, rank


def _select_slab(s0, s1, *, stable):
    if stable:
        a, rank0 = _top_rows_stable(s0, PEER_TOPK)
        b, rank1 = _top_rows_stable(s1, PEER_TOPK)
    else:
        a = _top_rows(s0, PEER_TOPK)
        b, rank1 = _top_rows(s1, PEER_TOPK, with_rank=True)
    row = lax.broadcasted_iota(jnp.int32, (PEER_TOPK, s0.shape[1]), 0)
    bmat = jnp.zeros((PEER_TOPK, s0.shape[1]), f32)
    for j in range(PEER_TOPK):
        bmat = jnp.where(row == j, b[j], bmat)
    row8 = lax.broadcasted_iota(jnp.int32, (SUBLANES, s0.shape[1]), 0)
    ahi = jnp.zeros((SUBLANES, s0.shape[1]), f32)
    for n in range(SUBLANES):
        ahi = jnp.where(row8 == n, a[SUBLANES + n], ahi)
    cand = jnp.concatenate([a[0] + bmat] + [a[i] + bmat[:SUBLANES] for i in range(1, SUBLANES)]
                           + [ahi + b[0]], axis=0)
    rows_of = [PEER_TOPK] + [SUBLANES] * (SUBLANES - 1) + [1] * SUBLANES
    if stable:
        picked = _top_rows_stable(cand, PEER_TOPK)[1] < float(PEER_TOPK)
    else:
        picked = cand >= _top_rows(cand, PEER_TOPK)[-1]
    picked = jnp.where(picked, 1.0, 0.0)
    cmax = a[0] + b[0]
    z = jnp.sum(picked * jnp.exp(cand - cmax), axis=0, keepdims=True)
    cnt = jnp.zeros_like(s0)
    total = jnp.zeros_like(z)
    lo = 0
    for i in range(PEER_TOPK):
        hi = lo + rows_of[i]
        cnt_i = jnp.sum(picked[lo:hi], axis=0, keepdims=True)
        lo = hi
        total = total + cnt_i
        cnt = jnp.where((rank0 == float(i)) if stable else (s0 == a[i]), cnt_i, cnt)
    p0 = jnp.exp(s0 - a[0]) / z
    p1 = jnp.exp(s1 - b[0])
    if stable:
        return cnt, rank1, p0, p1, None
    k = float(PEER_TOPK)
    n0 = jnp.sum(jnp.where(s0 >= a[-1], 1.0, 0.0), axis=0, keepdims=True)
    n1 = jnp.sum(jnp.where(rank1 < k, 1.0, 0.0), axis=0, keepdims=True)
    tied = jnp.abs(n0 - k) + jnp.abs(n1 - k) + jnp.abs(total - k)
    return cnt, rank1, p0, p1, tied


def _bf16_twice(t):
    bits = pltpu.bitcast(t.astype(bf16).astype(f32), u32) >> 16
    return bits | (bits << 16)


def peer_select(x, wq_b, keys_b, *, tq):
    n, d = x.shape
    nh = PEER_HEADS
    full_spec = pl.BlockSpec((1, tq // LANES, PEER_NKEYS, LANES), lambda i, h: (h, i, 0, 0))
    half_spec = pl.BlockSpec((1, tq // LANES, PEER_NKEYS // 2, LANES), lambda i, h: (h, i, 0, 0))
    full = jax.ShapeDtypeStruct((nh, n // LANES, PEER_NKEYS, LANES), u32)
    half = jax.ShapeDtypeStruct((nh, n // LANES, PEER_NKEYS // 2, LANES), u32)
    return pl.pallas_call(
        _peer_select_kernel,
        grid=(n // tq, nh),
        in_specs=[pl.BlockSpec((tq, d), lambda i, h: (i, 0)),
                  pl.BlockSpec((d, 2 * nh * PEER_DHALF), lambda i, h: (0, 0)),
                  pl.BlockSpec((1, 2, PEER_NKEYS, PEER_DHALF), lambda i, h: (h, 0, 0, 0))],
        out_specs=[full_spec, half_spec, full_spec, half_spec],
        out_shape=[full, half, full, half],
        scratch_shapes=[pltpu.VMEM((tq, 2 * nh * PEER_DHALF), f32)],
        compiler_params=pltpu.CompilerParams(
            dimension_semantics=("arbitrary", "arbitrary"), vmem_limit_bytes=VMEM_LIMIT),
        name="peer_select",
    )(x, wq_b, keys_b)


def _peer_main_kernel(xb_ref, x_ref, cnt_ref, rk1_ref, p0_ref, p1_ref, u_ref, vt_ref,
                      g_ref, b_ref, o_ref, acc_ref, *slab_refs, te, tt):
    j = pl.program_id(1)
    n_i0 = te // PEER_NKEYS
    n_slab = tt // LANES
    per_chunk = MXU_COLS // LANES
    at_refs, ga_refs = slab_refs[:n_slab], slab_refs[n_slab:]
    half = PEER_NKEYS // 2
    unpack = lambda words: pltpu.bitcast(words, bf16)

    @pl.when(j == 0)
    def _():
        acc_ref[...] = jnp.zeros_like(acc_ref)

    def act_chunk(c):
        at = lax.dot_general(u_ref[...], xb_ref[c * MXU_COLS:(c + 1) * MXU_COLS, :],
                             (((1,), (1,)), ((), ())), preferred_element_type=f32)
        for k in range(per_chunk):
            at_refs[c * per_chunk + k][...] = at[:, k * LANES:(k + 1) * LANES]

    def gate_slab(s):
        for il in range(n_i0):
            g = jnp.zeros((PEER_NKEYS, LANES), bf16)
            for h in range(PEER_HEADS):
                cntrow = jnp.broadcast_to(cnt_ref[h, s, il:il + 1, :], (half, LANES))
                p0row = jnp.broadcast_to(p0_ref[h, s, il:il + 1, :], (half, LANES))
                sel = unpack(rk1_ref[h, s]) < unpack(cntrow)
                g = g + jnp.where(sel, unpack(p1_ref[h, s]) * unpack(p0row), jnp.zeros((), bf16))
            ga = _gelu_tanh(at_refs[s][il * PEER_NKEYS:(il + 1) * PEER_NKEYS, :].astype(bf16)) * g
            ga_refs[s][il * half:(il + 1) * half, :] = pltpu.bitcast(ga, u32)

    def out_chunk(c):
        ga = jnp.concatenate([unpack(ga_refs[c * per_chunk + k][...]) for k in range(per_chunk)], axis=1)
        lanes = slice(c * MXU_COLS, (c + 1) * MXU_COLS)
        acc_ref[:, lanes] += jnp.dot(vt_ref[...], ga, preferred_element_type=f32)

    n_chunk = tt // MXU_COLS
    act_chunk(0)
    for c in range(n_chunk):
        gate_slab(c * per_chunk)
        if c + 1 < n_chunk:
            act_chunk(c + 1)
        for k in range(1, per_chunk):
            gate_slab(c * per_chunk + k)
        out_chunk(c)

    @pl.when(j == pl.num_programs(1) - 1)
    def _():
        z = ALPHA * x_ref[...] + acc_ref[...].T
        o_ref[...] = _layer_norm_rows(z, g_ref[...], b_ref[...])


def peer_main(x, xb, stats, u_b, vt_b, ln_g, ln_b, *, tt, te):
    n, d = x.shape
    e = u_b.shape[0]
    nh = PEER_HEADS
    cnt, rk1, p0, p1 = stats
    full_spec = pl.BlockSpec((nh, tt // LANES, te // PEER_NKEYS, LANES), lambda i, j: (0, i, j, 0))
    half_spec = pl.BlockSpec((nh, tt // LANES, PEER_NKEYS // 2, LANES), lambda i, j: (0, i, 0, 0))
    return pl.pallas_call(
        functools.partial(_peer_main_kernel, te=te, tt=tt),
        grid=(n // tt, e // te),
        in_specs=[pl.BlockSpec((tt, d), lambda i, j: (i, 0)),
                  pl.BlockSpec((tt, d), lambda i, j: (i, 0)),
                  full_spec, half_spec, full_spec, half_spec,
                  pl.BlockSpec((te, d), lambda i, j: (j, 0)),
                  pl.BlockSpec((d, te), lambda i, j: (0, j)),
                  pl.BlockSpec((1, d), lambda i, j: (0, 0)),
                  pl.BlockSpec((1, d), lambda i, j: (0, 0))],
        out_specs=pl.BlockSpec((tt, d), lambda i, j: (i, 0)),
        out_shape=jax.ShapeDtypeStruct((n, d), f32),
        scratch_shapes=([pltpu.VMEM((d, tt), f32)]
                        + [pltpu.VMEM((te, LANES), f32)] * (tt // LANES)
                        + [pltpu.VMEM((te // 2, LANES), u32)] * (tt // LANES)),
        compiler_params=pltpu.CompilerParams(
            dimension_semantics=("arbitrary", "arbitrary"), vmem_limit_bytes=VMEM_LIMIT),
        name="peer_main",
    )(xb, x, cnt, rk1, p0, p1, u_b, vt_b, ln_g.reshape(1, d), ln_b.reshape(1, d))


def peer_layer(x, xb, w_q, sub_keys, u, v, ln_g, ln_b, *, tq=1024, tt=1024, te=1024):
    stats = peer_select(xb, w_q.astype(bf16), sub_keys.astype(bf16), tq=tq)
    return peer_main(x, xb, stats, u.astype(bf16), v.T.astype(bf16), ln_g, ln_b, tt=tt, te=te)


def _rglru_kernel(x_ref, win_ref, cw_ref, vec_ref, wg_ref, wout_ref, lng_ref, lnb_ref, o_ref, ob_ref,
                  tail_ref, h_ref, *, tiles_per_seq):
    i = pl.program_id(0)
    w = RG_WIDTH
    x = x_ref[...]
    ts = x.shape[0]

    @pl.when(i % tiles_per_seq == 0)
    def _():
        tail_ref[...] = jnp.zeros_like(tail_ref)
        h_ref[...] = jnp.zeros_like(h_ref)

    conv_b, b_a, b_x, lam = (vec_ref[n:n + 1, :] for n in range(4))
    hin = jnp.dot(x.astype(bf16), win_ref[...], preferred_element_type=f32)
    gate_branch = _gelu_tanh(hin[:, :w])
    hx = hin[:, w:]

    row = lax.broadcasted_iota(jnp.int32, (ts, w), 0)
    row8 = lax.broadcasted_iota(jnp.int32, (SUBLANES, w), 0)
    tail = tail_ref[...]
    xc = hx * cw_ref[RG_CONV - 1:RG_CONV, :] + conv_b
    for s in range(1, RG_CONV):
        rolled = pltpu.roll(hx, s, axis=0)
        head = jnp.where(row8 < s, pltpu.roll(tail, s, axis=0), rolled[:SUBLANES])
        shifted = jnp.concatenate([head, rolled[SUBLANES:]], axis=0)
        xc = xc + shifted * cw_ref[RG_CONV - 1 - s:RG_CONV - s, :]
    tail_ref[...] = hx[ts - SUBLANES:]

    xcb = xc.astype(bf16)
    bands = [jnp.dot(xcb[:, lo:lo + RG_BAND], wg_ref[j], preferred_element_type=f32)
             for j, lo in enumerate(_rg_band_starts())]
    r = _sigmoid(jnp.concatenate([g[:, :LANES] for g in bands], axis=1) + b_a)
    ig = _sigmoid(jnp.concatenate([g[:, LANES:] for g in bands], axis=1) + b_x)
    log_a = (-RG_C * _softplus(-lam)) * r
    a = jnp.exp(log_a)
    u = jnp.sqrt(-jnp.tanh(log_a) * (a * a + 1.0)) * (ig * xc)

    sub = row & (SUBLANES - 1)
    s = 1
    while s < SUBLANES:
        keep = sub >= s
        a_sh = jnp.where(keep, pltpu.roll(a, s, axis=0), 1.0)
        u_sh = jnp.where(keep, pltpu.roll(u, s, axis=0), 0.0)
        u = a * u_sh + u
        a = a * a_sh
        s *= 2
    carry = h_ref[SUBLANES - 1:SUBLANES, :]
    groups = []
    for g in range(0, ts, SUBLANES):
        hg = u[g:g + SUBLANES] + a[g:g + SUBLANES] * carry
        groups.append(hg)
        carry = hg[SUBLANES - 1:]
    hs = jnp.concatenate(groups, axis=0)
    h_ref[...] = groups[-1]

    m = jnp.dot((hs * gate_branch).astype(bf16), wout_ref[...], preferred_element_type=f32)
    out = _layer_norm_rows(ALPHA * x + m, lng_ref[...], lnb_ref[...])
    o_ref[...] = out
    ob_ref[...] = out.astype(bf16)


def rglru_layer(x, seq, w_in, conv_w, conv_b, w_a, b_a, w_x, b_x, lam, w_out, ln_g, ln_b, *, ts=256):
    n, d = x.shape
    w = RG_WIDTH
    blockdiag = lambda t: jax.scipy.linalg.block_diag(*t)
    full_a, full_x = blockdiag(w_a), blockdiag(w_x)
    w_gates = jnp.stack([
        jnp.concatenate([m[lo:lo + RG_BAND, j * LANES:(j + 1) * LANES] for m in (full_a, full_x)], axis=1)
        for j, lo in enumerate(_rg_band_starts())]).astype(bf16)
    vecs = jnp.stack([conv_b, b_a.reshape(w), b_x.reshape(w), lam.reshape(w)])
    consts = (w_in.astype(bf16), conv_w, vecs, w_gates, w_out.astype(bf16), ln_g.reshape(1, d), ln_b.reshape(1, d))
    full = lambda arr: pl.BlockSpec(arr.shape, lambda i: (0,) * arr.ndim, pipeline_mode=pl.Buffered(1))
    tok = pl.BlockSpec((ts, d), lambda i: (i, 0))
    return pl.pallas_call(
        functools.partial(_rglru_kernel, tiles_per_seq=seq // ts),
        grid=(n // ts,),
        in_specs=[tok] + [full(c) for c in consts],
        out_specs=[tok, tok],
        out_shape=[jax.ShapeDtypeStruct((n, d), f32), jax.ShapeDtypeStruct((n, d), bf16)],
        scratch_shapes=[pltpu.VMEM((SUBLANES, w), f32), pltpu.VMEM((SUBLANES, w), f32)],
        compiler_params=pltpu.CompilerParams(
            dimension_semantics=("arbitrary",), vmem_limit_bytes=VMEM_LIMIT),
        name="rglru",
    )(x, *consts)


def _sigmoid(x):
    return 1.0 / (1.0 + jnp.exp(-x))


def _softplus(x):
    return jnp.maximum(x, 0.0) + jnp.log1p(jnp.exp(-jnp.abs(x)))


def _head_sums(t, ones_ref):
    hi = t.astype(bf16)
    lo = (t - hi.astype(f32)).astype(bf16)
    return (jnp.dot(hi, ones_ref[...], preferred_element_type=f32)
            + jnp.dot(lo, ones_ref[...], preferred_element_type=f32))


def _rwkv_proj_kernel(x_ref, xp_ref, mix_ref, wr_ref, wk_ref, wv_ref, w1_ref, w2_ref, a1_ref, a2_ref,
                      g1_ref, g2_ref, vec_ref,
                      r_ref, w_ref, k_ref, v_ref, gate_ref, g_ref, *, tiles_per_seq):
    i = pl.program_id(0)
    x = x_ref[...]
    ts = x.shape[0]
    prev_last = jnp.where(i % tiles_per_seq == 0, 0.0, xp_ref[SUBLANES - 1:SUBLANES, :])
    row = lax.broadcasted_iota(jnp.int32, x.shape, 0)
    xprev = jnp.where(row == 0, prev_last, pltpu.roll(x, 1, axis=0))
    xx = xprev - x

    def mixed(m):
        return (x + xx * mix_ref[m:m + 1, :]).astype(bf16)

    def mm(a, w_ref_):
        return jnp.dot(a, w_ref_[...], preferred_element_type=f32)

    w0, a0 = (vec_ref[n:n + 1, :] for n in range(2))
    lw = mm(jnp.tanh(mm(mixed(1), w1_ref)).astype(bf16), w2_ref)
    w_log = -_softplus(-(w0 + lw)) - 0.5
    r_ref[...] = mm(mixed(0), wr_ref).astype(bf16)
    w_ref[...] = jnp.exp(w_log).astype(bf16)
    k_ref[...] = mm(mixed(2), wk_ref).astype(bf16)
    v_ref[...] = mm(mixed(3), wv_ref).astype(bf16)
    gate_ref[...] = _sigmoid(a0 + mm(mm(mixed(4), a1_ref).astype(bf16), a2_ref)).astype(bf16)
    g_ref[...] = mm(_sigmoid(mm(mixed(5), g1_ref)).astype(bf16), g2_ref)


def rwkv_proj(x, seq, mix, w_r, w_k, w_v, w1, w2, a1, a2, g1, g2, vecs, *, ts):
    n, d = x.shape
    full = lambda arr: pl.BlockSpec(arr.shape, lambda i: (0,) * arr.ndim)
    tok = pl.BlockSpec((ts, d), lambda i: (i, 0))
    prev = pl.BlockSpec((SUBLANES, d), lambda i: (jnp.maximum(i * (ts // SUBLANES) - 1, 0), 0))
    weights = (mix, w_r, w_k, w_v, w1, w2, a1, a2, g1, g2, vecs)
    out = lambda dt: jax.ShapeDtypeStruct((n, d), dt)
    return pl.pallas_call(
        functools.partial(_rwkv_proj_kernel, tiles_per_seq=seq // ts),
        grid=(n // ts,),
        in_specs=[tok, prev] + [full(w) for w in weights],
        out_specs=[tok] * 6,
        out_shape=[out(bf16), out(bf16), out(bf16), out(bf16), out(bf16), out(f32)],
        compiler_params=pltpu.CompilerParams(
            dimension_semantics=("arbitrary",), vmem_limit_bytes=VMEM_LIMIT),
        name="rwkv_proj",
    )(x, x, *weights)


def _rwkv_scan_kernel(rin_ref, win_ref, k_ref, vin_ref, gate_ref, knext_ref, kk_ref, ka_ref, o_ref,
                      s_ref, sa_ref, a_ref, b_ref, km_ref, r_ref, v_ref, w_ref):
    nk = s_ref.shape[0]
    tc = rin_ref.shape[0]
    r_ref[...] = rin_ref[...].astype(f32)
    v_ref[...] = vin_ref[...].astype(f32)
    w_ref[...] = jnp.exp(-win_ref[...].astype(f32))

    @pl.when(pl.program_id(0) == 0)
    def _():
        s_ref[...] = jnp.zeros_like(s_ref)
        sa_ref[...] = jnp.zeros_like(sa_ref)

    def unit_key(k):
        kk = k * kk_ref[...]
        norm = jnp.sqrt(jnp.sum(kk * kk, axis=-2, keepdims=True))
        return kk / jnp.maximum(norm, RWKV_L2_EPS)

    k_blk = k_ref[...].astype(f32)
    gate = gate_ref[...].astype(f32)
    kk_blk = unit_key(k_blk)
    a_ref[0:tc] = -kk_blk
    a_ref[tc:tc + 1] = -unit_key(knext_ref[...].astype(f32))
    b_ref[...] = kk_blk * gate
    km_ref[...] = k_blk * (1.0 + (gate - 1.0) * ka_ref[...])

    def step(t, sa):
        vt = v_ref[t]
        o = [jnp.zeros(s_ref.shape[1:], f32)] * 2
        nsa = [jnp.zeros(s_ref.shape[1:], f32)] * 2
        for kk in range(nk):
            new = (s_ref[kk] * w_ref[t, kk:kk + 1, :]
                   + (sa * b_ref[t, kk:kk + 1, :] + vt * km_ref[t, kk:kk + 1, :]))
            s_ref[kk] = new
            o[kk % 2] = o[kk % 2] + new * r_ref[t, kk:kk + 1, :]
            nsa[kk % 2] = nsa[kk % 2] + new * a_ref[t + 1, kk:kk + 1, :]
        ot = o[0] + o[1]
        mu = jnp.mean(ot, axis=0, keepdims=True)
        oc = ot - mu
        var = jnp.mean(oc * oc, axis=0, keepdims=True)
        o_ref[t] = (oc * lax.rsqrt(var + RWKV_GN_EPS)).astype(o_ref.dtype)
        return nsa[0] + nsa[1]

    sa_ref[...] = lax.fori_loop(0, tc, step, sa_ref[...])


def rwkv_scan(r, w, k, v, gate, kk_vec, ka_vec, *, tc):
    s, hd, chains = r.shape
    blk = pl.BlockSpec((tc, hd, chains), lambda i: (i, 0, 0))
    nxt = pl.BlockSpec((1, hd, chains), lambda i: (jnp.minimum((i + 1) * tc, s - 1), 0, 0))
    vec = pl.BlockSpec((hd, chains), lambda i: (0, 0))
    return pl.pallas_call(
        _rwkv_scan_kernel,
        grid=(s // tc,),
        in_specs=[blk] * 5 + [nxt, vec, vec],
        out_specs=blk,
        out_shape=jax.ShapeDtypeStruct((s, hd, chains), bf16),
        scratch_shapes=[pltpu.VMEM((hd, hd, chains), f32), pltpu.VMEM((hd, chains), f32),
                        pltpu.VMEM((tc + 1, hd, chains), f32)] + [pltpu.VMEM((tc, hd, chains), f32)] * 5,
        compiler_params=pltpu.CompilerParams(
            dimension_semantics=("arbitrary",), vmem_limit_bytes=VMEM_LIMIT),
        name="rwkv_scan",
    )(r, w, k, v, gate, k, kk_vec, ka_vec)


def _rwkv_out_kernel(x_ref, o_ref, r_ref, k_ref, gate_ref, v_ref, g_ref, vec_ref, ones_ref, wo_ref,
                     lng_ref, lnb_ref, out_ref, outb_ref):
    lnx_g, lnx_b, r_k, k_a = (vec_ref[n:n + 1, :] for n in range(4))
    widen = lambda ref: ref[...].astype(f32)
    k_mod = widen(k_ref) * (1.0 + (widen(gate_ref) - 1.0) * k_a)
    bonus = _head_sums(widen(r_ref) * k_mod * r_k, ones_ref) * widen(v_ref)
    y = (widen(o_ref) * lnx_g + lnx_b + bonus) * g_ref[...]
    m = jnp.dot(y.astype(bf16), wo_ref[...], preferred_element_type=f32)
    out = _layer_norm_rows(ALPHA * x_ref[...] + m, lng_ref[...], lnb_ref[...])
    out_ref[...] = out
    outb_ref[...] = out.astype(bf16)


def rwkv_out(x, o, r, k, gate, v, g, vecs, ones, w_o, ln_g, ln_b, *, ts):
    n, d = x.shape
    full = lambda arr: pl.BlockSpec(arr.shape, lambda i: (0,) * arr.ndim)
    tok = pl.BlockSpec((ts, d), lambda i: (i, 0))
    consts = (vecs, ones, w_o, ln_g.reshape(1, d), ln_b.reshape(1, d))
    return pl.pallas_call(
        _rwkv_out_kernel,
        grid=(n // ts,),
        in_specs=[tok] * 7 + [full(c) for c in consts],
        out_specs=[tok, tok],
        out_shape=[jax.ShapeDtypeStruct((n, d), f32), jax.ShapeDtypeStruct((n, d), bf16)],
        compiler_params=pltpu.CompilerParams(
            dimension_semantics=("arbitrary",), vmem_limit_bytes=VMEM_LIMIT),
        name="rwkv_out",
    )(x, o, r, k, gate, v, g, *consts)


def _head_ones(d, head):
    seg = jnp.arange(d) // head
    return (seg[:, None] == seg[None, :]).astype(bf16)


def rwkv_layer(x, seq, mix, w_r, w_k, w_v, w0, w1, w2, a0, a1, a2, g1, g2, k_k, k_a, r_k, lnx_g, lnx_b, w_o,
               ln_g, ln_b, *, ts=256, tc=16):
    n, d = x.shape
    bsz = n // seq
    nh, hd = RWKV_HEADS, RWKV_HEAD
    ones = _head_ones(d, hd)
    c = lambda w: w.astype(bf16)
    r, w, k, v, gate, g = rwkv_proj(x, seq, mix, c(w_r), c(w_k), c(w_v), c(w1), c(w2), c(a1), c(a2),
                                    c(g1), c(g2), jnp.stack([w0, a0]), ts=ts)
    to_chains = lambda t: t.reshape(bsz, seq, nh, hd).transpose(1, 3, 0, 2).reshape(seq, hd, bsz * nh)
    vec_chains = lambda t: jnp.tile(t.reshape(nh, hd).T, (1, bsz))
    o = rwkv_scan(*(to_chains(t) for t in (r, w, k, v, gate)), vec_chains(k_k), vec_chains(k_a), tc=tc)
    o = o.reshape(seq, hd, bsz, nh).transpose(2, 0, 3, 1).reshape(n, d)
    vecs_out = jnp.stack([lnx_g, lnx_b, r_k.reshape(d), k_a])
    return rwkv_out(x, o, r, k, gate, v, g, vecs_out, ones, c(w_o), ln_g, ln_b, ts=ts)


def kernel(x, rg_w_in, rg_conv_w, rg_conv_b, rg_w_a, rg_b_a, rg_w_x, rg_b_x, rg_lambda, rg_w_out, rw_mix, rw_w_r, rw_w_k, rw_w_v, rw_w0, rw_w1, rw_w2, rw_a0, rw_a1, rw_a2, rw_g1, rw_g2, rw_k_k, rw_k_a, rw_r_k, rw_lnx_g, rw_lnx_b, rw_w_o, peer_w_q, peer_sub_keys, peer_u, peer_v, ln_g, ln_b):
    bsz, s, d = x.shape
    x = x.reshape(bsz * s, d)
    for i in range(DEPTH):
        j = i // 2
        if i % 2 == 0:
            x1, x1b = rglru_layer(x, s, rg_w_in[j], rg_conv_w[j], rg_conv_b[j], rg_w_a[j], rg_b_a[j],
                                  rg_w_x[j], rg_b_x[j], rg_lambda[j], rg_w_out[j], ln_g[i, 0], ln_b[i, 0])
        else:
            x1, x1b = rwkv_layer(x, s, rw_mix[j], rw_w_r[j], rw_w_k[j], rw_w_v[j], rw_w0[j], rw_w1[j],
                                 rw_w2[j], rw_a0[j], rw_a1[j], rw_a2[j], rw_g1[j], rw_g2[j],
                                 rw_k_k[j], rw_k_a[j], rw_r_k[j], rw_lnx_g[j], rw_lnx_b[j], rw_w_o[j],
                                 ln_g[i, 0], ln_b[i, 0])
        x = peer_layer(x1, x1b, peer_w_q[i], peer_sub_keys[i], peer_u[i], peer_v[i], ln_g[i, 1], ln_b[i, 1])
    return x.reshape(bsz, s, d)
```
